```python
import jax
import jax.numpy as jnp
from jax import lax
import numpy as np

D_MODEL = 2048
BATCH = 4
SEQ = 4096
DEPTH = 2

CTX_LEN = 256
GRID_W = 64
NORM_EPS = 1e-6

N_HEADS = 16
N_KV_HEADS = 4
HEAD_DIM = D_MODEL // 32
WINDOW = 128
ATTN_BLOCK = 128
ROPE_BASE = 10000.0

FOURIER_GROUPS = 4
FOURIER_GROUP_DIM = D_MODEL // 8

MLSTM_HEADS = 8
MLSTM_DK = D_MODEL // 32
MLSTM_DV = D_MODEL // 16
MLSTM_CHUNK = 64

N_BRANCHES = 3

N_GROUPS = 4
EXPERTS_PER_GROUP = 8
N_EXPERTS = N_GROUPS * EXPERTS_PER_GROUP
TOP_K = 2
EXPERT_FF = D_MODEL // 2
MOE_BLOCK = 128

ATTN_Q = N_HEADS * HEAD_DIM
ATTN_KV = N_KV_HEADS * HEAD_DIM
ML_QK = MLSTM_HEADS * MLSTM_DK
ML_V = MLSTM_HEADS * MLSTM_DV
ML_GATES = 4 * MLSTM_HEADS
FOURIER_W = FOURIER_GROUPS * FOURIER_GROUP_DIM
PROJ_SPLITS = (ATTN_Q, ATTN_KV, ATTN_KV, ML_QK, ML_QK, ML_V, ML_V, ML_GATES, FOURIER_W, N_BRANCHES * D_MODEL)
PROJ_DIM = sum(PROJ_SPLITS)

kernel_name = 'hybrid_gated_flow_block'


def rmsnorm(x, g):
    xf = x.astype(jnp.float32)
    y = xf * lax.rsqrt(jnp.mean(xf * xf, axis=-1, keepdims=True) + NORM_EPS)
    return (y * g.astype(jnp.float32)).astype(x.dtype)


def split_cols(a):
    idx = [int(i) for i in np.cumsum(PROJ_SPLITS)[:-1]]
    return jnp.split(a, idx, axis=-1)


def grid_positions(n_tok):
    n_rows = n_tok // GRID_W
    row = jnp.repeat(jnp.arange(n_rows, dtype=jnp.float32), GRID_W)
    col = jnp.tile(jnp.arange(GRID_W, dtype=jnp.float32), n_rows)
    return row, col


def axial_rope(x, row, col):
    n_freq = HEAD_DIM // 4
    inv = ROPE_BASE ** (-jnp.arange(n_freq, dtype=jnp.float32) / n_freq)
    ang = jnp.concatenate([row[:, None] * inv, col[:, None] * inv], axis=-1)
    cos = jnp.cos(ang)[None, :, None, :]
    sin = jnp.sin(ang)[None, :, None, :]
    xf = x.astype(jnp.float32)
    x1, x2 = xf[..., 0::2], xf[..., 1::2]
    out = jnp.stack([x1 * cos - x2 * sin, x1 * sin + x2 * cos], axis=-1).reshape(x.shape)
    return out.astype(x.dtype)


def window_attention(q, k, v, kc, vc, sink):
    B, T = q.shape[:2]
    Lc = kc.shape[1]
    G = N_HEADS // N_KV_HEADS
    nb = T // ATTN_BLOCK
    scale = HEAD_DIM ** -0.5
    qb = q.reshape(B, nb, ATTN_BLOCK, N_KV_HEADS, G, HEAD_DIM)

    def band(a):
        ab = a.reshape(B, nb, ATTN_BLOCK, N_KV_HEADS, HEAD_DIM)
        pad = jnp.zeros_like(ab[:, :1])
        ap = jnp.concatenate([pad, ab, pad], axis=1)
        return jnp.concatenate([ap[:, :-2], ap[:, 1:-1], ap[:, 2:]], axis=2)

    kw, vw = band(k), band(v)
    blk_start = jnp.arange(nb)[:, None, None] * ATTN_BLOCK
    qpos = blk_start + jnp.arange(ATTN_BLOCK)[None, :, None]
    kpos = blk_start + jnp.arange(3 * ATTN_BLOCK)[None, None, :] - ATTN_BLOCK
    mask = (jnp.abs(kpos - qpos) <= WINDOW) & (kpos >= 0) & (kpos < T)
    s_loc = jnp.einsum('bnqhgd,bnkhd->bhgnqk', qb, kw).astype(jnp.float32) * scale
    s_loc = jnp.where(mask, s_loc, -jnp.inf)
    s_ctx = jnp.einsum('bnqhgd,bchd->bhgnqc', qb, kc).astype(jnp.float32) * scale
    s_sink = jnp.broadcast_to(sink.astype(jnp.float32).reshape(1, N_KV_HEADS, G, 1, 1, 1), s_ctx.shape[:-1] + (1,))
    p = jax.nn.softmax(jnp.concatenate([s_loc, s_ctx, s_sink], axis=-1), axis=-1).astype(v.dtype)
    nk = 3 * ATTN_BLOCK
    o = (jnp.einsum('bhgnqk,bnkhd->bnqhgd', p[..., :nk], vw)
         + jnp.einsum('bhgnqc,bchd->bnqhgd', p[..., nk:nk + Lc], vc))
    return o.reshape(B, T, ATTN_Q)


def context_attention(qc, kc, vc, sink):
    B, Lc = qc.shape[:2]
    G = N_HEADS // N_KV_HEADS
    qg = qc.reshape(B, Lc, N_KV_HEADS, G, HEAD_DIM)
    s = jnp.einsum('bqhgd,bkhd->bhgqk', qg, kc).astype(jnp.float32) * HEAD_DIM ** -0.5
    s_sink = jnp.broadcast_to(sink.astype(jnp.float32).reshape(1, N_KV_HEADS, G, 1, 1), s.shape[:-1] + (1,))
    p = jax.nn.softmax(jnp.concatenate([s, s_sink], axis=-1), axis=-1)[..., :Lc].astype(vc.dtype)
    o = jnp.einsum('bhgqk,bkhd->bqhgd', p, vc)
    return o.reshape(B, Lc, ATTN_Q)


def fourier_mix(u):
    B, T = u.shape[:2]
    ug = u.reshape(B, T, FOURIER_GROUPS, FOURIER_GROUP_DIM).astype(jnp.float32)
    f = jnp.fft.fft2(ug, axes=(1, 3), norm='ortho').real
    return f.reshape(B, T, FOURIER_W).astype(u.dtype)


def mlstm_inputs(mq, mk, mv, mg, gate_b):
    B, T = mq.shape[:2]
    f32 = jnp.float32
    q = mq.reshape(B, T, MLSTM_HEADS, MLSTM_DK).transpose(0, 2, 1, 3).astype(f32) * MLSTM_DK ** -0.5
    k = mk.reshape(B, T, MLSTM_HEADS, MLSTM_DK).transpose(0, 2, 1, 3).astype(f32)
    v = mv.reshape(B, T, MLSTM_HEADS, MLSTM_DV).transpose(0, 2, 1, 3).astype(f32)
    g = (mg.reshape(B, T, 2, 2, MLSTM_HEADS).astype(f32) + gate_b.astype(f32)).transpose(2, 3, 0, 4, 1)
    ig = g[:, 0]
    lf = jax.nn.log_sigmoid(g[:, 1])
    return q, k, v, ig, lf


def mlstm_zero_state(B):
    return (jnp.zeros((B, MLSTM_HEADS, MLSTM_DK, MLSTM_DV), jnp.float32),
            jnp.zeros((B, MLSTM_HEADS, MLSTM_DK), jnp.float32),
            jnp.zeros((B, MLSTM_HEADS), jnp.float32))


def mlstm_chunk_scan(q, k, v, ig, lf, state):
    B, H, T, _ = q.shape
    L = MLSTM_CHUNK
    nc = T // L
    tril = jnp.tril(jnp.ones((L, L), bool))

    def to_chunks(a):
        return jnp.moveaxis(a.reshape(B, H, nc, L, *a.shape[3:]), 2, 0)

    def step(carry, inp):
        C, n, m = carry
        qc, kc, vc, ic, fc = inp
        b = jnp.cumsum(fc, axis=-1)
        log_d = jnp.where(tril, b[..., :, None] - b[..., None, :] + ic[..., None, :], -jnp.inf)
        inter = b + m[..., None]
        m_t = jnp.maximum(inter, jnp.max(log_d, axis=-1))
        s = jnp.einsum('bhtd,bhsd->bhts', qc, kc) * jnp.exp(log_d - m_t[..., None])
        w_inter = jnp.exp(inter - m_t)
        num = jnp.einsum('bhts,bhsv->bhtv', s, vc) + w_inter[..., None] * jnp.einsum('bhtd,bhdv->bhtv', qc, C)
        den = jnp.sum(s, axis=-1) + w_inter * jnp.einsum('bhtd,bhd->bht', qc, n)
        h = num / jnp.maximum(jnp.abs(den), jnp.exp(-m_t))[..., None]
        b_last = b[..., -1]
        log_w = b_last[..., None] - b + ic
        m_new = jnp.maximum(b_last + m, jnp.max(log_w, axis=-1))
        w = jnp.exp(log_w - m_new[..., None])
        decay = jnp.exp(b_last + m - m_new)
        C_new = decay[..., None, None] * C + jnp.einsum('bhs,bhsd,bhsv->bhdv', w, kc, vc)
        n_new = decay[..., None] * n + jnp.einsum('bhs,bhsd->bhd', w, kc)
        return (C_new, n_new, m_new), h

    final, hs = lax.scan(step, state, (to_chunks(q), to_chunks(k), to_chunks(v), to_chunks(ig), to_chunks(lf)))
    return jnp.moveaxis(hs, 0, 2).reshape(B, H, T, MLSTM_DV), final


def mlstm_output(h_sum, mo, norm_g):
    B, H, T, DV = h_sum.shape
    hh = h_sum.transpose(0, 2, 1, 3)
    hn = hh * lax.rsqrt(jnp.mean(hh * hh, axis=-1, keepdims=True) + NORM_EPS)
    hn = hn.reshape(B, T, ML_V) * norm_g.astype(jnp.float32)
    return (jax.nn.sigmoid(mo.astype(jnp.float32)) * hn).astype(mo.dtype)


def merge_branches(gp, a, f, m, b_gate, w_br_attn, w_br_four, w_br_mlstm, w_out):
    B, T = gp.shape[:2]
    g = jax.nn.sigmoid(gp.reshape(B, T, N_BRANCHES, D_MODEL) + b_gate)
    y = (g[:, :, 0] * (a @ w_br_attn) + g[:, :, 1] * (f @ w_br_four) + g[:, :, 2] * (m @ w_br_mlstm))
    return y @ w_out


def mixer(h, hc, need_ctx, w_in, q_norm_g, k_norm_g, attn_sink, ml_gate_b, ml_norm_g,
          w_br_attn, w_br_four, w_br_mlstm, b_gate, w_out):
    B, T = h.shape[:2]
    Lc = hc.shape[1]
    q, k, v, mq, mk, mv, mo, mg, fu, gp = split_cols(h @ w_in)
    qc, kc, vc, mqc, mkc, mvc, moc, mgc, fuc, gpc = split_cols(hc @ w_in)

    row, col = grid_positions(T)
    q_l = axial_rope(rmsnorm(q.reshape(B, T, N_HEADS, HEAD_DIM), q_norm_g), row, col)
    k_l = axial_rope(rmsnorm(k.reshape(B, T, N_KV_HEADS, HEAD_DIM), k_norm_g), row, col)
    v_l = v.reshape(B, T, N_KV_HEADS, HEAD_DIM)
    k_c = rmsnorm(kc.reshape(B, Lc, N_KV_HEADS, HEAD_DIM), k_norm_g)
    v_c = vc.reshape(B, Lc, N_KV_HEADS, HEAD_DIM)
    a_lat = window_attention(q_l, k_l, v_l, k_c, v_c, attn_sink)

    f_lat = fourier_mix(fu)

    flip = lambda a: jnp.flip(a, axis=2)
    lq, lk, lv, lig, llf = mlstm_inputs(mq, mk, mv, mg, ml_gate_b)
    cq, ck, cv, cig, clf = mlstm_inputs(mqc, mkc, mvc, mgc, ml_gate_b)
    st0 = mlstm_zero_state(B)
    h_cf, st_f = mlstm_chunk_scan(cq, ck, cv, cig[0], clf[0], st0)
    h_cb, st_b = mlstm_chunk_scan(flip(cq), flip(ck), flip(cv), flip(cig[1]), flip(clf[1]), st0)
    h_lf, _ = mlstm_chunk_scan(lq, lk, lv, lig[0], llf[0], st_f)
    h_lb, _ = mlstm_chunk_scan(flip(lq), flip(lk), flip(lv), flip(lig[1]), flip(llf[1]), st_b)
    m_lat = mlstm_output(h_lf + flip(h_lb), mo, ml_norm_g)

    y = merge_branches(gp, a_lat, f_lat, m_lat, b_gate, w_br_attn, w_br_four, w_br_mlstm, w_out)
    if not need_ctx:
        return y, None
    q_c = rmsnorm(qc.reshape(B, Lc, N_HEADS, HEAD_DIM), q_norm_g)
    a_c = context_attention(q_c, k_c, v_c, attn_sink)
    f_c = fourier_mix(fuc)
    m_c = mlstm_output(h_cf + flip(h_cb), moc, ml_norm_g)
    yc = merge_branches(gpc, a_c, f_c, m_c, b_gate, w_br_attn, w_br_four, w_br_mlstm, w_out)
    return y, yc


def grouped_expert_ffn(xt, eid, wts, w1, w3, w2):
    N, D = xt.shape
    M = N * TOP_K
    flat_e = eid.reshape(M).astype(jnp.int32)
    flat_tok = jnp.repeat(jnp.arange(N, dtype=jnp.int32), TOP_K)
    flat_w = wts.reshape(M)
    order = jnp.argsort(flat_e)
    se, stok, sw = flat_e[order], flat_tok[order], flat_w[order]
    counts = jnp.bincount(flat_e, length=N_EXPERTS).astype(jnp.int32)
    padded = (counts + MOE_BLOCK - 1) // MOE_BLOCK * MOE_BLOCK
    start = jnp.cumsum(counts) - counts
    pend = jnp.cumsum(padded)
    pstart = pend - padded
    dest = pstart[se] + jnp.arange(M, dtype=jnp.int32) - start[se]
    n_blocks = -(-M // MOE_BLOCK) + N_EXPERTS
    buf = jnp.zeros((n_blocks * MOE_BLOCK, D), xt.dtype).at[dest].set(xt[stok])
    block_e = jnp.minimum(jnp.searchsorted(pend, jnp.arange(n_blocks, dtype=jnp.int32) * MOE_BLOCK, side='right'),
                          N_EXPERTS - 1)

    def expert_block(args):
        xb, e = args
        return (jax.nn.silu(xb @ w1[e]) * (xb @ w3[e])) @ w2[e]

    yb = lax.map(expert_block, (buf.reshape(n_blocks, MOE_BLOCK, D), block_e))
    y_sorted = yb.reshape(n_blocks * MOE_BLOCK, D)[dest]
    return jnp.zeros((N, D), xt.dtype).at[stok].add(sw[:, None].astype(xt.dtype) * y_sorted)


def hierarchical_moe(h, w_grp, b_grp, w_exp_router, b_exp_router, w1, w3, w2):
    B, T, D = h.shape
    xt = h.reshape(B * T, D)
    glog = (xt @ w_grp).astype(jnp.float32) + b_grp.astype(jnp.float32)
    gprob = jax.nn.softmax(glog, axis=-1)
    gsel = jnp.argmax(glog, axis=-1)
    pg = jnp.take_along_axis(gprob, gsel[:, None], axis=1)
    elog_all = jnp.einsum('nd,dge->nge', xt, w_exp_router).astype(jnp.float32) + b_exp_router.astype(jnp.float32)
    elog = jnp.take_along_axis(elog_all, gsel[:, None, None], axis=1)[:, 0]
    top_v, top_i = lax.top_k(elog, TOP_K)
    wts = pg * jax.nn.softmax(top_v, axis=-1)
    eid = gsel[:, None] * EXPERTS_PER_GROUP + top_i
    return grouped_expert_ffn(xt, eid, wts, w1, w3, w2).reshape(B, T, D)


def setup_inputs(seed: int = 0) -> dict:
    key = jax.random.key(seed)
    ks = jax.random.split(key, 26)
    L, D = DEPTH, D_MODEL

    def nrm(k, shape, s):
        return jax.random.normal(k, shape, jnp.float32) * s

    return {
        'x': nrm(ks[0], (BATCH, SEQ, D), 1.0),
        'c': nrm(ks[1], (BATCH, D), 1.0),
        'ctx': nrm(ks[2], (BATCH, CTX_LEN, D), 1.0),
        'c_ctx': nrm(ks[3], (D,), 1.0),
        'w_mod': nrm(ks[4], (L, D, 6 * D), 0.5 * D ** -0.5),
        'b_mod': nrm(ks[5], (L, 6 * D), 0.02),
        'norm1_g': 1.0 + nrm(ks[6], (L, D), 0.02),
        'norm2_g': 1.0 + nrm(ks[7], (L, D), 0.02),
        'w_in': nrm(ks[8], (L, D, PROJ_DIM), D ** -0.5),
        'q_norm_g': 1.0 + nrm(ks[9], (L, HEAD_DIM), 0.02),
        'k_norm_g': 1.0 + nrm(ks[10], (L, HEAD_DIM), 0.02),
        'attn_sink': nrm(ks[11], (L, N_HEADS), 0.5),
        'ml_gate_b': jnp.array([0.0, 3.0], jnp.float32).reshape(1, 1, 2, 1) + nrm(ks[12], (L, 2, 2, MLSTM_HEADS), 0.1),
        'ml_norm_g': 1.0 + nrm(ks[13], (L, ML_V), 0.02),
        'w_br_attn': nrm(ks[14], (L, ATTN_Q, D), ATTN_Q ** -0.5),
        'w_br_four': nrm(ks[15], (L, FOURIER_W, D), FOURIER_W ** -0.5),
        'w_br_mlstm': nrm(ks[16], (L, ML_V, D), ML_V ** -0.5),
        'b_gate': nrm(ks[17], (L, N_BRANCHES, D), 0.02),
        'w_out': nrm(ks[18], (L, D, D), D ** -0.5),
        'w_grp': nrm(ks[19], (L, D, N_GROUPS), D ** -0.5),
        'b_grp': nrm(ks[20], (L, N_GROUPS), 0.01),
        'w_exp_router': nrm(ks[21], (L, D, N_GROUPS, EXPERTS_PER_GROUP), D ** -0.5),
        'b_exp_router': nrm(ks[22], (L, N_GROUPS, EXPERTS_PER_GROUP), 0.01),
        'w1': nrm(ks[23], (L, N_EXPERTS, D, EXPERT_FF), D ** -0.5),
        'w3': nrm(ks[24], (L, N_EXPERTS, D, EXPERT_FF), D ** -0.5),
        'w2': nrm(ks[25], (L, N_EXPERTS, EXPERT_FF, D), EXPERT_FF ** -0.5),
    }


def reference(x, c, ctx, c_ctx, w_mod, b_mod, norm1_g, norm2_g, w_in, q_norm_g, k_norm_g, attn_sink,
              ml_gate_b, ml_norm_g, w_br_attn, w_br_four, w_br_mlstm, b_gate, w_out, w_grp, b_grp,
              w_exp_router, b_exp_router, w1, w3, w2):
    xc = ctx
    silu_c = jax.nn.silu(c)
    silu_cc = jax.nn.silu(c_ctx)
    for l in range(DEPTH):
        need_ctx = l < DEPTH - 1
        mod = silu_c @ w_mod[l] + b_mod[l]
        mod_c = silu_cc @ w_mod[l] + b_mod[l]
        sh1, sc1, g1, sh2, sc2, g2 = jnp.split(mod[:, None, :], 6, axis=-1)
        csh1, csc1, cg1, csh2, csc2, cg2 = jnp.split(mod_c, 6, axis=-1)
        h = rmsnorm(x, norm1_g[l]) * (1.0 + sc1) + sh1
        hc = rmsnorm(xc, norm1_g[l]) * (1.0 + csc1) + csh1
        y, yc = mixer(h, hc, need_ctx, w_in[l], q_norm_g[l], k_norm_g[l], attn_sink[l], ml_gate_b[l],
                      ml_norm_g[l], w_br_attn[l], w_br_four[l], w_br_mlstm[l], b_gate[l], w_out[l])
        x = x + g1 * y
        h2 = rmsnorm(x, norm2_g[l]) * (1.0 + sc2) + sh2
        x = x + g2 * hierarchical_moe(h2, w_grp[l], b_grp[l], w_exp_router[l], b_exp_router[l], w1[l], w3[l], w2[l])
        if need_ctx:
            xc = xc + cg1 * yc
            hc2 = rmsnorm(xc, norm2_g[l]) * (1.0 + csc2) + csh2
            xc = xc + cg2 * hierarchical_moe(hc2, w_grp[l], b_grp[l], w_exp_router[l], b_exp_router[l],
                                             w1[l], w3[l], w2[l])
    return x
```

```python
import functools

import numpy as np
import jax
import jax.numpy as jnp
from jax import lax
from jax.experimental import pallas as pl
from jax.experimental.pallas import tpu as pltpu

F32 = jnp.float32
BF16 = jnp.bfloat16

D = 2048
N_HEADS, N_KV, HD = 16, 4, 64
GRID_W = 64
ROPE_BASE = 10000.0
EPS = 1e-6
ML_H, ML_DK, ML_DV = 8, 64, 128
ML_L = 128
FG, FGD = 4, 256
N_GRP, EPG, N_EXP, FF = 4, 8, 32, 1024
NEG = -1e30
LANE = 128
ROW_T = 256
MOE_T = 256

C_Q, C_K, C_V = 0, 1024, 1536
C_MQ, C_MK, C_MV, C_MO = 2048, 2560, 3072, 4096
C_F, C_GP, C_TOT = 5120, 6144, 12288


def _cp(n_axes, vmem_mb):
    return pltpu.CompilerParams(dimension_semantics=("arbitrary",) * n_axes,
                                vmem_limit_bytes=int(vmem_mb) * 2 ** 20)


def _pick(n, cands):
    for c in cands:
        if n % c == 0:
            return c
    raise ValueError(f"no tile for {n} in {cands}")


def _dot(a, b):
    return jnp.dot(a, b, preferred_element_type=F32)


def _dot_nt(a, b):
    return lax.dot_general(a, b, (((1,), (1,)), ((), ())), preferred_element_type=F32)


def _split(x):
    hi = x.astype(BF16)
    return hi, (x - hi.astype(F32)).astype(BF16)


def _dot3(a, w):
    ah, al = _split(a)
    wh, wl = _split(w)
    return _dot(ah, wh) + _dot(al, wh) + _dot(ah, wl)


def _mod_kernel(a_ref, w_ref, b_ref, o_ref):
    a = a_ref[...]
    a = a * jax.nn.sigmoid(a)
    o_ref[...] = _dot3(a, w_ref[...]) + b_ref[...]


def modulation(cvec, w_mod, b_mod):
    n = w_mod.shape[1]
    tn = 512
    return pl.pallas_call(
        _mod_kernel,
        out_shape=jax.ShapeDtypeStruct((8, n), F32),
        grid=(n // tn,),
        in_specs=[pl.BlockSpec((8, D), lambda j: (0, 0)),
                  pl.BlockSpec((D, tn), lambda j: (0, j)),
                  pl.BlockSpec((1, tn), lambda j: (0, j))],
        out_specs=pl.BlockSpec((8, tn), lambda j: (0, j)),
        compiler_params=_cp(1, 24), name="modulation",
    )(cvec, w_mod, b_mod.reshape(1, n))


def _mod_spec(k, nct, nb):
    return pl.BlockSpec((None, 1, D), lambda b, j: (jnp.where(j < nct, nb, b), 0, k))


def _norm_kernel(x_ref, g_ref, sc_ref, sh_ref, o_ref):
    x = x_ref[...]
    y = x * lax.rsqrt(jnp.mean(x * x, axis=-1, keepdims=True) + EPS) * g_ref[...]
    o_ref[...] = (y * (1.0 + sc_ref[...]) + sh_ref[...]).astype(o_ref.dtype)


def norm_mod(xa, g, mod3, k_sh, k_sc, nct):
    B, S, _ = xa.shape
    return pl.pallas_call(
        _norm_kernel,
        out_shape=jax.ShapeDtypeStruct((B, S, D), BF16),
        grid=(B, S // ROW_T),
        in_specs=[pl.BlockSpec((None, ROW_T, D), lambda b, j: (b, j, 0)),
                  pl.BlockSpec((1, D), lambda b, j: (0, 0)),
                  _mod_spec(k_sc, nct, B), _mod_spec(k_sh, nct, B)],
        out_specs=pl.BlockSpec((None, ROW_T, D), lambda b, j: (b, j, 0)),
        compiler_params=_cp(2, 24), name="norm_mod",
    )(xa, g.reshape(1, D), mod3, mod3)


def _mm_kernel(a_ref, b_ref, o_ref):
    o_ref[...] = _dot(a_ref[...], b_ref[...]).astype(o_ref.dtype)


def mm(a, b, out_dtype, tn):
    M, K = a.shape
    N = b.shape[1]
    tm = _pick(M, (1024, 512, 256))
    return pl.pallas_call(
        _mm_kernel,
        out_shape=jax.ShapeDtypeStruct((M, N), out_dtype),
        grid=(M // tm, N // tn),
        in_specs=[pl.BlockSpec((tm, K), lambda i, j: (i, 0)),
                  pl.BlockSpec((K, tn), lambda i, j: (0, j))],
        out_specs=pl.BlockSpec((tm, tn), lambda i, j: (i, j)),
        compiler_params=_cp(2, 40), name="mm",
    )(a, b)


def _prep_kernel(x_ref, g_ref, cos_ref, sin_ref, bd_ref, o_ref):
    cosv = cos_ref[...]
    sinv = sin_ref[...]
    bd = bd_ref[...]
    lane = lax.broadcasted_iota(jnp.int32, cosv.shape, 1)
    first = (lane & (HD - 1)) < (HD // 2)
    for s in range((C_V - C_Q) // LANE):
        sl = slice(s * LANE, (s + 1) * LANE)
        x = x_ref[:, sl].astype(F32)
        hi, lo = _split(x * x)
        ss = _dot(hi, bd) + _dot(lo, bd)
        y = x * lax.rsqrt(ss * (1.0 / HD) + EPS) * g_ref[:, sl]
        sw = jnp.where(first, pltpu.roll(y, LANE - HD // 2, 1), pltpu.roll(y, HD // 2, 1))
        o_ref[:, sl] = (y * cosv + sw * sinv).astype(o_ref.dtype)


def qk_prep(proj, gain, cos_t, sin_t, bd):
    B, S, _ = proj.shape
    w = C_V - C_Q
    return pl.pallas_call(
        _prep_kernel,
        out_shape=jax.ShapeDtypeStruct((B, S, w), BF16),
        grid=(B, S // ROW_T),
        in_specs=[pl.BlockSpec((None, ROW_T, w), lambda b, j: (b, j, 0)),
                  pl.BlockSpec((1, w), lambda b, j: (0, 0)),
                  pl.BlockSpec((ROW_T, LANE), lambda b, j: (j, 0)),
                  pl.BlockSpec((ROW_T, LANE), lambda b, j: (j, 0)),
                  pl.BlockSpec((LANE, LANE), lambda b, j: (0, 0))],
        out_specs=pl.BlockSpec((None, ROW_T, w), lambda b, j: (b, j, 0)),
        compiler_params=_cp(2, 24), name="qk_prep",
    )(proj, gain, cos_t, sin_t, bd)


AB = 128


def _attn_kernel(sink_ref, q_ref, kp_ref, kc_ref, kn_ref, kx_ref, vp_ref, vc_ref, vn_ref, vx_ref, o_ref,
                 *, nctb, nblk):
    n = pl.program_id(1)
    is_lat = n >= nctb
    iq = lax.broadcasted_iota(jnp.int32, (AB, AB), 0)
    ik = lax.broadcasted_iota(jnp.int32, (AB, AB), 1)
    ok_p = jnp.logical_and(ik >= iq, jnp.logical_and(is_lat, n - 1 >= nctb))
    ok_c = jnp.logical_and(ik >= 0, is_lat)
    ok_n = jnp.logical_and(ik <= iq, jnp.logical_and(is_lat, n + 1 <= nblk - 1))
    nx = kx_ref.shape[0]
    bias = jnp.concatenate([jnp.where(ok_p, 0.0, NEG), jnp.where(ok_c, 0.0, NEG), jnp.where(ok_n, 0.0, NEG),
                            jnp.zeros((AB, nx), F32)], axis=1)
    kall = jnp.concatenate([kp_ref[...], kc_ref[...], kn_ref[...], kx_ref[...]], axis=0)
    vall = jnp.concatenate([vp_ref[...], vc_ref[...], vn_ref[...], vx_ref[...]], axis=0)
    lane = lax.broadcasted_iota(jnp.int32, (AB, LANE), 1)
    lo_half = lane < HD
    for slab in range(N_HEADS // 2):
        g = slab // 2
        kg = kall[:, g * LANE:(g + 1) * LANE]
        vg = vall[:, g * LANE:(g + 1) * LANE]
        qs = q_ref[:, slab * LANE:(slab + 1) * LANE]
        outs = []
        for half in range(2):
            qm = jnp.where(lo_half if half == 0 else jnp.logical_not(lo_half), qs, jnp.zeros_like(qs))
            s = _dot_nt(qm, kg) + bias
            sk = sink_ref[slab * 2 + half]
            m = jnp.maximum(jnp.max(s, axis=-1, keepdims=True), sk)
            p = jnp.exp(s - m)
            l = jnp.sum(p, axis=-1, keepdims=True) + jnp.exp(sk - m)
            outs.append(_dot(p.astype(BF16), vg) / l)
        o_ref[:, slab * LANE:(slab + 1) * LANE] = jnp.where(lo_half, outs[0], outs[1]).astype(o_ref.dtype)


def attention(qkn, proj, sink, lc):
    B, S, _ = qkn.shape
    nblk = S // AB
    nctb = lc // AB
    kw = C_V - C_K
    kcol, vcol = C_K // kw, C_V // kw

    def rows(fn, col):
        return pl.BlockSpec((None, AB, kw), lambda b, n: (b, fn(n), col))

    prev = lambda n: jnp.maximum(n - 1, 0)
    cur = lambda n: n
    nxt = lambda n: jnp.minimum(n + 1, nblk - 1)
    ctx = lambda col: pl.BlockSpec((None, lc, kw), lambda b, n: (b, 0, col))
    return pl.pallas_call(
        functools.partial(_attn_kernel, nctb=nctb, nblk=nblk),
        out_shape=jax.ShapeDtypeStruct((B, S, C_K), BF16),
        grid=(B, nblk),
        in_specs=[pl.BlockSpec(memory_space=pltpu.SMEM),
                  pl.BlockSpec((None, AB, C_K), lambda b, n: (b, n, 0)),
                  rows(prev, kcol), rows(cur, kcol), rows(nxt, kcol), ctx(kcol),
                  rows(prev, vcol), rows(cur, vcol), rows(nxt, vcol), ctx(vcol)],
        out_specs=pl.BlockSpec((None, AB, C_K), lambda b, n: (b, n, 0)),
        compiler_params=_cp(2, 32), name="attention",
    )(sink, qkn, qkn, qkn, qkn, qkn, proj, proj, proj, proj)


def _dftc_kernel(u_ref, w_ref, o_ref):
    o_ref[...] = _dot(u_ref[...], w_ref[...]).astype(o_ref.dtype)


def dft_channels(proj, wc, row0, nrows):
    B = proj.shape[0]
    off = row0 // ROW_T
    return pl.pallas_call(
        _dftc_kernel,
        out_shape=jax.ShapeDtypeStruct((B, nrows, 2 * FG * FGD), BF16),
        grid=(B, nrows // ROW_T),
        in_specs=[pl.BlockSpec((None, ROW_T, FG * FGD), lambda b, j: (b, j + off, C_F // (FG * FGD))),
                  pl.BlockSpec((FG * FGD, 2 * FG * FGD), lambda b, j: (0, 0))],
        out_specs=pl.BlockSpec((None, ROW_T, 2 * FG * FGD), lambda b, j: (b, j, 0)),
        compiler_params=_cp(2, 32), name="dft_channels",
    )(proj, wc)


def _dftp_kernel(c_ref, s_ref, y1_ref, y2_ref, o_ref, acc_ref):
    k = pl.program_id(2)

    @pl.when(k == 0)
    def _():
        acc_ref[...] = jnp.zeros_like(acc_ref)

    acc_ref[...] += _dot(c_ref[...], y1_ref[...]) + _dot(s_ref[...], y2_ref[...])

    @pl.when(k == pl.num_programs(2) - 1)
    def _():
        o_ref[...] = acc_ref[...].astype(o_ref.dtype)


def dft_positions(ct, st, y):
    B, n, w2 = y.shape
    w = w2 // 2
    t = _pick(n, (1024, 512, 256))
    return pl.pallas_call(
        _dftp_kernel,
        out_shape=jax.ShapeDtypeStruct((B, n, w), BF16),
        grid=(B, n // t, n // t),
        in_specs=[pl.BlockSpec((t, t), lambda b, i, k: (i, k)),
                  pl.BlockSpec((t, t), lambda b, i, k: (i, k)),
                  pl.BlockSpec((None, t, w), lambda b, i, k: (b, k, 0)),
                  pl.BlockSpec((None, t, w), lambda b, i, k: (b, k, 1))],
        out_specs=pl.BlockSpec((None, t, w), lambda b, i, k: (b, i, 0)),
        scratch_shapes=[pltpu.VMEM((t, w), F32)],
        compiler_params=_cp(3, 40), name="dft_positions",
    )(ct, st, y, y)


def _log_sigmoid(x):
    return jnp.minimum(x, 0.0) - jnp.log(1.0 + jnp.exp(-jnp.abs(x)))


def _mlstm_kernel(q_ref, k_ref, v_ref, mo_ref, g_ref, gt_ref, gb_ref, gbt_ref, ng_ref, o_ref,
                  hacc_ref, c_ref, m_ref, *, nctc, nchunk):
    L = ML_L
    hacc_ref[...] = jnp.zeros_like(hacc_ref)
    c_ref[...] = jnp.zeros_like(c_ref)
    m_ref[...] = jnp.zeros_like(m_ref)
    it_r = lax.broadcasted_iota(jnp.int32, (L, L), 0)
    it_c = lax.broadcasted_iota(jnp.int32, (L, L), 1)
    tri = (it_c <= it_r, it_c >= it_r)
    trif = tuple(jnp.where(t, 1.0, 0.0).astype(F32) for t in tri)
    lane = lax.broadcasted_iota(jnp.int32, (L, LANE), 1)
    head_lanes = (lane < ML_DK, lane >= ML_DK)
    is_f_col = (lax.broadcasted_iota(jnp.int32, (L, LANE), 1) & 2) == 2
    is_f_row = (lax.broadcasted_iota(jnp.int32, (8, L), 0) & 2) == 2
    ones = jnp.ones((L, LANE), BF16)
    hp = lax.Precision.HIGHEST

    def step(it, carry):
        for d in range(2):
            if d == 0:
                c = it
            else:
                c = jnp.where(it < nctc, nctc - 1 - it, nchunk - 1 - (it - nctc))
            r0 = pl.multiple_of(c * L, L)
            graw = g_ref[pl.ds(r0, L), :] + gb_ref[...]
            grawt = gt_ref[c] + gbt_ref[...]
            lf = jnp.where(is_f_col, _log_sigmoid(graw), 0.0)
            lft = jnp.where(is_f_row, _log_sigmoid(grawt), 0.0)
            bcum = jnp.dot(trif[d], lf, precision=hp, preferred_element_type=F32)
            bcumt = lax.dot_general(lft, trif[d], (((1,), (1,)), ((), ())), precision=hp,
                                    preferred_element_type=F32)
            qp = q_ref[pl.ds(r0, L), :]
            kp = k_ref[pl.ds(r0, L), :]
            last = L - 1 if d == 0 else 0
            for hh in range(2):
                icol = d * 4 + hh
                fcol = d * 4 + 2 + hh
                sidx = d * 2 + hh
                i_col = graw[:, icol:icol + 1]
                i_row = grawt[icol:icol + 1, :]
                b_col = bcum[:, fcol:fcol + 1]
                b_row = bcumt[fcol:fcol + 1, :]
                m_prev = m_ref[sidx:sidx + 1, 0:1]
                logd = jnp.where(tri[d], b_col - b_row + i_row, NEG)
                inter = b_col + m_prev
                m_t = jnp.maximum(inter, jnp.max(logd, axis=-1, keepdims=True))
                dm = jnp.exp(logd - m_t)
                qm = jnp.where(head_lanes[hh], qp, jnp.zeros_like(qp))
                km = jnp.where(head_lanes[hh], kp, jnp.zeros_like(kp))
                s = (_dot_nt(qm, km) * (ML_DK ** -0.5) * dm).astype(BF16)
                vext = jnp.concatenate([v_ref[pl.ds(r0, L), hh * ML_DV:(hh + 1) * ML_DV], ones], axis=1)
                cst = c_ref[sidx]
                tot = _dot(s, vext) + jnp.exp(inter - m_t) * (_dot(qm, cst.astype(BF16)) * (ML_DK ** -0.5))
                num = tot[:, :ML_DV]
                den = tot[:, ML_DV:]
                h = num / jnp.maximum(jnp.abs(den), jnp.exp(-m_t))
                hs = slice(hh * ML_DV, (hh + 1) * ML_DV)
                hacc_ref[pl.ds(r0, L), hs] = hacc_ref[pl.ds(r0, L), hs] + h
                b_last = b_col[last:last + 1, :]
                log_w = b_last - b_col + i_col
                m_new = jnp.maximum(b_last + m_prev, jnp.max(log_w, axis=0, keepdims=True))
                wgt = jnp.exp(log_w - m_new)
                decay = jnp.exp(b_last + m_prev - m_new)
                kw = (km.astype(F32) * wgt).astype(BF16)
                dc = lax.dot_general(kw, vext, (((0,), (0,)), ((), ())), preferred_element_type=F32)
                c_ref[sidx] = decay * cst + dc
                m_ref[sidx:sidx + 1, :] = jnp.broadcast_to(m_new, (1, LANE))
        return carry

    lax.fori_loop(0, nchunk, step, 0)

    def out_step(i, carry):
        r0 = pl.multiple_of(i * L, L)
        for hh in range(2):
            hs = slice(hh * ML_DV, (hh + 1) * ML_DV)
            h = hacc_ref[pl.ds(r0, L), hs]
            hn = h * lax.rsqrt(jnp.mean(h * h, axis=-1, keepdims=True) + EPS) * ng_ref[:, hs]
            og = jax.nn.sigmoid(mo_ref[pl.ds(r0, L), hs].astype(F32))
            o_ref[pl.ds(r0, L), hs] = (og * hn).astype(o_ref.dtype)
        return carry

    lax.fori_loop(0, nchunk, out_step, 0)


def mlstm(proj, gates, gates_t, gb, gbt, norm_g, lc):
    B, S, _ = proj.shape
    nchunk = S // ML_L
    pw = 2 * ML_DK
    vw = 2 * ML_DV
    return pl.pallas_call(
        functools.partial(_mlstm_kernel, nctc=lc // ML_L, nchunk=nchunk),
        out_shape=jax.ShapeDtypeStruct((B, S, ML_H * ML_DV), BF16),
        grid=(B, ML_H // 2),
        in_specs=[pl.BlockSpec((None, S, pw), lambda b, p: (b, 0, C_MQ // pw + p)),
                  pl.BlockSpec((None, S, pw), lambda b, p: (b, 0, C_MK // pw + p)),
                  pl.BlockSpec((None, S, vw), lambda b, p: (b, 0, C_MV // vw + p)),
                  pl.BlockSpec((None, S, vw), lambda b, p: (b, 0, C_MO // vw + p)),
                  pl.BlockSpec((None, None, S, LANE), lambda b, p: (b, p, 0, 0)),
                  pl.BlockSpec((None, None, nchunk, 8, ML_L), lambda b, p: (b, p, 0, 0, 0)),
                  pl.BlockSpec((None, 1, LANE), lambda b, p: (p, 0, 0)),
                  pl.BlockSpec((None, 8, ML_L), lambda b, p: (p, 0, 0)),
                  pl.BlockSpec((1, vw), lambda b, p: (0, p))],
        out_specs=pl.BlockSpec((None, S, vw), lambda b, p: (b, 0, p)),
        scratch_shapes=[pltpu.VMEM((S, vw), F32), pltpu.VMEM((4, 2 * ML_DK, 2 * ML_DV), F32),
                        pltpu.VMEM((8, LANE), F32)],
        compiler_params=_cp(2, 48), name="mlstm",
    )(proj, proj, proj, proj, gates, gates_t, gb, gbt, norm_g)


def _merge_kernel(a_ref, f_ref, m_ref, wa_ref, wf_ref, wm_ref, ga_ref, gf_ref, gm_ref,
                  ba_ref, bf_ref, bm_ref, o_ref):
    def br(x_ref, w_ref, g_ref, b_ref):
        return jax.nn.sigmoid(g_ref[...].astype(F32) + b_ref[...]) * _dot(x_ref[...], w_ref[...])

    y = br(a_ref, wa_ref, ga_ref, ba_ref) + br(f_ref, wf_ref, gf_ref, bf_ref) + br(m_ref, wm_ref, gm_ref, bm_ref)
    o_ref[...] = y.astype(o_ref.dtype)


def merge(a, f, m, proj, wa, wf, wm, b_gate):
    R, kin = a.shape
    tm = _pick(R, (512, 256))
    tn = 1024
    x_spec = pl.BlockSpec((tm, kin), lambda j, i: (i, 0))
    w_spec = pl.BlockSpec((kin, tn), lambda j, i: (0, j))
    gp = lambda br: pl.BlockSpec((tm, tn), lambda j, i: (i, (C_GP + br * D) // tn + j))
    bg = lambda br: pl.BlockSpec((None, 1, tn), lambda j, i: (br, 0, j))
    return pl.pallas_call(
        _merge_kernel,
        out_shape=jax.ShapeDtypeStruct((R, D), BF16),
        grid=(D // tn, R // tm),
        in_specs=[x_spec, x_spec, x_spec, w_spec, w_spec, w_spec, gp(0), gp(1), gp(2), bg(0), bg(1), bg(2)],
        out_specs=pl.BlockSpec((tm, tn), lambda j, i: (i, j)),
        compiler_params=_cp(2, 40), name="merge",
    )(a, f, m, wa, wf, wm, proj, proj, proj, b_gate, b_gate, b_gate)


def _outproj_kernel(y_ref, w_ref, x_ref, g_ref, o_ref):
    o_ref[...] = x_ref[...] + g_ref[...] * _dot(y_ref[...], w_ref[...])


def out_proj_residual(y, w_out, xa, mod3, k_gate, nct):
    B, S, _ = xa.shape
    return pl.pallas_call(
        _outproj_kernel,
        out_shape=jax.ShapeDtypeStruct((B, S, D), F32),
        grid=(B, S // ROW_T),
        in_specs=[pl.BlockSpec((None, ROW_T, D), lambda b, j: (b, j, 0)),
                  pl.BlockSpec((D, D), lambda b, j: (0, 0)),
                  pl.BlockSpec((None, ROW_T, D), lambda b, j: (b, j, 0)),
                  _mod_spec(k_gate, nct, B)],
        out_specs=pl.BlockSpec((None, ROW_T, D), lambda b, j: (b, j, 0)),
        compiler_params=_cp(2, 40), name="out_proj",
    )(y, w_out, xa, mod3)


def _pack2(lo, hi):
    lo_b = lax.bitcast_convert_type(lo.astype(BF16).astype(F32), jnp.uint32)
    hi_b = lax.bitcast_convert_type(hi.astype(BF16).astype(F32), jnp.uint32)
    return (lo_b >> 16) | (hi_b & jnp.uint32(0xFFFF0000))


def _unpack2(w):
    lo = lax.bitcast_convert_type(w << 16, F32)
    hi = lax.bitcast_convert_type(w & jnp.uint32(0xFFFF0000), F32)
    return lo, hi


def _router_kernel(x_ref, g_ref, sc_ref, sh_ref, wh_ref, wl_ref, b_ref, hp_ref, eid_ref, wt_ref):
    x = x_ref[...]
    y = x * lax.rsqrt(jnp.mean(x * x, axis=-1, keepdims=True) + EPS) * g_ref[...]
    y = y * (1.0 + sc_ref[...]) + sh_ref[...]
    hp_ref[...] = _pack2(y[:, :D // 2], y[:, D // 2:])
    yh, yl = _split(y)
    logits = _dot(yh, wh_ref[...]) + _dot(yl, wh_ref[...]) + _dot(yh, wl_ref[...]) + b_ref[...]
    lane = lax.broadcasted_iota(jnp.int32, logits.shape, 1)
    lanef = lane.astype(F32)
    big = float(LANE)
    glog = jnp.where(lane < N_GRP, logits, NEG)
    gmax = jnp.max(glog, axis=-1, keepdims=True)
    gsel = jnp.min(jnp.where(glog == gmax, lanef, big), axis=-1, keepdims=True)
    pg = 1.0 / jnp.sum(jnp.exp(glog - gmax), axis=-1, keepdims=True)
    lo = N_GRP + EPG * gsel
    el = jnp.where(jnp.logical_and(lanef >= lo, lanef < lo + EPG), logits, NEG)
    v1 = jnp.max(el, axis=-1, keepdims=True)
    i1 = jnp.min(jnp.where(el == v1, lanef, big), axis=-1, keepdims=True)
    el2 = jnp.where(lanef == i1, NEG, el)
    v2 = jnp.max(el2, axis=-1, keepdims=True)
    i2 = jnp.min(jnp.where(el2 == v2, lanef, big), axis=-1, keepdims=True)
    e = jnp.exp(v2 - v1)
    w1 = pg / (1.0 + e)
    w2 = pg * e / (1.0 + e)
    eid_ref[...] = jnp.where(lane == 0, i1 - N_GRP, jnp.where(lane == 1, i2 - N_GRP, 0.0)).astype(jnp.int32)
    wt_ref[...] = jnp.where(lane == 0, w1, jnp.where(lane == 1, w2, 0.0))


def norm_router(xa, g, mod3, k_sh, k_sc, nct, wr_hi, wr_lo, br):
    B, S, _ = xa.shape
    tile = lambda w: pl.BlockSpec((None, ROW_T, w), lambda b, j: (b, j, 0))
    const = lambda shape: pl.BlockSpec(shape, lambda b, j: (0, 0))
    return pl.pallas_call(
        _router_kernel,
        out_shape=(jax.ShapeDtypeStruct((B, S, D // 2), jnp.uint32),
                   jax.ShapeDtypeStruct((B, S, LANE), jnp.int32),
                   jax.ShapeDtypeStruct((B, S, LANE), F32)),
        grid=(B, S // ROW_T),
        in_specs=[tile(D), const((1, D)), _mod_spec(k_sc, nct, B), _mod_spec(k_sh, nct, B),
                  const((D, LANE)), const((D, LANE)), const((1, LANE))],
        out_specs=(tile(D // 2), tile(LANE), tile(LANE)),
        compiler_params=_cp(2, 32), name="norm_router",
    )(xa, g.reshape(1, D), mod3, mod3, wr_hi, wr_lo, br)


def _row_copy(src_hbm, row, dst_ref, r, sem):
    return pltpu.make_async_copy(src_hbm.at[pl.ds(row, 1), :], dst_ref.at[pl.ds(r, 1), :], sem)


def _dispatch_kernel(src_ref, nused_ref, x_hbm, o_ref, sem):
    i = pl.program_id(0)
    used = i < nused_ref[0]

    @pl.when(used)
    def _():
        def start(r, carry):
            _row_copy(x_hbm, src_ref[i * MOE_T + r], o_ref, r, sem).start()
            return carry

        lax.fori_loop(0, MOE_T, start, 0)

        def wait(r, carry):
            _row_copy(x_hbm, 0, o_ref, r, sem).wait()
            return carry

        lax.fori_loop(0, MOE_T, wait, 0)

    @pl.when(jnp.logical_not(used))
    def _():
        o_ref[...] = jnp.zeros_like(o_ref)


def dispatch(hp, src, nused, n_blocks):
    w = hp.shape[1]
    grid_spec = pltpu.PrefetchScalarGridSpec(
        num_scalar_prefetch=2, grid=(n_blocks,),
        in_specs=[pl.BlockSpec(memory_space=pl.ANY)],
        out_specs=pl.BlockSpec((MOE_T, w), lambda i, s, n: (i, 0)),
        scratch_shapes=[pltpu.SemaphoreType.DMA(())])
    return pl.pallas_call(
        _dispatch_kernel,
        out_shape=jax.ShapeDtypeStruct((n_blocks * MOE_T, w), jnp.uint32),
        grid_spec=grid_spec, compiler_params=_cp(1, 16), name="moe_dispatch",
    )(src, nused, hp)


def _expert_kernel(be_ref, nused_ref, x_ref, w1_ref, w3_ref, w2_ref, o_ref):
    i = pl.program_id(0)

    @pl.when(i < nused_ref[0])
    def _():
        xl, xh = _unpack2(x_ref[...])
        xl = xl.astype(BF16)
        xh = xh.astype(BF16)
        half = D // 2
        a = _dot(xl, w1_ref[:half, :]) + _dot(xh, w1_ref[half:, :])
        b = _dot(xl, w3_ref[:half, :]) + _dot(xh, w3_ref[half:, :])
        hmid = (a * jax.nn.sigmoid(a) * b).astype(BF16)
        y = _dot(hmid, w2_ref[...])
        o_ref[...] = _pack2(y[:, :half], y[:, half:])

    @pl.when(i >= nused_ref[0])
    def _():
        o_ref[...] = jnp.zeros_like(o_ref)


def expert_ffn(xs, block_e, nused, w1, w3, w2, n_blocks):
    w = xs.shape[1]
    wspec = lambda shape: pl.BlockSpec((None,) + shape, lambda i, be, n: (be[i], 0, 0))
    grid_spec = pltpu.PrefetchScalarGridSpec(
        num_scalar_prefetch=2, grid=(n_blocks,),
        in_specs=[pl.BlockSpec((MOE_T, w), lambda i, be, n: (jnp.minimum(i, n[0] - 1), 0)),
                  wspec((D, FF)), wspec((D, FF)), wspec((FF, D))],
        out_specs=pl.BlockSpec((MOE_T, w), lambda i, be, n: (i, 0)))
    return pl.pallas_call(
        _expert_kernel,
        out_shape=jax.ShapeDtypeStruct((n_blocks * MOE_T, w), jnp.uint32),
        grid_spec=grid_spec, compiler_params=_cp(1, 48), name="moe_experts",
    )(block_e, nused, xs, w1, w3, w2)


def _combine_kernel(dest_ref, y_hbm, x_ref, wt_ref, g_ref, o_ref, ya_ref, yb_ref, sem):
    i = pl.program_id(0) * pl.num_programs(1) + pl.program_id(1)
    base = i * (2 * ROW_T)

    def start(r, carry):
        _row_copy(y_hbm, dest_ref[base + 2 * r], ya_ref, r, sem).start()
        _row_copy(y_hbm, dest_ref[base + 2 * r + 1], yb_ref, r, sem).start()
        return carry

    lax.fori_loop(0, ROW_T, start, 0)

    def wait(r, carry):
        _row_copy(y_hbm, 0, ya_ref, r, sem).wait()
        _row_copy(y_hbm, 0, yb_ref, r, sem).wait()
        return carry

    lax.fori_loop(0, ROW_T, wait, 0)
    w0 = wt_ref[:, 0:1]
    w1 = wt_ref[:, 1:2]
    al, ah = _unpack2(ya_ref[...])
    bl, bh = _unpack2(yb_ref[...])
    half = D // 2
    g = g_ref[...]
    o_ref[:, :half] = x_ref[:, :half] + g[:, :half] * (w0 * al + w1 * bl)
    o_ref[:, half:] = x_ref[:, half:] + g[:, half:] * (w0 * ah + w1 * bh)


def combine(ys, dest, xa, wts, mod3, k_gate, nct):
    B, S, _ = xa.shape
    w = ys.shape[1]
    tile = lambda wd: pl.BlockSpec((None, ROW_T, wd), lambda b, j, d: (b, j, 0))
    grid_spec = pltpu.PrefetchScalarGridSpec(
        num_scalar_prefetch=1, grid=(B, S // ROW_T),
        in_specs=[pl.BlockSpec(memory_space=pl.ANY), tile(D), tile(LANE),
                  pl.BlockSpec((None, 1, D), lambda b, j, d: (jnp.where(j < nct, B, b), 0, k_gate))],
        out_specs=tile(D),
        scratch_shapes=[pltpu.VMEM((ROW_T, w), jnp.uint32), pltpu.VMEM((ROW_T, w), jnp.uint32),
                        pltpu.SemaphoreType.DMA(())])
    return pl.pallas_call(
        _combine_kernel,
        out_shape=jax.ShapeDtypeStruct((B, S, D), F32),
        grid_spec=grid_spec, compiler_params=_cp(2, 32), name="moe_combine",
    )(dest, ys, xa, wts, mod3)


def moe_plan(eid, n_blocks):
    m = eid.shape[0]
    order = jnp.argsort(eid, stable=True).astype(jnp.int32)
    onehot = (eid[:, None] == jnp.arange(N_EXP, dtype=jnp.int32)[None, :]).astype(jnp.int32)
    csum = jnp.cumsum(onehot, axis=0)
    counts = csum[-1]
    padded = (counts + MOE_T - 1) // MOE_T * MOE_T
    start = jnp.cumsum(counts) - counts
    pend = jnp.cumsum(padded)
    pstart = pend - padded
    rank_in_e = jnp.sum((csum - onehot) * onehot, axis=1)
    dest = (jnp.sum(pstart[None, :] * onehot, axis=1) + rank_in_e).astype(jnp.int32)
    slot = jnp.arange(n_blocks * MOE_T, dtype=jnp.int32)
    block_e = jnp.minimum(jnp.searchsorted(pend, jnp.arange(n_blocks, dtype=jnp.int32) * MOE_T, side="right"),
                          N_EXP - 1).astype(jnp.int32)
    se_slot = block_e[slot // MOE_T]
    rank = slot - pstart[se_slot]
    valid = rank < counts[se_slot]
    src = jnp.where(valid, order[jnp.clip(start[se_slot] + rank, 0, m - 1)] // 2, 0).astype(jnp.int32)
    nused = (pend[-1] // MOE_T).astype(jnp.int32).reshape(1)
    return dest, src, block_e, nused


def _rope_tables(s, lc):
    t = np.arange(s - lc)
    row = (t // GRID_W).astype(np.float64)
    col = (t % GRID_W).astype(np.float64)
    nf = HD // 4
    inv = np.float32(ROPE_BASE) ** (-np.arange(nf, dtype=np.float32) / np.float32(nf))
    ang = np.concatenate([row[:, None] * inv, col[:, None] * inv], axis=-1)
    ang = np.concatenate([np.zeros((lc, HD // 2)), ang], axis=0)
    cos = np.cos(ang)
    sin = np.sin(ang)
    cos_t = np.tile(np.concatenate([cos, cos], axis=-1), (1, LANE // HD))
    sin_t = np.tile(np.concatenate([-sin, sin], axis=-1), (1, LANE // HD))
    return jnp.asarray(cos_t, F32), jnp.asarray(sin_t, F32)


def _chan_dft():
    j = np.arange(FGD)
    ang = 2.0 * np.pi * ((j[:, None] * j[None, :]) % FGD) / FGD
    c = np.kron(np.eye(FG), np.cos(ang)) / np.sqrt(FGD)
    s = np.kron(np.eye(FG), np.sin(ang)) / np.sqrt(FGD)
    return jnp.asarray(np.concatenate([c, -s], axis=1), BF16)


def _pos_dft(n):
    if n <= 512:
        k = np.arange(n)
        ang = 2.0 * np.pi * ((k[:, None] * k[None, :]) % n) / n
        return (jnp.asarray(np.cos(ang) / np.sqrt(n), BF16), jnp.asarray(np.sin(ang) / np.sqrt(n), BF16))
    nr = n // LANE
    k = np.arange(n)
    a = 2.0 * np.pi * ((k[:, None] * np.arange(nr)[None, :]) % nr) / nr
    b = 2.0 * np.pi * (k[:, None] * np.arange(LANE)[None, :]) / n
    sc = 1.0 / np.sqrt(n)
    ca, sa = jnp.asarray(np.cos(a) * sc, F32)[:, :, None], jnp.asarray(np.sin(a) * sc, F32)[:, :, None]
    cb, sb = jnp.asarray(np.cos(b), F32)[:, None, :], jnp.asarray(np.sin(b), F32)[:, None, :]
    ct = (ca * cb - sa * sb).reshape(n, n).astype(BF16)
    st = (sa * cb + ca * sb).reshape(n, n).astype(BF16)
    return ct, st


def _head_perm(w, nh):
    return w.reshape(D, nh, HD // 2, 2).transpose(0, 1, 3, 2).reshape(D, nh, HD)


def _proj_weights(w_in):
    q = _head_perm(w_in[:, 0:1024], N_HEADS).reshape(D, 1024)
    k = _head_perm(w_in[:, 1024:1280], N_KV)
    k = jnp.concatenate([k, k], axis=-1).reshape(D, 2 * N_KV * HD)
    v = w_in[:, 1280:1536].reshape(D, N_KV, HD)
    v = jnp.concatenate([v, v], axis=-1).reshape(D, 2 * N_KV * HD)
    w_big = jnp.concatenate([q, k, v, w_in[:, 1536:4608], w_in[:, 4640:]], axis=1).astype(BF16)
    w_gate = jnp.pad(w_in[:, 4608:4640], ((0, 0), (0, LANE - 32))).astype(BF16)
    return w_big, w_gate


def _qk_gain(qg, kg):
    perm = np.concatenate([np.arange(0, HD, 2), np.arange(1, HD, 2)])
    qp = jnp.tile(qg[perm] * (HD ** -0.5), N_HEADS)
    kp = jnp.tile(kg[perm], 2 * N_KV)
    return jnp.concatenate([qp, kp]).reshape(1, -1).astype(F32)


def kernel(x, c, ctx, c_ctx, w_mod, b_mod, norm1_g, norm2_g, w_in, q_norm_g, k_norm_g, attn_sink, ml_gate_b,
           ml_norm_g, w_br_attn, w_br_four, w_br_mlstm, b_gate, w_out, w_grp, b_grp, w_exp_router,
           b_exp_router, w1, w3, w2):
    B, T, _ = x.shape
    lc = ctx.shape[1]
    S = lc + T
    R = B * S
    depth = w_mod.shape[0]
    nct = lc // ROW_T
    assert lc % ROW_T == 0 and T % ROW_T == 0 and B < 8

    xa = jnp.concatenate([ctx, x], axis=1)
    cvec = jnp.concatenate([c, c_ctx[None], jnp.zeros((7 - B, D), F32)], axis=0)
    cos_t, sin_t = _rope_tables(S, lc)
    bd = jnp.asarray(np.kron(np.eye(LANE // HD), np.ones((HD, HD))), BF16)
    wc = _chan_dft()
    ct_lat, st_lat = _pos_dft(T)
    ct_ctx, st_ctx = _pos_dft(lc)
    nchunk = S // ML_L
    n_pairs = R * 2
    n_blocks = -(-n_pairs // MOE_T) + N_EXP

    for l in range(depth):
        mod = modulation(cvec, w_mod[l], b_mod[l])
        mod3 = mod.reshape(8, 1, 6 * D)
        h = norm_mod(xa, norm1_g[l], mod3, 0, 1, nct)
        w_big, w_gate = _proj_weights(w_in[l])
        h2d = h.reshape(R, D)
        proj = mm(h2d, w_big, BF16, 1024).reshape(B, S, C_TOT)
        graw = mm(h2d, w_gate, F32, LANE).reshape(B, S, LANE)[..., :32]

        qkn = qk_prep(proj, _qk_gain(q_norm_g[l], k_norm_g[l]), cos_t, sin_t, bd)
        a = attention(qkn, proj, attn_sink[l].astype(F32), lc)

        f_ctx = dft_positions(ct_ctx, st_ctx, dft_channels(proj, wc, 0, lc))
        f_lat = dft_positions(ct_lat, st_lat, dft_channels(proj, wc, lc, T))
        f = jnp.concatenate([f_ctx, f_lat], axis=1)

        g5 = graw.reshape(B, S, 2, 2, ML_H // 2, 2).transpose(0, 4, 1, 2, 3, 5).reshape(B, ML_H // 2, S, 8)
        gates = jnp.pad(g5, ((0, 0), (0, 0), (0, 0), (0, LANE - 8)))
        gates_t = g5.reshape(B, ML_H // 2, nchunk, ML_L, 8).transpose(0, 1, 2, 4, 3)
        gb5 = ml_gate_b[l].reshape(2, 2, ML_H // 2, 2).transpose(2, 0, 1, 3).reshape(ML_H // 2, 8).astype(F32)
        gb = jnp.pad(gb5, ((0, 0), (0, LANE - 8))).reshape(ML_H // 2, 1, LANE)
        gbt = jnp.broadcast_to(gb5[:, :, None], (ML_H // 2, 8, ML_L))
        m = mlstm(proj, gates, gates_t, gb, gbt, ml_norm_g[l].reshape(1, -1).astype(F32), lc)

        y = merge(a.reshape(R, -1), f.reshape(R, -1), m.reshape(R, -1), proj.reshape(R, C_TOT),
                  w_br_attn[l].astype(BF16), w_br_four[l].astype(BF16), w_br_mlstm[l].astype(BF16),
                  b_gate[l].reshape(3, 1, D).astype(F32))
        xa = out_proj_residual(y.reshape(B, S, D), w_out[l].astype(BF16), xa, mod3, 2, nct)

        wr = jnp.concatenate([w_grp[l], w_exp_router[l].reshape(D, N_EXP)], axis=1)
        wr = jnp.pad(wr, ((0, 0), (0, LANE - N_GRP - N_EXP))).astype(F32)
        wr_hi = wr.astype(BF16)
        wr_lo = (wr - wr_hi.astype(F32)).astype(BF16)
        br = jnp.pad(jnp.concatenate([b_grp[l], b_exp_router[l].reshape(N_EXP)]),
                     (0, LANE - N_GRP - N_EXP)).reshape(1, LANE).astype(F32)
        hp, eid, wts = norm_router(xa, norm2_g[l], mod3, 3, 4, nct, wr_hi, wr_lo, br)
        dest, src, block_e, nused = moe_plan(eid[..., :2].reshape(n_pairs), n_blocks)
        xs = dispatch(hp.reshape(R, D // 2), src, nused, n_blocks)
        ys = expert_ffn(xs, block_e, nused, w1[l].astype(BF16), w3[l].astype(BF16), w2[l].astype(BF16), n_blocks)
        xa = combine(ys, dest, xa, wts, mod3, 5, nct)

    return xa[:, lc:, :]
```

```python
import functools

import numpy as np
import jax
import jax.numpy as jnp
from jax import lax
from jax.experimental import pallas as pl
from jax.experimental.pallas import tpu as pltpu

F32 = jnp.float32
BF16 = jnp.bfloat16

D = 2048
N_HEADS, N_KV, HD = 16, 4, 64
GRID_W = 64
ROPE_BASE = 10000.0
EPS = 1e-6
ML_H, ML_DK, ML_DV = 8, 64, 128
ML_L = 128
FG, FGD = 4, 256
N_GRP, EPG, N_EXP, FF = 4, 8, 32, 1024
NEG = -1e30
LOG2E = 1.4426950408889634
LANE = 128
ROW_T = 256
MOE_T = 256

C_Q, C_K, C_V = 0, 1024, 1536
C_MQ, C_MK, C_MV, C_MO = 2048, 2560, 3072, 4096
C_F, C_GP, C_TOT = 5120, 6144, 12288


def _cp(n_axes, vmem_mb):
    return pltpu.CompilerParams(dimension_semantics=("arbitrary",) * n_axes,
                                vmem_limit_bytes=int(vmem_mb) * 2 ** 20)


def _pick(n, cands):
    for c in cands:
        if n % c == 0:
            return c
    raise ValueError(f"no tile for {n} in {cands}")


def _dot(a, b):
    return jnp.dot(a, b, preferred_element_type=F32)


def _dot_nt(a, b):
    return lax.dot_general(a, b, (((1,), (1,)), ((), ())), preferred_element_type=F32)


def _split(x):
    hi = x.astype(BF16)
    return hi, (x - hi.astype(F32)).astype(BF16)


def _dot3(a, w):
    ah, al = _split(a)
    wh, wl = _split(w)
    return _dot(ah, wh) + _dot(al, wh) + _dot(ah, wl)


def _mod_kernel(a_ref, w_ref, b_ref, o_ref):
    a = a_ref[...]
    a = a * jax.nn.sigmoid(a)
    o_ref[...] = _dot3(a, w_ref[...]) + b_ref[...]


def modulation(cvec, w_mod, b_mod):
    n = w_mod.shape[1]
    tn = 512
    return pl.pallas_call(
        _mod_kernel,
        out_shape=jax.ShapeDtypeStruct((8, n), F32),
        grid=(n // tn,),
        in_specs=[pl.BlockSpec((8, D), lambda j: (0, 0)),
                  pl.BlockSpec((D, tn), lambda j: (0, j)),
                  pl.BlockSpec((1, tn), lambda j: (0, j))],
        out_specs=pl.BlockSpec((8, tn), lambda j: (0, j)),
        compiler_params=_cp(1, 24), name="modulation",
    )(cvec, w_mod, b_mod.reshape(1, n))


def _mod_spec(k, nct, nb):
    return pl.BlockSpec((None, 1, D), lambda b, j: (jnp.where(j < nct, nb, b), 0, k))


def _norm_kernel(x_ref, g_ref, sc_ref, sh_ref, o_ref):
    x = x_ref[...]
    y = x * lax.rsqrt(jnp.mean(x * x, axis=-1, keepdims=True) + EPS) * g_ref[...]
    o_ref[...] = (y * (1.0 + sc_ref[...]) + sh_ref[...]).astype(o_ref.dtype)


def norm_mod(xa, g, mod3, k_sh, k_sc, nct):
    B, S, _ = xa.shape
    return pl.pallas_call(
        _norm_kernel,
        out_shape=jax.ShapeDtypeStruct((B, S, D), BF16),
        grid=(B, S // ROW_T),
        in_specs=[pl.BlockSpec((None, ROW_T, D), lambda b, j: (b, j, 0)),
                  pl.BlockSpec((1, D), lambda b, j: (0, 0)),
                  _mod_spec(k_sc, nct, B), _mod_spec(k_sh, nct, B)],
        out_specs=pl.BlockSpec((None, ROW_T, D), lambda b, j: (b, j, 0)),
        compiler_params=_cp(2, 24), name="norm_mod",
    )(xa, g.reshape(1, D), mod3, mod3)


def _mm_kernel(a_ref, b_ref, o_ref):
    o_ref[...] = _dot(a_ref[...], b_ref[...]).astype(o_ref.dtype)


def mm(a, b, out_dtype, tn):
    M, K = a.shape
    N = b.shape[1]
    tm = _pick(M, (1024, 512, 256))
    return pl.pallas_call(
        _mm_kernel,
        out_shape=jax.ShapeDtypeStruct((M, N), out_dtype),
        grid=(M // tm, N // tn),
        in_specs=[pl.BlockSpec((tm, K), lambda i, j: (i, 0)),
                  pl.BlockSpec((K, tn), lambda i, j: (0, j))],
        out_specs=pl.BlockSpec((tm, tn), lambda i, j: (i, j)),
        compiler_params=_cp(2, 40), name="mm",
    )(a, b)


def _prep_kernel(x_ref, g_ref, cos_ref, sin_ref, bd_ref, o_ref):
    cosv = cos_ref[...]
    sinv = sin_ref[...]
    bd = bd_ref[...]
    lane = lax.broadcasted_iota(jnp.int32, cosv.shape, 1)
    even = (lane & 1) == 0
    for s in range((C_V - C_Q) // LANE):
        sl = slice(s * LANE, (s + 1) * LANE)
        x = x_ref[:, sl].astype(F32)
        hi, lo = _split(x * x)
        ss = _dot(hi, bd) + _dot(lo, bd)
        y = x * lax.rsqrt(ss * (1.0 / HD) + EPS) * g_ref[:, sl]
        sw = jnp.where(even, pltpu.roll(y, LANE - 1, 1), pltpu.roll(y, 1, 1))
        o_ref[:, sl] = (y * cosv + sw * sinv).astype(o_ref.dtype)


def qk_prep(proj, gain, cos_t, sin_t, bd):
    B, S, _ = proj.shape
    w = C_V - C_Q
    return pl.pallas_call(
        _prep_kernel,
        out_shape=jax.ShapeDtypeStruct((B, S, w), BF16),
        grid=(B, S // ROW_T),
        in_specs=[pl.BlockSpec((None, ROW_T, w), lambda b, j: (b, j, 0)),
                  pl.BlockSpec((1, w), lambda b, j: (0, 0)),
                  pl.BlockSpec((ROW_T, LANE), lambda b, j: (j, 0)),
                  pl.BlockSpec((ROW_T, LANE), lambda b, j: (j, 0)),
                  pl.BlockSpec((LANE, LANE), lambda b, j: (0, 0))],
        out_specs=pl.BlockSpec((None, ROW_T, w), lambda b, j: (b, j, 0)),
        compiler_params=_cp(2, 24), name="qk_prep",
    )(proj, gain, cos_t, sin_t, bd)


AB = 128


def _attn_kernel(sink_ref, q_ref, kp_ref, kc_ref, kn_ref, kx_ref, vp_ref, vc_ref, vn_ref, vx_ref, o_ref,
                 *, nctb, nblk):
    n = pl.program_id(1)
    is_lat = n >= nctb
    iq = lax.broadcasted_iota(jnp.int32, (AB, AB), 0)
    ik = lax.broadcasted_iota(jnp.int32, (AB, AB), 1)
    ok_p = jnp.logical_and(ik >= iq, jnp.logical_and(is_lat, n - 1 >= nctb))
    ok_c = jnp.logical_and(ik >= 0, is_lat)
    ok_n = jnp.logical_and(ik <= iq, jnp.logical_and(is_lat, n + 1 <= nblk - 1))
    nloc = 3 * AB
    bias = jnp.concatenate([jnp.where(ok_p, 0.0, NEG), jnp.where(ok_c, 0.0, NEG), jnp.where(ok_n, 0.0, NEG)], axis=1)
    kall = jnp.concatenate([kp_ref[...], kc_ref[...], kn_ref[...], kx_ref[...]], axis=0)
    vall = jnp.concatenate([vp_ref[...], vc_ref[...], vn_ref[...], vx_ref[...]], axis=0)
    ones = jnp.ones((kall.shape[0], LANE), BF16)
    lane = lax.broadcasted_iota(jnp.int32, (AB, LANE), 1)
    lo_half = lane < HD
    hi_half = jnp.logical_not(lo_half)
    gq = N_HEADS // N_KV
    for g in range(N_KV):
        kg = kall[:, g * LANE:(g + 1) * LANE]
        vext = jnp.concatenate([vall[:, g * LANE:(g + 1) * LANE], ones], axis=1)
        qparts = []
        for jj in range(2):
            qs = q_ref[:, (2 * g + jj) * LANE:(2 * g + jj + 1) * LANE]
            qparts += [jnp.where(lo_half, qs, jnp.zeros_like(qs)), jnp.where(hi_half, qs, jnp.zeros_like(qs))]
        s = _dot_nt(jnp.concatenate(qparts, axis=0), kg)
        ps, sink_w = [], []
        for seg in range(gq):
            sl = s[seg * AB:(seg + 1) * AB]
            s_loc = sl[:, :nloc] + bias
            s_ctx = sl[:, nloc:]
            sk = sink_ref[gq * g + seg]
            m = jnp.maximum(jnp.maximum(jnp.max(s_loc, axis=-1, keepdims=True),
                                        jnp.max(s_ctx, axis=-1, keepdims=True)), sk)
            ps.append(jnp.concatenate([jnp.exp2(s_loc - m), jnp.exp2(s_ctx - m)], axis=1).astype(BF16))
            sink_w.append(jnp.exp2(sk - m))
        o = _dot(jnp.concatenate(ps, axis=0), vext)
        for jj in range(2):
            outs = []
            for half in range(2):
                seg = 2 * jj + half
                rows = slice(seg * AB, (seg + 1) * AB)
                outs.append(o[rows, :LANE] / (o[rows, LANE:] + sink_w[seg]))
            sl_out = slice((2 * g + jj) * LANE, (2 * g + jj + 1) * LANE)
            o_ref[:, sl_out] = jnp.where(lo_half, outs[0], outs[1]).astype(o_ref.dtype)


def attention(qkn, proj, sink, lc):
    B, S, _ = qkn.shape
    nblk = S // AB
    nctb = lc // AB
    kw = C_V - C_K
    kcol, vcol = C_K // kw, C_V // kw

    def rows(fn, col):
        return pl.BlockSpec((None, AB, kw), lambda b, n: (b, fn(n), col))

    prev = lambda n: jnp.maximum(n - 1, 0)
    cur = lambda n: n
    nxt = lambda n: jnp.minimum(n + 1, nblk - 1)
    ctx = lambda col: pl.BlockSpec((None, lc, kw), lambda b, n: (b, 0, col))
    return pl.pallas_call(
        functools.partial(_attn_kernel, nctb=nctb, nblk=nblk),
        out_shape=jax.ShapeDtypeStruct((B, S, C_K), BF16),
        grid=(B, nblk),
        in_specs=[pl.BlockSpec(memory_space=pltpu.SMEM),
                  pl.BlockSpec((None, AB, C_K), lambda b, n: (b, n, 0)),
                  rows(prev, kcol), rows(cur, kcol), rows(nxt, kcol), ctx(kcol),
                  rows(prev, vcol), rows(cur, vcol), rows(nxt, vcol), ctx(vcol)],
        out_specs=pl.BlockSpec((None, AB, C_K), lambda b, n: (b, n, 0)),
        compiler_params=_cp(2, 32), name="attention",
    )(sink, qkn, qkn, qkn, qkn, qkn, proj, proj, proj, proj)


def _dftc_kernel(u_ref, w_ref, o_ref):
    o_ref[...] = _dot(u_ref[...], w_ref[...]).astype(o_ref.dtype)


def dft_channels(proj, wc, row0, nrows):
    B = proj.shape[0]
    off = row0 // ROW_T
    return pl.pallas_call(
        _dftc_kernel,
        out_shape=jax.ShapeDtypeStruct((B, nrows, 2 * FG * FGD), BF16),
        grid=(B, nrows // ROW_T),
        in_specs=[pl.BlockSpec((None, ROW_T, FG * FGD), lambda b, j: (b, j + off, C_F // (FG * FGD))),
                  pl.BlockSpec((FG * FGD, 2 * FG * FGD), lambda b, j: (0, 0))],
        out_specs=pl.BlockSpec((None, ROW_T, 2 * FG * FGD), lambda b, j: (b, j, 0)),
        compiler_params=_cp(2, 32), name="dft_channels",
    )(proj, wc)


def _dftp_kernel(c_ref, s_ref, y1_ref, y2_ref, o_ref, acc_ref):
    k = pl.program_id(2)

    @pl.when(k == 0)
    def _():
        acc_ref[...] = jnp.zeros_like(acc_ref)

    acc_ref[...] += _dot(c_ref[...], y1_ref[...]) + _dot(s_ref[...], y2_ref[...])

    @pl.when(k == pl.num_programs(2) - 1)
    def _():
        o_ref[...] = acc_ref[...].astype(o_ref.dtype)


def dft_positions(ct, st, y):
    B, n, w2 = y.shape
    w = w2 // 2
    t = _pick(n, (1024, 512, 256))
    return pl.pallas_call(
        _dftp_kernel,
        out_shape=jax.ShapeDtypeStruct((B, n, w), BF16),
        grid=(B, n // t, n // t),
        in_specs=[pl.BlockSpec((t, t), lambda b, i, k: (i, k)),
                  pl.BlockSpec((t, t), lambda b, i, k: (i, k)),
                  pl.BlockSpec((None, t, w), lambda b, i, k: (b, k, 0)),
                  pl.BlockSpec((None, t, w), lambda b, i, k: (b, k, 1))],
        out_specs=pl.BlockSpec((None, t, w), lambda b, i, k: (b, i, 0)),
        scratch_shapes=[pltpu.VMEM((t, w), F32)],
        compiler_params=_cp(3, 40), name="dft_positions",
    )(ct, st, y, y)


def _log_sigmoid(x):
    return jnp.minimum(x, 0.0) - jnp.log(1.0 + jnp.exp(-jnp.abs(x)))


def _gate_kernel(g_ref, b_ref, o_ref):
    L = ML_L
    g = g_ref[...] + b_ref[...]
    lane = lax.broadcasted_iota(jnp.int32, (L, LANE), 1)
    is_f = (lane & ML_H) == ML_H
    is_bwd = (lane & (2 * ML_H)) == 2 * ML_H
    lf = jnp.where(is_f, _log_sigmoid(g), 0.0)
    it_r = lax.broadcasted_iota(jnp.int32, (L, L), 0)
    it_c = lax.broadcasted_iota(jnp.int32, (L, L), 1)
    hp = lax.Precision.HIGHEST
    cf = jnp.dot(jnp.where(it_c <= it_r, 1.0, 0.0).astype(F32), lf, precision=hp, preferred_element_type=F32)
    cb = jnp.dot(jnp.where(it_c >= it_r, 1.0, 0.0).astype(F32), lf, precision=hp, preferred_element_type=F32)
    o_ref[...] = jnp.where(is_f, jnp.where(is_bwd, cb, cf), g)


def gate_prep(graw, bias):
    B, S, _ = graw.shape
    return pl.pallas_call(
        _gate_kernel,
        out_shape=jax.ShapeDtypeStruct((B, S, LANE), F32),
        grid=(B, S // ML_L),
        in_specs=[pl.BlockSpec((None, ML_L, LANE), lambda b, c: (b, c, 0)),
                  pl.BlockSpec((1, LANE), lambda b, c: (0, 0))],
        out_specs=pl.BlockSpec((None, ML_L, LANE), lambda b, c: (b, c, 0)),
        compiler_params=_cp(2, 16), name="gate_prep",
    )(graw, bias)


def _mlstm_kernel(q_ref, k_ref, v_ref, mo_ref, g_ref, gt_ref, ng_ref, o_ref,
                  hf_ref, hb_ref, c_ref, m_ref, *, nctc, nchunk):
    L = ML_L
    c_ref[...] = jnp.zeros_like(c_ref)
    m_ref[...] = jnp.zeros_like(m_ref)
    it_r = lax.broadcasted_iota(jnp.int32, (L, L), 0)
    it_c = lax.broadcasted_iota(jnp.int32, (L, L), 1)
    tri = (it_c <= it_r, it_c >= it_r)
    lane = lax.broadcasted_iota(jnp.int32, (L, LANE), 1)
    head_lanes = (lane < ML_DK, lane >= ML_DK)
    ones = jnp.ones((L, LANE), BF16)
    h_refs = (hf_ref, hb_ref)

    def step(it, carry):
        stores = []
        for d in range(2):
            if d == 0:
                c = it
            else:
                c = jnp.where(it < nctc, nctc - 1 - it, nchunk - 1 - (it - nctc))
            r0 = pl.multiple_of(c * L, L)
            gc = g_ref[pl.ds(r0, L), :]
            gr = gt_ref[c]
            qp = q_ref[pl.ds(r0, L), :] * jnp.asarray(ML_DK ** -0.5, BF16)
            kp = k_ref[pl.ds(r0, L), :]
            last = L - 1 if d == 0 else 0
            for hh in range(2):
                icol = d * 4 + hh
                fcol = d * 4 + 2 + hh
                sidx = d * 2 + hh
                i_col = gc[:, icol:icol + 1]
                b_col = gc[:, fcol:fcol + 1]
                row_term = gr[icol:icol + 1, :] - gr[fcol:fcol + 1, :]
                m_prev = m_ref[sidx:sidx + 1, 0:1]
                logd = jnp.where(tri[d], b_col + row_term, NEG)
                inter = b_col + m_prev
                m_t = jnp.maximum(inter, jnp.max(logd, axis=-1, keepdims=True))
                dm = jnp.exp(logd - m_t)
                qm = jnp.where(head_lanes[hh], qp, jnp.zeros_like(qp))
                km = jnp.where(head_lanes[hh], kp, jnp.zeros_like(kp))
                s = (_dot_nt(qm, km) * dm).astype(BF16)
                vext = jnp.concatenate([v_ref[pl.ds(r0, L), hh * ML_DV:(hh + 1) * ML_DV], ones], axis=1)
                cst = c_ref[sidx]
                tot = _dot(s, vext) + jnp.exp(inter - m_t) * _dot(qm, cst.astype(BF16))
                h = tot[:, :ML_DV] / jnp.maximum(jnp.abs(tot[:, ML_DV:]), jnp.exp(-m_t))
                b_last = b_col[last:last + 1, :]
                log_w = b_last - b_col + i_col
                m_new = jnp.maximum(b_last + m_prev, jnp.max(log_w, axis=0, keepdims=True))
                kw = (km.astype(F32) * jnp.exp(log_w - m_new)).astype(BF16)
                dc = lax.dot_general(kw, vext, (((0,), (0,)), ((), ())), preferred_element_type=F32)
                c_new = jnp.exp(b_last + m_prev - m_new) * cst + dc
                stores.append((d, hh, sidx, r0, h, c_new, m_new))
        for d, hh, sidx, r0, h, c_new, m_new in stores:
            h_refs[d][pl.ds(r0, L), hh * ML_DV:(hh + 1) * ML_DV] = h
            c_ref[sidx] = c_new
            m_ref[sidx:sidx + 1, :] = jnp.broadcast_to(m_new, (1, LANE))
        return carry

    lax.fori_loop(0, nchunk, step, 0)

    def out_step(i, carry):
        r0 = pl.multiple_of(i * L, L)
        for hh in range(2):
            hs = slice(hh * ML_DV, (hh + 1) * ML_DV)
            h = hf_ref[pl.ds(r0, L), hs] + hb_ref[pl.ds(r0, L), hs]
            hn = h * lax.rsqrt(jnp.mean(h * h, axis=-1, keepdims=True) + EPS) * ng_ref[:, hs]
            og = jax.nn.sigmoid(mo_ref[pl.ds(r0, L), hs].astype(F32))
            o_ref[pl.ds(r0, L), hs] = (og * hn).astype(o_ref.dtype)
        return carry

    lax.fori_loop(0, nchunk, out_step, 0)


def mlstm(proj, gates, gates_t, norm_g, lc):
    B, S, _ = proj.shape
    nchunk = S // ML_L
    pw = 2 * ML_DK
    vw = 2 * ML_DV
    return pl.pallas_call(
        functools.partial(_mlstm_kernel, nctc=lc // ML_L, nchunk=nchunk),
        out_shape=jax.ShapeDtypeStruct((B, S, ML_H * ML_DV), BF16),
        grid=(B, ML_H // 2),
        in_specs=[pl.BlockSpec((None, S, pw), lambda b, p: (b, 0, C_MQ // pw + p)),
                  pl.BlockSpec((None, S, pw), lambda b, p: (b, 0, C_MK // pw + p)),
                  pl.BlockSpec((None, S, vw), lambda b, p: (b, 0, C_MV // vw + p)),
                  pl.BlockSpec((None, S, vw), lambda b, p: (b, 0, C_MO // vw + p)),
                  pl.BlockSpec((None, None, S, LANE), lambda b, p: (b, p, 0, 0)),
                  pl.BlockSpec((None, None, nchunk, 8, ML_L), lambda b, p: (b, p, 0, 0, 0)),
                  pl.BlockSpec((1, vw), lambda b, p: (0, p))],
        out_specs=pl.BlockSpec((None, S, vw), lambda b, p: (b, 0, p)),
        scratch_shapes=[pltpu.VMEM((S, vw), F32), pltpu.VMEM((S, vw), F32),
                        pltpu.VMEM((4, 2 * ML_DK, 2 * ML_DV), F32), pltpu.VMEM((8, LANE), F32)],
        compiler_params=_cp(2, 48), name="mlstm",
    )(proj, proj, proj, proj, gates, gates_t, norm_g)


def _merge_kernel(a_ref, f_ref, m_ref, wa_ref, wf_ref, wm_ref, ga_ref, gf_ref, gm_ref,
                  ba_ref, bf_ref, bm_ref, o_ref):
    def br(x_ref, w_ref, g_ref, b_ref):
        return jax.nn.sigmoid(g_ref[...].astype(F32) + b_ref[...]) * _dot(x_ref[...], w_ref[...])

    y = br(a_ref, wa_ref, ga_ref, ba_ref) + br(f_ref, wf_ref, gf_ref, bf_ref) + br(m_ref, wm_ref, gm_ref, bm_ref)
    o_ref[...] = y.astype(o_ref.dtype)


def merge(a, f, m, proj, wa, wf, wm, b_gate):
    R, kin = a.shape
    tm = _pick(R, (512, 256))
    tn = 1024
    x_spec = pl.BlockSpec((tm, kin), lambda j, i: (i, 0))
    w_spec = pl.BlockSpec((kin, tn), lambda j, i: (0, j))
    gp = lambda br: pl.BlockSpec((tm, tn), lambda j, i: (i, (C_GP + br * D) // tn + j))
    bg = lambda br: pl.BlockSpec((None, 1, tn), lambda j, i: (br, 0, j))
    return pl.pallas_call(
        _merge_kernel,
        out_shape=jax.ShapeDtypeStruct((R, D), BF16),
        grid=(D // tn, R // tm),
        in_specs=[x_spec, x_spec, x_spec, w_spec, w_spec, w_spec, gp(0), gp(1), gp(2), bg(0), bg(1), bg(2)],
        out_specs=pl.BlockSpec((tm, tn), lambda j, i: (i, j)),
        compiler_params=_cp(2, 40), name="merge",
    )(a, f, m, wa, wf, wm, proj, proj, proj, b_gate, b_gate, b_gate)


def _outproj_kernel(y_ref, w_ref, x_ref, g_ref, o_ref):
    o_ref[...] = x_ref[...] + g_ref[...] * _dot(y_ref[...], w_ref[...])


def out_proj_residual(y, w_out, xa, mod3, k_gate, nct):
    B, S, _ = xa.shape
    return pl.pallas_call(
        _outproj_kernel,
        out_shape=jax.ShapeDtypeStruct((B, S, D), F32),
        grid=(B, S // ROW_T),
        in_specs=[pl.BlockSpec((None, ROW_T, D), lambda b, j: (b, j, 0)),
                  pl.BlockSpec((D, D), lambda b, j: (0, 0)),
                  pl.BlockSpec((None, ROW_T, D), lambda b, j: (b, j, 0)),
                  _mod_spec(k_gate, nct, B)],
        out_specs=pl.BlockSpec((None, ROW_T, D), lambda b, j: (b, j, 0)),
        compiler_params=_cp(2, 40), name="out_proj",
    )(y, w_out, xa, mod3)


def _pack2(lo, hi):
    lo_b = lax.bitcast_convert_type(lo.astype(BF16).astype(F32), jnp.uint32)
    hi_b = lax.bitcast_convert_type(hi.astype(BF16).astype(F32), jnp.uint32)
    return (lo_b >> 16) | (hi_b & jnp.uint32(0xFFFF0000))


def _unpack2(w):
    lo = lax.bitcast_convert_type(w << 16, F32)
    hi = lax.bitcast_convert_type(w & jnp.uint32(0xFFFF0000), F32)
    return lo, hi


def _router_kernel(x_ref, g_ref, sc_ref, sh_ref, wh_ref, wl_ref, b_ref, hp_ref, eid_ref, wt_ref):
    x = x_ref[...]
    y = x * lax.rsqrt(jnp.mean(x * x, axis=-1, keepdims=True) + EPS) * g_ref[...]
    y = y * (1.0 + sc_ref[...]) + sh_ref[...]
    hp_ref[...] = _pack2(y[:, :D // 2], y[:, D // 2:])
    yh, yl = _split(y)
    logits = _dot(yh, wh_ref[...]) + _dot(yl, wh_ref[...]) + _dot(yh, wl_ref[...]) + b_ref[...]
    lane = lax.broadcasted_iota(jnp.int32, logits.shape, 1)
    lanef = lane.astype(F32)
    big = float(LANE)
    glog = jnp.where(lane < N_GRP, logits, NEG)
    gmax = jnp.max(glog, axis=-1, keepdims=True)
    gsel = jnp.min(jnp.where(glog == gmax, lanef, big), axis=-1, keepdims=True)
    pg = 1.0 / jnp.sum(jnp.exp(glog - gmax), axis=-1, keepdims=True)
    lo = N_GRP + EPG * gsel
    el = jnp.where(jnp.logical_and(lanef >= lo, lanef < lo + EPG), logits, NEG)
    v1 = jnp.max(el, axis=-1, keepdims=True)
    i1 = jnp.min(jnp.where(el == v1, lanef, big), axis=-1, keepdims=True)
    el2 = jnp.where(lanef == i1, NEG, el)
    v2 = jnp.max(el2, axis=-1, keepdims=True)
    i2 = jnp.min(jnp.where(el2 == v2, lanef, big), axis=-1, keepdims=True)
    e = jnp.exp(v2 - v1)
    w1 = pg / (1.0 + e)
    w2 = pg * e / (1.0 + e)
    eid_ref[...] = jnp.where(lane == 0, i1 - N_GRP, jnp.where(lane == 1, i2 - N_GRP, 0.0)).astype(jnp.int32)
    wt_ref[...] = jnp.where(lane == 0, w1, jnp.where(lane == 1, w2, 0.0))


def norm_router(xa, g, mod3, k_sh, k_sc, nct, wr_hi, wr_lo, br):
    B, S, _ = xa.shape
    tile = lambda w: pl.BlockSpec((None, ROW_T, w), lambda b, j: (b, j, 0))
    const = lambda shape: pl.BlockSpec(shape, lambda b, j: (0, 0))
    return pl.pallas_call(
        _router_kernel,
        out_shape=(jax.ShapeDtypeStruct((B, S, D // 2), jnp.uint32),
                   jax.ShapeDtypeStruct((B, S, LANE), jnp.int32),
                   jax.ShapeDtypeStruct((B, S, LANE), F32)),
        grid=(B, S // ROW_T),
        in_specs=[tile(D), const((1, D)), _mod_spec(k_sc, nct, B), _mod_spec(k_sh, nct, B),
                  const((D, LANE)), const((D, LANE)), const((1, LANE))],
        out_specs=(tile(D // 2), tile(LANE), tile(LANE)),
        compiler_params=_cp(2, 32), name="norm_router",
    )(xa, g.reshape(1, D), mod3, mod3, wr_hi, wr_lo, br)


DMA_UNROLL = 8


def _row_copy(src_hbm, srow, dst_hbm, drow, sem):
    return pltpu.make_async_copy(src_hbm.at[pl.ds(srow, 1), :], dst_hbm.at[pl.ds(drow, 1), :], sem)


def _wait_rows(src_hbm, dst_hbm, sem, n):
    def wait(r, carry):
        _row_copy(src_hbm, 0, dst_hbm, 0, sem).wait()
        return carry

    lax.fori_loop(0, n, wait, 0)


def _permute_kernel(order_ref, rstart_ref, nvalid_ref, nused_ref, x_hbm, o_hbm, sem, *, n_tok, gather):
    i = pl.program_id(0)
    nused = nused_ref[0]
    used = i < nused

    def rows_of(step):
        return MOE_T if gather else nvalid_ref[step]

    @pl.when(used)
    def _():
        rs = rstart_ref[i]
        nv = nvalid_ref[i]
        base = i * MOE_T

        def one(r, prio):
            p = order_ref[rs + jnp.minimum(r, nv - 1)]
            if gather:
                tok = jnp.where(p >= n_tok, p - n_tok, p)
                _row_copy(x_hbm, tok, o_hbm, base + r, sem).start(priority=prio)
            else:
                _row_copy(x_hbm, base + r, o_hbm, p, sem).start(priority=prio)

        def group(k, carry):
            for u in range(DMA_UNROLL):
                one(k * DMA_UNROLL + u, u % 2)
            return carry

        if gather:
            lax.fori_loop(0, MOE_T // DMA_UNROLL, group, 0)
        else:
            full = nv // DMA_UNROLL
            lax.fori_loop(0, full, group, 0)

            def tail(r, carry):
                one(r, 0)
                return carry

            lax.fori_loop(full * DMA_UNROLL, nv, tail, 0)

    @pl.when(jnp.logical_and(i > 0, i - 1 < nused))
    def _():
        _wait_rows(x_hbm, o_hbm, sem, rows_of(i - 1))

    @pl.when(jnp.logical_and(i == pl.num_programs(0) - 1, used))
    def _():
        _wait_rows(x_hbm, o_hbm, sem, rows_of(i))


def permute_rows(x, plan, n_blocks, n_out, n_tok, gather):
    order, rstart, nvalid, _, nused = plan
    w = x.shape[1]
    grid_spec = pltpu.PrefetchScalarGridSpec(
        num_scalar_prefetch=4, grid=(n_blocks,),
        in_specs=[pl.BlockSpec(memory_space=pl.ANY)],
        out_specs=pl.BlockSpec(memory_space=pl.ANY),
        scratch_shapes=[pltpu.SemaphoreType.DMA(())])
    return pl.pallas_call(
        functools.partial(_permute_kernel, n_tok=n_tok, gather=gather),
        out_shape=jax.ShapeDtypeStruct((n_out, w), x.dtype),
        grid_spec=grid_spec, compiler_params=_cp(1, 16),
        name="moe_dispatch" if gather else "moe_unsort",
    )(order, rstart, nvalid, nused, x)


def _expert_kernel(be_ref, nused_ref, x_ref, w1_ref, w3_ref, w2_ref, o_ref):
    i = pl.program_id(0)

    @pl.when(i < nused_ref[0])
    def _():
        xl, xh = _unpack2(x_ref[...])
        xl = xl.astype(BF16)
        xh = xh.astype(BF16)
        half = D // 2
        a = _dot(xl, w1_ref[:half, :]) + _dot(xh, w1_ref[half:, :])
        b = _dot(xl, w3_ref[:half, :]) + _dot(xh, w3_ref[half:, :])
        hmid = (a * jax.nn.sigmoid(a) * b).astype(BF16)
        y = _dot(hmid, w2_ref[...])
        o_ref[...] = _pack2(y[:, :half], y[:, half:])

    @pl.when(i >= nused_ref[0])
    def _():
        o_ref[...] = jnp.zeros_like(o_ref)


def expert_ffn(xs, block_e, nused, w1, w3, w2, n_blocks):
    w = xs.shape[1]
    wspec = lambda shape: pl.BlockSpec((None,) + shape, lambda i, be, n: (be[i], 0, 0))
    grid_spec = pltpu.PrefetchScalarGridSpec(
        num_scalar_prefetch=2, grid=(n_blocks,),
        in_specs=[pl.BlockSpec((MOE_T, w), lambda i, be, n: (jnp.minimum(i, n[0] - 1), 0)),
                  wspec((D, FF)), wspec((D, FF)), wspec((FF, D))],
        out_specs=pl.BlockSpec((MOE_T, w), lambda i, be, n: (i, 0)))
    return pl.pallas_call(
        _expert_kernel,
        out_shape=jax.ShapeDtypeStruct((n_blocks * MOE_T, w), jnp.uint32),
        grid_spec=grid_spec, compiler_params=_cp(1, 48), name="moe_experts",
    )(block_e, nused, xs, w1, w3, w2)


def _combine_kernel(ya_ref, yb_ref, x_ref, wt_ref, g_ref, o_ref):
    w0 = wt_ref[:, 0:1]
    w1 = wt_ref[:, 1:2]
    al, ah = _unpack2(ya_ref[...])
    bl, bh = _unpack2(yb_ref[...])
    half = D // 2
    g = g_ref[...]
    o_ref[:, :half] = x_ref[:, :half] + g[:, :half] * (w0 * al + w1 * bl)
    o_ref[:, half:] = x_ref[:, half:] + g[:, half:] * (w0 * ah + w1 * bh)


def combine(y2, xa, wts, mod3, k_gate, nct):
    B, S, _ = xa.shape
    w = y2.shape[-1]
    tile = lambda wd: pl.BlockSpec((None, ROW_T, wd), lambda b, j: (b, j, 0))
    pick = lambda k: pl.BlockSpec((None, None, ROW_T, w), lambda b, j: (k, b, j, 0))
    return pl.pallas_call(
        _combine_kernel,
        out_shape=jax.ShapeDtypeStruct((B, S, D), F32),
        grid=(B, S // ROW_T),
        in_specs=[pick(0), pick(1), tile(D), tile(LANE), _mod_spec(k_gate, nct, B)],
        out_specs=tile(D),
        compiler_params=_cp(2, 32), name="moe_combine",
    )(y2, y2, xa, wts, mod3)


def moe_plan(eid, n_blocks):
    order = jnp.argsort(eid).astype(jnp.int32)
    experts = jnp.arange(N_EXP, dtype=jnp.int32)
    counts = jnp.sum((eid[:, None] == experts[None, :]).astype(jnp.int32), axis=0)
    nblk_e = (counts + MOE_T - 1) // MOE_T
    bend = jnp.cumsum(nblk_e)
    bstart = bend - nblk_e
    start = jnp.cumsum(counts) - counts
    blk = jnp.arange(n_blocks, dtype=jnp.int32)
    owner = jnp.logical_and(blk[:, None] >= bstart[None, :], blk[:, None] < bend[None, :]).astype(jnp.int32)
    take = lambda v: jnp.sum(owner * v[None, :], axis=1)
    within = blk - take(bstart)
    nused = bend[-1]
    last_e = jnp.max(jnp.where(counts > 0, experts, 0))
    block_e = jnp.where(blk < nused, take(experts), last_e).astype(jnp.int32)
    rstart = jnp.where(blk < nused, take(start) + within * MOE_T, 0).astype(jnp.int32)
    nvalid = jnp.where(blk < nused, jnp.clip(take(counts) - within * MOE_T, 0, MOE_T), 0).astype(jnp.int32)
    return order, rstart, nvalid, block_e, nused.astype(jnp.int32).reshape(1)


def _rope_tables(s, lc):
    t = np.arange(s - lc)
    row = (t // GRID_W).astype(np.float64)
    col = (t % GRID_W).astype(np.float64)
    nf = HD // 4
    inv = np.float32(ROPE_BASE) ** (-np.arange(nf, dtype=np.float32) / np.float32(nf))
    ang = np.concatenate([row[:, None] * inv, col[:, None] * inv], axis=-1)
    ang = np.concatenate([np.zeros((lc, HD // 2)), ang], axis=0)
    cos = np.repeat(np.cos(ang), 2, axis=-1)
    sin = np.repeat(np.sin(ang), 2, axis=-1) * np.tile([-1.0, 1.0], HD // 2)
    return jnp.asarray(np.tile(cos, (1, LANE // HD)), F32), jnp.asarray(np.tile(sin, (1, LANE // HD)), F32)


def _chan_dft():
    j = np.arange(FGD)
    ang = 2.0 * np.pi * ((j[:, None] * j[None, :]) % FGD) / FGD
    c = np.kron(np.eye(FG), np.cos(ang)) / np.sqrt(FGD)
    s = np.kron(np.eye(FG), np.sin(ang)) / np.sqrt(FGD)
    return jnp.asarray(np.concatenate([c, -s], axis=1), BF16)


def _pos_dft(n):
    if n <= 512:
        k = np.arange(n)
        ang = 2.0 * np.pi * ((k[:, None] * k[None, :]) % n) / n
        return (jnp.asarray(np.cos(ang) / np.sqrt(n), BF16), jnp.asarray(np.sin(ang) / np.sqrt(n), BF16))
    nr = n // LANE
    k = np.arange(n)
    a = 2.0 * np.pi * ((k[:, None] * np.arange(nr)[None, :]) % nr) / nr
    b = 2.0 * np.pi * (k[:, None] * np.arange(LANE)[None, :]) / n
    sc = 1.0 / np.sqrt(n)
    ca, sa = jnp.asarray(np.cos(a) * sc, F32)[:, :, None], jnp.asarray(np.sin(a) * sc, F32)[:, :, None]
    cb, sb = jnp.asarray(np.cos(b), F32)[:, None, :], jnp.asarray(np.sin(b), F32)[:, None, :]
    ct = (ca * cb - sa * sb).reshape(n, n).astype(BF16)
    st = (sa * cb + ca * sb).reshape(n, n).astype(BF16)
    return ct, st


def _proj_weights(w_in):
    dup = lambda w: jnp.concatenate([w.reshape(D, N_KV, HD)] * 2, axis=-1).reshape(D, 2 * N_KV * HD)
    w_big = jnp.concatenate([w_in[:, 0:1024], dup(w_in[:, 1024:1280]), dup(w_in[:, 1280:1536]),
                             w_in[:, 1536:4608], w_in[:, 4640:]], axis=1).astype(BF16)
    w_gate = jnp.pad(w_in[:, 4608:4640], ((0, 0), (0, LANE - 32))).astype(BF16)
    return w_big, w_gate


def _qk_gain(qg, kg):
    qs = jnp.tile(qg * (HD ** -0.5 * LOG2E), N_HEADS)
    return jnp.concatenate([qs, jnp.tile(kg, 2 * N_KV)]).reshape(1, -1).astype(F32)


def kernel(x, c, ctx, c_ctx, w_mod, b_mod, norm1_g, norm2_g, w_in, q_norm_g, k_norm_g, attn_sink, ml_gate_b,
           ml_norm_g, w_br_attn, w_br_four, w_br_mlstm, b_gate, w_out, w_grp, b_grp, w_exp_router,
           b_exp_router, w1, w3, w2):
    B, T, _ = x.shape
    lc = ctx.shape[1]
    S = lc + T
    R = B * S
    depth = w_mod.shape[0]
    nct = lc // ROW_T
    assert lc % ROW_T == 0 and T % ROW_T == 0 and B < 8

    xa = jnp.concatenate([ctx, x], axis=1)
    cvec = jnp.concatenate([c, c_ctx[None], jnp.zeros((7 - B, D), F32)], axis=0)
    cos_t, sin_t = _rope_tables(S, lc)
    bd = jnp.asarray(np.kron(np.eye(LANE // HD), np.ones((HD, HD))), BF16)
    wc = _chan_dft()
    ct_lat, st_lat = _pos_dft(T)
    ct_ctx, st_ctx = _pos_dft(lc)
    nchunk = S // ML_L
    n_pairs = R * 2
    n_blocks = -(-n_pairs // MOE_T) + N_EXP

    for l in range(depth):
        mod = modulation(cvec, w_mod[l], b_mod[l])
        mod3 = mod.reshape(8, 1, 6 * D)
        h = norm_mod(xa, norm1_g[l], mod3, 0, 1, nct)
        w_big, w_gate = _proj_weights(w_in[l])
        h2d = h.reshape(R, D)
        proj = mm(h2d, w_big, BF16, 1024).reshape(B, S, C_TOT)
        graw = mm(h2d, w_gate, F32, LANE).reshape(B, S, LANE)

        qkn = qk_prep(proj, _qk_gain(q_norm_g[l], k_norm_g[l]), cos_t, sin_t, bd)
        a = attention(qkn, proj, attn_sink[l].astype(F32) * LOG2E, lc)

        f_ctx = dft_positions(ct_ctx, st_ctx, dft_channels(proj, wc, 0, lc))
        f_lat = dft_positions(ct_lat, st_lat, dft_channels(proj, wc, lc, T))
        f = jnp.concatenate([f_ctx, f_lat], axis=1)

        gbias = jnp.pad(ml_gate_b[l].reshape(1, 4 * ML_H).astype(F32), ((0, 0), (0, LANE - 4 * ML_H)))
        gproc = gate_prep(graw, gbias)[..., :4 * ML_H]
        g5 = gproc.reshape(B, S, 2, 2, ML_H // 2, 2).transpose(0, 4, 1, 2, 3, 5).reshape(B, ML_H // 2, S, 8)
        gates = jnp.pad(g5, ((0, 0), (0, 0), (0, 0), (0, LANE - 8)))
        gates_t = g5.reshape(B, ML_H // 2, nchunk, ML_L, 8).transpose(0, 1, 2, 4, 3)
        m = mlstm(proj, gates, gates_t, ml_norm_g[l].reshape(1, -1).astype(F32), lc)

        y = merge(a.reshape(R, -1), f.reshape(R, -1), m.reshape(R, -1), proj.reshape(R, C_TOT),
                  w_br_attn[l].astype(BF16), w_br_four[l].astype(BF16), w_br_mlstm[l].astype(BF16),
                  b_gate[l].reshape(3, 1, D).astype(F32))
        xa = out_proj_residual(y.reshape(B, S, D), w_out[l].astype(BF16), xa, mod3, 2, nct)

        wr = jnp.concatenate([w_grp[l], w_exp_router[l].reshape(D, N_EXP)], axis=1)
        wr = jnp.pad(wr, ((0, 0), (0, LANE - N_GRP - N_EXP))).astype(F32)
        wr_hi = wr.astype(BF16)
        wr_lo = (wr - wr_hi.astype(F32)).astype(BF16)
        br = jnp.pad(jnp.concatenate([b_grp[l], b_exp_router[l].reshape(N_EXP)]),
                     (0, LANE - N_GRP - N_EXP)).reshape(1, LANE).astype(F32)
        hp, eid, wts = norm_router(xa, norm2_g[l], mod3, 3, 4, nct, wr_hi, wr_lo, br)
        plan = moe_plan(jnp.concatenate([eid[..., 0].reshape(R), eid[..., 1].reshape(R)]), n_blocks)
        xs = permute_rows(hp.reshape(R, D // 2), plan, n_blocks, n_blocks * MOE_T, R, True)
        ys = expert_ffn(xs, plan[3], plan[4], w1[l].astype(BF16), w3[l].astype(BF16), w2[l].astype(BF16), n_blocks)
        y2 = permute_rows(ys, plan, n_blocks, n_pairs, R, False)
        xa = combine(y2.reshape(2, B, S, D // 2), xa, wts, mod3, 5, nct)

    return xa[:, lc:, :]
```

```python
import functools

import numpy as np
import jax
import jax.numpy as jnp
from jax import lax
from jax.experimental import pallas as pl
from jax.experimental.pallas import tpu as pltpu

F32 = jnp.float32
BF16 = jnp.bfloat16

D = 2048
N_HEADS, N_KV, HD = 16, 4, 64
GRID_W = 64
ROPE_BASE = 10000.0
EPS = 1e-6
ML_H, ML_DK, ML_DV = 8, 64, 128
ML_L = 128
FG, FGD = 4, 256
N_GRP, EPG, N_EXP, FF = 4, 8, 32, 1024
NEG = -1e30
LOG2E = 1.4426950408889634
LANE = 128
ROW_T = 256
MOE_T = 256

C_Q, C_K, C_V = 0, 1024, 1536
C_MQ, C_MK, C_MV, C_MO = 2048, 2560, 3072, 4096
C_F, C_GP, C_TOT = 5120, 6144, 12288


def _cp(n_axes, vmem_mb):
    return pltpu.CompilerParams(dimension_semantics=("arbitrary",) * n_axes,
                                vmem_limit_bytes=int(vmem_mb) * 2 ** 20)


def _pick(n, cands):
    for c in cands:
        if n % c == 0:
            return c
    raise ValueError(f"no tile for {n} in {cands}")


def _dot(a, b):
    return jnp.dot(a, b, preferred_element_type=F32)


def _dot_nt(a, b):
    return lax.dot_general(a, b, (((1,), (1,)), ((), ())), preferred_element_type=F32)


def _split(x):
    hi = x.astype(BF16)
    return hi, (x - hi.astype(F32)).astype(BF16)


def _dot3(a, w):
    ah, al = _split(a)
    wh, wl = _split(w)
    return _dot(ah, wh) + _dot(al, wh) + _dot(ah, wl)


def _mod_kernel(a_ref, w_ref, b_ref, o_ref):
    a = a_ref[...]
    a = a * jax.nn.sigmoid(a)
    o_ref[...] = _dot3(a, w_ref[...]) + b_ref[...]


def modulation(cvec, w_mod, b_mod, layer):
    depth, _, n = w_mod.shape
    tn = 512
    return pl.pallas_call(
        _mod_kernel,
        out_shape=jax.ShapeDtypeStruct((8, n), F32),
        grid=(n // tn,),
        in_specs=[pl.BlockSpec((8, D), lambda j: (0, 0)),
                  pl.BlockSpec((None, D, tn), lambda j: (layer, 0, j)),
                  pl.BlockSpec((None, 1, tn), lambda j: (layer, 0, j))],
        out_specs=pl.BlockSpec((8, tn), lambda j: (0, j)),
        compiler_params=_cp(1, 24), name="modulation",
    )(cvec, w_mod, b_mod.reshape(depth, 1, n))


def _mod_spec(k, nct, nb, off=0):
    return pl.BlockSpec((None, 1, D), lambda b, j: (jnp.where(j + off < nct, nb, b), 0, k))


def _norm_kernel(x_ref, g_ref, sc_ref, sh_ref, o_ref):
    x = x_ref[...]
    y = x * lax.rsqrt(jnp.mean(x * x, axis=-1, keepdims=True) + EPS) * g_ref[...]
    o_ref[...] = (y * (1.0 + sc_ref[...]) + sh_ref[...]).astype(o_ref.dtype)


def norm_mod(xa, g, mod3, k_sh, k_sc, nct):
    B, S, _ = xa.shape
    return pl.pallas_call(
        _norm_kernel,
        out_shape=jax.ShapeDtypeStruct((B, S, D), BF16),
        grid=(B, S // ROW_T),
        in_specs=[pl.BlockSpec((None, ROW_T, D), lambda b, j: (b, j, 0)),
                  pl.BlockSpec((1, D), lambda b, j: (0, 0)),
                  _mod_spec(k_sc, nct, B), _mod_spec(k_sh, nct, B)],
        out_specs=pl.BlockSpec((None, ROW_T, D), lambda b, j: (b, j, 0)),
        compiler_params=_cp(2, 24), name="norm_mod",
    )(xa, g.reshape(1, D), mod3, mod3)


def _mm_kernel(a_ref, b_ref, o_ref):
    o_ref[...] = _dot(a_ref[...], b_ref[...]).astype(o_ref.dtype)


def mm(a, b, out_dtype, tn):
    M, K = a.shape
    N = b.shape[1]
    tm = _pick(M, (1024, 512, 256))
    return pl.pallas_call(
        _mm_kernel,
        out_shape=jax.ShapeDtypeStruct((M, N), out_dtype),
        grid=(M // tm, N // tn),
        in_specs=[pl.BlockSpec((tm, K), lambda i, j: (i, 0)),
                  pl.BlockSpec((K, tn), lambda i, j: (0, j))],
        out_specs=pl.BlockSpec((tm, tn), lambda i, j: (i, j)),
        compiler_params=_cp(2, 40), name="mm",
    )(a, b)


def _prep_kernel(x_ref, g_ref, cos_ref, sin_ref, bd_ref, o_ref):
    cosv = cos_ref[...]
    sinv = sin_ref[...]
    bd = bd_ref[...]
    lane = lax.broadcasted_iota(jnp.int32, cosv.shape, 1)
    even = (lane & 1) == 0
    for s in range((C_V - C_Q) // LANE):
        sl = slice(s * LANE, (s + 1) * LANE)
        x = x_ref[:, sl].astype(F32)
        hi, lo = _split(x * x)
        ss = _dot(hi, bd) + _dot(lo, bd)
        y = x * lax.rsqrt(ss * (1.0 / HD) + EPS) * g_ref[:, sl]
        sw = jnp.where(even, pltpu.roll(y, LANE - 1, 1), pltpu.roll(y, 1, 1))
        o_ref[:, sl] = (y * cosv + sw * sinv).astype(o_ref.dtype)


def qk_prep(proj, gain, cos_t, sin_t, bd):
    B, S, _ = proj.shape
    w = C_V - C_Q
    return pl.pallas_call(
        _prep_kernel,
        out_shape=jax.ShapeDtypeStruct((B, S, w), BF16),
        grid=(B, S // ROW_T),
        in_specs=[pl.BlockSpec((None, ROW_T, w), lambda b, j: (b, j, 0)),
                  pl.BlockSpec((1, w), lambda b, j: (0, 0)),
                  pl.BlockSpec((ROW_T, LANE), lambda b, j: (j, 0)),
                  pl.BlockSpec((ROW_T, LANE), lambda b, j: (j, 0)),
                  pl.BlockSpec((LANE, LANE), lambda b, j: (0, 0))],
        out_specs=pl.BlockSpec((None, ROW_T, w), lambda b, j: (b, j, 0)),
        compiler_params=_cp(2, 24), name="qk_prep",
    )(proj, gain, cos_t, sin_t, bd)


AB = 128


def _attn_kernel(sink_ref, q_ref, kp_ref, kc_ref, kn_ref, kx_ref, vp_ref, vc_ref, vn_ref, vx_ref, o_ref,
                 *, nctb, nblk):
    n = pl.program_id(1)
    is_lat = n >= nctb
    iq = lax.broadcasted_iota(jnp.int32, (AB, AB), 0)
    ik = lax.broadcasted_iota(jnp.int32, (AB, AB), 1)
    ok_p = jnp.logical_and(ik >= iq, jnp.logical_and(is_lat, n - 1 >= nctb))
    ok_c = jnp.logical_and(ik >= 0, is_lat)
    ok_n = jnp.logical_and(ik <= iq, jnp.logical_and(is_lat, n + 1 <= nblk - 1))
    nloc = 3 * AB
    bias = jnp.concatenate([jnp.where(ok_p, 0.0, NEG), jnp.where(ok_c, 0.0, NEG), jnp.where(ok_n, 0.0, NEG)], axis=1)
    kall = jnp.concatenate([kp_ref[...], kc_ref[...], kn_ref[...], kx_ref[...]], axis=0)
    vall = jnp.concatenate([vp_ref[...], vc_ref[...], vn_ref[...], vx_ref[...]], axis=0)
    ones = jnp.ones((kall.shape[0], LANE), BF16)
    lane = lax.broadcasted_iota(jnp.int32, (AB, LANE), 1)
    lo_half = lane < HD
    hi_half = jnp.logical_not(lo_half)
    gq = N_HEADS // N_KV
    for g in range(N_KV):
        kg = kall[:, g * LANE:(g + 1) * LANE]
        vext = jnp.concatenate([vall[:, g * LANE:(g + 1) * LANE], ones], axis=1)
        qparts = []
        for jj in range(2):
            qs = q_ref[:, (2 * g + jj) * LANE:(2 * g + jj + 1) * LANE]
            qparts += [jnp.where(lo_half, qs, jnp.zeros_like(qs)), jnp.where(hi_half, qs, jnp.zeros_like(qs))]
        s = _dot_nt(jnp.concatenate(qparts, axis=0), kg)
        ps, sink_w = [], []
        for seg in range(gq):
            sl = s[seg * AB:(seg + 1) * AB]
            s_loc = sl[:, :nloc] + bias
            s_ctx = sl[:, nloc:]
            sk = sink_ref[gq * g + seg]
            m = jnp.maximum(jnp.maximum(jnp.max(s_loc, axis=-1, keepdims=True),
                                        jnp.max(s_ctx, axis=-1, keepdims=True)), sk)
            ps.append(jnp.concatenate([jnp.exp2(s_loc - m), jnp.exp2(s_ctx - m)], axis=1).astype(BF16))
            sink_w.append(jnp.exp2(sk - m))
        o = _dot(jnp.concatenate(ps, axis=0), vext)
        for jj in range(2):
            outs = []
            for half in range(2):
                seg = 2 * jj + half
                rows = slice(seg * AB, (seg + 1) * AB)
                outs.append(o[rows, :LANE] / (o[rows, LANE:] + sink_w[seg]))
            sl_out = slice((2 * g + jj) * LANE, (2 * g + jj + 1) * LANE)
            o_ref[:, sl_out] = jnp.where(lo_half, outs[0], outs[1]).astype(o_ref.dtype)


def attention(qkn, proj, sink, lc):
    B, S, _ = qkn.shape
    nblk = S // AB
    nctb = lc // AB
    kw = C_V - C_K
    kcol, vcol = C_K // kw, C_V // kw

    def rows(fn, col):
        return pl.BlockSpec((None, AB, kw), lambda b, n: (b, fn(n), col))

    prev = lambda n: jnp.maximum(n - 1, 0)
    cur = lambda n: n
    nxt = lambda n: jnp.minimum(n + 1, nblk - 1)
    ctx = lambda col: pl.BlockSpec((None, lc, kw), lambda b, n: (b, 0, col))
    return pl.pallas_call(
        functools.partial(_attn_kernel, nctb=nctb, nblk=nblk),
        out_shape=jax.ShapeDtypeStruct((B, S, C_K), BF16),
        grid=(B, nblk),
        in_specs=[pl.BlockSpec(memory_space=pltpu.SMEM),
                  pl.BlockSpec((None, AB, C_K), lambda b, n: (b, n, 0)),
                  rows(prev, kcol), rows(cur, kcol), rows(nxt, kcol), ctx(kcol),
                  rows(prev, vcol), rows(cur, vcol), rows(nxt, vcol), ctx(vcol)],
        out_specs=pl.BlockSpec((None, AB, C_K), lambda b, n: (b, n, 0)),
        compiler_params=_cp(2, 32), name="attention",
    )(sink, qkn, qkn, qkn, qkn, qkn, proj, proj, proj, proj)


def _dftc_kernel(u_ref, w_ref, o_ref):
    o_ref[...] = _dot(u_ref[...], w_ref[...]).astype(o_ref.dtype)


def dft_channels(proj, wc, row0, nrows):
    B = proj.shape[0]
    off = row0 // ROW_T
    return pl.pallas_call(
        _dftc_kernel,
        out_shape=jax.ShapeDtypeStruct((B, nrows, 2 * FG * FGD), BF16),
        grid=(B, nrows // ROW_T),
        in_specs=[pl.BlockSpec((None, ROW_T, FG * FGD), lambda b, j: (b, j + off, C_F // (FG * FGD))),
                  pl.BlockSpec((FG * FGD, 2 * FG * FGD), lambda b, j: (0, 0))],
        out_specs=pl.BlockSpec((None, ROW_T, 2 * FG * FGD), lambda b, j: (b, j, 0)),
        compiler_params=_cp(2, 32), name="dft_channels",
    )(proj, wc)


def _dftp_kernel(c_ref, s_ref, y1_ref, y2_ref, o_ref, acc_ref):
    k = pl.program_id(2)

    @pl.when(k == 0)
    def _():
        acc_ref[...] = jnp.zeros_like(acc_ref)

    acc_ref[...] += _dot(c_ref[...], y1_ref[...]) + _dot(s_ref[...], y2_ref[...])

    @pl.when(k == pl.num_programs(2) - 1)
    def _():
        o_ref[...] = acc_ref[...].astype(o_ref.dtype)


def dft_positions(ct, st, y):
    B, n, w2 = y.shape
    w = w2 // 2
    t = _pick(n, (1024, 512, 256))
    return pl.pallas_call(
        _dftp_kernel,
        out_shape=jax.ShapeDtypeStruct((B, n, w), BF16),
        grid=(B, n // t, n // t),
        in_specs=[pl.BlockSpec((t, t), lambda b, i, k: (i, k)),
                  pl.BlockSpec((t, t), lambda b, i, k: (i, k)),
                  pl.BlockSpec((None, t, w), lambda b, i, k: (b, k, 0)),
                  pl.BlockSpec((None, t, w), lambda b, i, k: (b, k, 1))],
        out_specs=pl.BlockSpec((None, t, w), lambda b, i, k: (b, i, 0)),
        scratch_shapes=[pltpu.VMEM((t, w), F32)],
        compiler_params=_cp(3, 40), name="dft_positions",
    )(ct, st, y, y)


def _log_sigmoid(x):
    return jnp.minimum(x, 0.0) - jnp.log(1.0 + jnp.exp(-jnp.abs(x)))


def _gate_kernel(g_ref, b_ref, o_ref):
    L = ML_L
    g = g_ref[...] + b_ref[...]
    lane = lax.broadcasted_iota(jnp.int32, (L, LANE), 1)
    is_f = (lane & ML_H) == ML_H
    is_bwd = (lane & (2 * ML_H)) == 2 * ML_H
    lf = jnp.where(is_f, _log_sigmoid(g), 0.0)
    it_r = lax.broadcasted_iota(jnp.int32, (L, L), 0)
    it_c = lax.broadcasted_iota(jnp.int32, (L, L), 1)
    hp = lax.Precision.HIGHEST
    cf = jnp.dot(jnp.where(it_c <= it_r, 1.0, 0.0).astype(F32), lf, precision=hp, preferred_element_type=F32)
    cb = jnp.dot(jnp.where(it_c >= it_r, 1.0, 0.0).astype(F32), lf, precision=hp, preferred_element_type=F32)
    o_ref[...] = jnp.where(is_f, jnp.where(is_bwd, cb, cf), g)


def gate_prep(graw, bias):
    B, S, _ = graw.shape
    return pl.pallas_call(
        _gate_kernel,
        out_shape=jax.ShapeDtypeStruct((B, S, LANE), F32),
        grid=(B, S // ML_L),
        in_specs=[pl.BlockSpec((None, ML_L, LANE), lambda b, c: (b, c, 0)),
                  pl.BlockSpec((1, LANE), lambda b, c: (0, 0))],
        out_specs=pl.BlockSpec((None, ML_L, LANE), lambda b, c: (b, c, 0)),
        compiler_params=_cp(2, 16), name="gate_prep",
    )(graw, bias)


def _mlstm_kernel(k_ref, qt_ref, vt_ref, mo_ref, g_ref, gt_ref, ng_ref, o_ref,
                  hf_ref, hb_ref, c_ref, m_ref, *, nctc, nchunk):
    L = ML_L
    c_ref[...] = jnp.zeros_like(c_ref)
    m_ref[...] = jnp.zeros_like(m_ref)
    key_i = lax.broadcasted_iota(jnp.int32, (L, L), 0)
    qry_i = lax.broadcasted_iota(jnp.int32, (L, L), 1)
    tri = (key_i <= qry_i, key_i >= qry_i)
    lane = lax.broadcasted_iota(jnp.int32, (L, LANE), 1)
    head_lanes = (lane < ML_DK, lane >= ML_DK)
    ones_t = jnp.ones((ML_DV, L), BF16)
    h_refs = (hf_ref, hb_ref)

    def step(it, carry):
        stores = []
        for d in range(2):
            if d == 0:
                c = it
            else:
                c = jnp.where(it < nctc, nctc - 1 - it, nchunk - 1 - (it - nctc))
            r0 = pl.multiple_of(c * L, L)
            gc = g_ref[pl.ds(r0, L), :]
            gr = gt_ref[c]
            kp = k_ref[pl.ds(r0, L), :]
            qt = qt_ref[c]
            last = L - 1 if d == 0 else 0
            for hh in range(2):
                icol = d * 4 + hh
                fcol = d * 4 + 2 + hh
                sidx = d * 2 + hh
                key_term = gc[:, icol:icol + 1] - gc[:, fcol:fcol + 1]
                b_row = gr[fcol:fcol + 1, :]
                i_row = gr[icol:icol + 1, :]
                m_prev = m_ref[sidx:sidx + 1, 0:1]
                logd = jnp.where(tri[d], key_term + b_row, NEG)
                inter = b_row + m_prev
                m_t = jnp.maximum(inter, jnp.max(logd, axis=0, keepdims=True))
                km = jnp.where(head_lanes[hh], kp, jnp.zeros_like(kp))
                p_t = (_dot(km, qt) * jnp.exp(logd - m_t)).astype(BF16)
                q_in = (qt.astype(F32) * jnp.exp(inter - m_t)).astype(BF16)
                vext_t = jnp.concatenate([vt_ref[c, hh * ML_DV:(hh + 1) * ML_DV, :], ones_t], axis=0)
                c_t = c_ref[sidx]
                tot = _dot(jnp.concatenate([vext_t, c_t.astype(BF16)], axis=1),
                           jnp.concatenate([p_t, q_in], axis=0))
                h_t = tot[:ML_DV] / jnp.maximum(jnp.abs(tot[ML_DV:]), jnp.exp(-m_t))
                b_last = b_row[:, last:last + 1]
                log_w = b_last - b_row + i_row
                m_new = jnp.maximum(b_last + m_prev, jnp.max(log_w, axis=-1, keepdims=True))
                w_row = jnp.exp(log_w - m_new).astype(BF16)
                c_new = jnp.exp(b_last + m_prev - m_new) * c_t + _dot(vext_t * w_row, km)
                stores.append((d, hh, sidx, c, h_t, c_new, m_new))
        for d, hh, sidx, c, h_t, c_new, m_new in stores:
            h_refs[d][c, hh] = h_t
            c_ref[sidx] = c_new
            m_ref[sidx:sidx + 1, :] = jnp.broadcast_to(m_new, (1, LANE))
        return carry

    lax.fori_loop(0, nchunk, step, 0)

    def out_step(i, carry):
        r0 = pl.multiple_of(i * L, L)
        for hh in range(2):
            hs = slice(hh * ML_DV, (hh + 1) * ML_DV)
            h = jnp.transpose(hf_ref[i, hh] + hb_ref[i, hh])
            hn = h * lax.rsqrt(jnp.mean(h * h, axis=-1, keepdims=True) + EPS) * ng_ref[:, hs]
            og = jax.nn.sigmoid(mo_ref[pl.ds(r0, L), hs].astype(F32))
            o_ref[pl.ds(r0, L), hs] = (og * hn).astype(o_ref.dtype)
        return carry

    lax.fori_loop(0, nchunk, out_step, 0)


def mlstm(proj, q_t, v_t, gates, gates_t, norm_g, lc):
    B, S, _ = proj.shape
    nchunk = S // ML_L
    pw = 2 * ML_DK
    vw = 2 * ML_DV
    chunked = lambda rows: pl.BlockSpec((None, None, nchunk, rows, ML_L), lambda b, p: (b, p, 0, 0, 0))
    return pl.pallas_call(
        functools.partial(_mlstm_kernel, nctc=lc // ML_L, nchunk=nchunk),
        out_shape=jax.ShapeDtypeStruct((B, S, ML_H * ML_DV), BF16),
        grid=(B, ML_H // 2),
        in_specs=[pl.BlockSpec((None, S, pw), lambda b, p: (b, 0, C_MK // pw + p)),
                  chunked(pw), chunked(vw),
                  pl.BlockSpec((None, S, vw), lambda b, p: (b, 0, C_MO // vw + p)),
                  pl.BlockSpec((None, None, S, LANE), lambda b, p: (b, p, 0, 0)),
                  chunked(8),
                  pl.BlockSpec((1, vw), lambda b, p: (0, p))],
        out_specs=pl.BlockSpec((None, S, vw), lambda b, p: (b, 0, p)),
        scratch_shapes=[pltpu.VMEM((nchunk, 2, ML_DV, ML_L), F32), pltpu.VMEM((nchunk, 2, ML_DV, ML_L), F32),
                        pltpu.VMEM((4, 2 * ML_DV, 2 * ML_DK), F32), pltpu.VMEM((8, LANE), F32)],
        compiler_params=_cp(2, 48), name="mlstm",
    )(proj, q_t, v_t, proj, gates, gates_t, norm_g)


def _merge_kernel(a_ref, f_ref, m_ref, wa_ref, wf_ref, wm_ref, ga_ref, gf_ref, gm_ref,
                  ba_ref, bf_ref, bm_ref, o_ref):
    def br(x_ref, w_ref, g_ref, b_ref):
        return jax.nn.sigmoid(g_ref[...].astype(F32) + b_ref[...]) * _dot(x_ref[...], w_ref[...])

    y = br(a_ref, wa_ref, ga_ref, ba_ref) + br(f_ref, wf_ref, gf_ref, bf_ref) + br(m_ref, wm_ref, gm_ref, bm_ref)
    o_ref[...] = y.astype(o_ref.dtype)


def merge(a, f, m, proj, wa, wf, wm, b_gate):
    R, kin = a.shape
    tm = _pick(R, (512, 256))
    tn = 1024
    x_spec = pl.BlockSpec((tm, kin), lambda j, i: (i, 0))
    w_spec = pl.BlockSpec((kin, tn), lambda j, i: (0, j))
    gp = lambda br: pl.BlockSpec((tm, tn), lambda j, i: (i, (C_GP + br * D) // tn + j))
    bg = lambda br: pl.BlockSpec((None, 1, tn), lambda j, i: (br, 0, j))
    return pl.pallas_call(
        _merge_kernel,
        out_shape=jax.ShapeDtypeStruct((R, D), BF16),
        grid=(D // tn, R // tm),
        in_specs=[x_spec, x_spec, x_spec, w_spec, w_spec, w_spec, gp(0), gp(1), gp(2), bg(0), bg(1), bg(2)],
        out_specs=pl.BlockSpec((tm, tn), lambda j, i: (i, j)),
        compiler_params=_cp(2, 40), name="merge",
    )(a, f, m, wa, wf, wm, proj, proj, proj, b_gate, b_gate, b_gate)


def _outproj_kernel(y_ref, w_ref, x_ref, g_ref, o_ref):
    o_ref[...] = x_ref[...] + g_ref[...] * _dot(y_ref[...], w_ref[...])


def out_proj_residual(y, w_out, xa, mod3, k_gate, nct):
    B, S, _ = xa.shape
    return pl.pallas_call(
        _outproj_kernel,
        out_shape=jax.ShapeDtypeStruct((B, S, D), F32),
        grid=(B, S // ROW_T),
        in_specs=[pl.BlockSpec((None, ROW_T, D), lambda b, j: (b, j, 0)),
                  pl.BlockSpec((D, D), lambda b, j: (0, 0)),
                  pl.BlockSpec((None, ROW_T, D), lambda b, j: (b, j, 0)),
                  _mod_spec(k_gate, nct, B)],
        out_specs=pl.BlockSpec((None, ROW_T, D), lambda b, j: (b, j, 0)),
        compiler_params=_cp(2, 40), name="out_proj",
    )(y, w_out, xa, mod3)


def _pack2(lo, hi):
    lo_b = lax.bitcast_convert_type(lo.astype(BF16).astype(F32), jnp.uint32)
    hi_b = lax.bitcast_convert_type(hi.astype(BF16).astype(F32), jnp.uint32)
    return (lo_b >> 16) | (hi_b & jnp.uint32(0xFFFF0000))


def _unpack2(w):
    lo = lax.bitcast_convert_type(w << 16, F32)
    hi = lax.bitcast_convert_type(w & jnp.uint32(0xFFFF0000), F32)
    return lo, hi


def _router_kernel(x_ref, g_ref, sc_ref, sh_ref, wh_ref, wl_ref, b_ref, hp_ref, eid_ref, wt_ref):
    x = x_ref[...]
    y = x * lax.rsqrt(jnp.mean(x * x, axis=-1, keepdims=True) + EPS) * g_ref[...]
    y = y * (1.0 + sc_ref[...]) + sh_ref[...]
    hp_ref[...] = _pack2(y[:, :D // 2], y[:, D // 2:])
    yh, yl = _split(y)
    logits = _dot(yh, wh_ref[...]) + _dot(yl, wh_ref[...]) + _dot(yh, wl_ref[...]) + b_ref[...]
    lane = lax.broadcasted_iota(jnp.int32, logits.shape, 1)
    lanef = lane.astype(F32)
    big = float(LANE)
    glog = jnp.where(lane < N_GRP, logits, NEG)
    gmax = jnp.max(glog, axis=-1, keepdims=True)
    gsel = jnp.min(jnp.where(glog == gmax, lanef, big), axis=-1, keepdims=True)
    pg = 1.0 / jnp.sum(jnp.exp(glog - gmax), axis=-1, keepdims=True)
    lo = N_GRP + EPG * gsel
    el = jnp.where(jnp.logical_and(lanef >= lo, lanef < lo + EPG), logits, NEG)
    v1 = jnp.max(el, axis=-1, keepdims=True)
    i1 = jnp.min(jnp.where(el == v1, lanef, big), axis=-1, keepdims=True)
    el2 = jnp.where(lanef == i1, NEG, el)
    v2 = jnp.max(el2, axis=-1, keepdims=True)
    i2 = jnp.min(jnp.where(el2 == v2, lanef, big), axis=-1, keepdims=True)
    e = jnp.exp(v2 - v1)
    w1 = pg / (1.0 + e)
    w2 = pg * e / (1.0 + e)
    eid_ref[...] = jnp.where(lane == 0, i1 - N_GRP, jnp.where(lane == 1, i2 - N_GRP, 0.0)).astype(jnp.int32)
    wt_ref[...] = jnp.where(lane == 0, w1, jnp.where(lane == 1, w2, 0.0))


def norm_router(xa, g, mod3, k_sh, k_sc, nct, wr_hi, wr_lo, br):
    B, S, _ = xa.shape
    tile = lambda w: pl.BlockSpec((None, ROW_T, w), lambda b, j: (b, j, 0))
    const = lambda shape: pl.BlockSpec(shape, lambda b, j: (0, 0))
    return pl.pallas_call(
        _router_kernel,
        out_shape=(jax.ShapeDtypeStruct((B, S, D // 2), jnp.uint32),
                   jax.ShapeDtypeStruct((B, S, LANE), jnp.int32),
                   jax.ShapeDtypeStruct((B, S, LANE), F32)),
        grid=(B, S // ROW_T),
        in_specs=[tile(D), const((1, D)), _mod_spec(k_sc, nct, B), _mod_spec(k_sh, nct, B),
                  const((D, LANE)), const((D, LANE)), const((1, LANE))],
        out_specs=(tile(D // 2), tile(LANE), tile(LANE)),
        compiler_params=_cp(2, 32), name="norm_router",
    )(xa, g.reshape(1, D), mod3, mod3, wr_hi, wr_lo, br)


DMA_UNROLL = 8


def _row_copy(src_hbm, srow, dst_hbm, drow, sem):
    return pltpu.make_async_copy(src_hbm.at[pl.ds(srow, 1), :], dst_hbm.at[pl.ds(drow, 1), :], sem)


def _expert_kernel(tok_ref, order_ref, rstart_ref, nvalid_ref, be_ref, nexte_ref, nused_ref,
                   x_hbm, w1_hbm, w3_hbm, w2_hbm, y_hbm,
                   xbuf, ybuf, st1, st3, st2, wb1, wb3, wb2, gsem, ssem, wsem, *, layer):
    i = pl.program_id(0)
    nused = nused_ref[0]
    last = pl.num_programs(0) - 1
    used = i < nused
    weights = ((w1_hbm, st1, wb1), (w3_hbm, st3, wb3), (w2_hbm, st2, wb2))

    def gather_rows(blk):
        slot = blk & 1
        rs = rstart_ref[blk]

        def group(k, carry):
            for u in range(DMA_UNROLL):
                r = k * DMA_UNROLL + u
                _row_copy(x_hbm, tok_ref[rs + r], xbuf.at[slot], r, gsem.at[slot]).start(priority=u % 2)
            return carry

        lax.fori_loop(0, MOE_T // DMA_UNROLL, group, 0)

    def scatter_rows(blk, wait):
        slot = blk & 1
        rs = rstart_ref[blk]
        nv = nvalid_ref[blk]

        def one(r, prio):
            if wait:
                _row_copy(ybuf.at[slot], 0, y_hbm, 0, ssem.at[slot]).wait()
            else:
                _row_copy(ybuf.at[slot], r, y_hbm, order_ref[rs + r], ssem.at[slot]).start(priority=prio)

        def group(k, carry):
            if wait:
                pltpu.make_async_copy(ybuf.at[slot].at[pl.ds(0, DMA_UNROLL), :], y_hbm.at[pl.ds(0, DMA_UNROLL), :],
                                      ssem.at[slot]).wait()
            else:
                for u in range(DMA_UNROLL):
                    one(k * DMA_UNROLL + u, u % 2)
            return carry

        def tail(r, carry):
            one(r, 0)
            return carry

        full = nv // DMA_UNROLL
        lax.fori_loop(0, full, group, 0)
        lax.fori_loop(full * DMA_UNROLL, nv, tail, 0)

    def fetch_weights(e):
        for k, (hbm, st, _) in enumerate(weights):
            pltpu.make_async_copy(hbm.at[layer, e], st, wsem.at[k]).start()

    @pl.when(i == 0)
    def _():
        fetch_weights(be_ref[0])
        gather_rows(0)

    @pl.when(i + 1 < nused)
    def _():
        gather_rows(i + 1)

    @pl.when(jnp.logical_and(i >= 2, i - 2 < nused))
    def _():
        scatter_rows(i - 2, True)

    @pl.when(used)
    def _():
        e = be_ref[i]

        @pl.when(jnp.logical_or(i == 0, be_ref[jnp.maximum(i - 1, 0)] != e))
        def _():
            for k, (hbm, st, wb) in enumerate(weights):
                pltpu.make_async_copy(hbm.at[layer, 0], st, wsem.at[k]).wait()
                wb[...] = st[...].astype(BF16)

            @pl.when(nexte_ref[i] >= 0)
            def _():
                fetch_weights(nexte_ref[i])

        slot = i & 1
        pltpu.make_async_copy(x_hbm.at[pl.ds(0, MOE_T), :], xbuf.at[slot], gsem.at[slot]).wait()
        xl, xh = _unpack2(xbuf[slot])
        xl = xl.astype(BF16)
        xh = xh.astype(BF16)
        half = D // 2
        a = _dot(xl, wb1[:half, :]) + _dot(xh, wb1[half:, :])
        b = _dot(xl, wb3[:half, :]) + _dot(xh, wb3[half:, :])
        hmid = (a * jax.nn.sigmoid(a) * b).astype(BF16)
        y = _dot(hmid, wb2[...])
        ybuf[slot] = _pack2(y[:, :half], y[:, half:])
        scatter_rows(i, False)

    @pl.when(i == last)
    def _():
        @pl.when(jnp.logical_and(i >= 1, i - 1 < nused))
        def _():
            scatter_rows(i - 1, True)

        @pl.when(used)
        def _():
            scatter_rows(i, True)


def expert_ffn(hp, plan, w1, w3, w2, layer, n_blocks, n_pairs):
    w = hp.shape[1]
    hbm = pl.BlockSpec(memory_space=pl.ANY)
    grid_spec = pltpu.PrefetchScalarGridSpec(
        num_scalar_prefetch=7, grid=(n_blocks,),
        in_specs=[hbm, hbm, hbm, hbm],
        out_specs=hbm,
        scratch_shapes=[pltpu.VMEM((2, MOE_T, w), jnp.uint32), pltpu.VMEM((2, MOE_T, w), jnp.uint32),
                        pltpu.VMEM((D, FF), F32), pltpu.VMEM((D, FF), F32), pltpu.VMEM((FF, D), F32),
                        pltpu.VMEM((D, FF), BF16), pltpu.VMEM((D, FF), BF16), pltpu.VMEM((FF, D), BF16),
                        pltpu.SemaphoreType.DMA((2,)), pltpu.SemaphoreType.DMA((2,)),
                        pltpu.SemaphoreType.DMA((3,))])
    return pl.pallas_call(
        functools.partial(_expert_kernel, layer=layer),
        out_shape=jax.ShapeDtypeStruct((n_pairs, w), jnp.uint32),
        grid_spec=grid_spec, compiler_params=_cp(1, 56), name="moe_experts",
    )(*plan, hp, w1, w3, w2)


def _combine_kernel(ya_ref, yb_ref, x_ref, wt_ref, g_ref, o_ref):
    w0 = wt_ref[:, 0:1]
    w1 = wt_ref[:, 1:2]
    al, ah = _unpack2(ya_ref[...])
    bl, bh = _unpack2(yb_ref[...])
    half = D // 2
    g = g_ref[...]
    o_ref[:, :half] = x_ref[:, :half] + g[:, :half] * (w0 * al + w1 * bl)
    o_ref[:, half:] = x_ref[:, half:] + g[:, half:] * (w0 * ah + w1 * bh)


def combine(y2, xa, wts, mod3, k_gate, nct, latent_only):
    B, S, _ = xa.shape
    w = y2.shape[-1]
    off = nct if latent_only else 0
    tile = lambda wd: pl.BlockSpec((None, ROW_T, wd), lambda b, j: (b, j + off, 0))
    pick = lambda k: pl.BlockSpec((None, None, ROW_T, w), lambda b, j: (k, b, j + off, 0))
    return pl.pallas_call(
        _combine_kernel,
        out_shape=jax.ShapeDtypeStruct((B, S - off * ROW_T, D), F32),
        grid=(B, S // ROW_T - off),
        in_specs=[pick(0), pick(1), tile(D), tile(LANE), _mod_spec(k_gate, nct, B, off)],
        out_specs=pl.BlockSpec((None, ROW_T, D), lambda b, j: (b, j, 0)),
        compiler_params=_cp(2, 32), name="moe_combine",
    )(y2, y2, xa, wts, mod3)


def moe_plan(eid, n_blocks, n_tok):
    order = jnp.argsort(eid).astype(jnp.int32)
    tok = jnp.pad(jnp.where(order >= n_tok, order - n_tok, order), (0, MOE_T))
    experts = jnp.arange(N_EXP, dtype=jnp.int32)
    counts = jnp.sum((eid[:, None] == experts[None, :]).astype(jnp.int32), axis=0)
    nblk_e = (counts + MOE_T - 1) // MOE_T
    bend = jnp.cumsum(nblk_e)
    bstart = bend - nblk_e
    start = jnp.cumsum(counts) - counts
    blk = jnp.arange(n_blocks, dtype=jnp.int32)
    owner = jnp.logical_and(blk[:, None] >= bstart[None, :], blk[:, None] < bend[None, :]).astype(jnp.int32)
    take = lambda v: jnp.sum(owner * v[None, :], axis=1)
    within = blk - take(bstart)
    nused = bend[-1]
    last_e = jnp.max(jnp.where(counts > 0, experts, 0))
    block_e = jnp.where(blk < nused, take(experts), last_e).astype(jnp.int32)
    rstart = jnp.where(blk < nused, take(start) + within * MOE_T, 0).astype(jnp.int32)
    nvalid = jnp.where(blk < nused, jnp.clip(take(counts) - within * MOE_T, 0, MOE_T), 0).astype(jnp.int32)
    nxt = take(bend)
    e_at_nxt = jnp.sum((blk[None, :] == nxt[:, None]).astype(jnp.int32) * block_e[None, :], axis=1)
    next_e = jnp.where(jnp.logical_and(blk < nused, nxt < nused), e_at_nxt, -1).astype(jnp.int32)
    return tok, order, rstart, nvalid, block_e, next_e, nused.astype(jnp.int32).reshape(1)


def _rope_tables(s, lc):
    t = np.arange(s - lc)
    row = (t // GRID_W).astype(np.float64)
    col = (t % GRID_W).astype(np.float64)
    nf = HD // 4
    inv = np.float32(ROPE_BASE) ** (-np.arange(nf, dtype=np.float32) / np.float32(nf))
    ang = np.concatenate([row[:, None] * inv, col[:, None] * inv], axis=-1)
    ang = np.concatenate([np.zeros((lc, HD // 2)), ang], axis=0)
    cos = np.repeat(np.cos(ang), 2, axis=-1)
    sin = np.repeat(np.sin(ang), 2, axis=-1) * np.tile([-1.0, 1.0], HD // 2)
    return jnp.asarray(np.tile(cos, (1, LANE // HD)), F32), jnp.asarray(np.tile(sin, (1, LANE // HD)), F32)


def _chan_dft():
    j = np.arange(FGD)
    ang = 2.0 * np.pi * ((j[:, None] * j[None, :]) % FGD) / FGD
    c = np.kron(np.eye(FG), np.cos(ang)) / np.sqrt(FGD)
    s = np.kron(np.eye(FG), np.sin(ang)) / np.sqrt(FGD)
    return jnp.asarray(np.concatenate([c, -s], axis=1), BF16)


def _pos_dft(n):
    if n <= 512:
        k = np.arange(n)
        ang = 2.0 * np.pi * ((k[:, None] * k[None, :]) % n) / n
        return (jnp.asarray(np.cos(ang) / np.sqrt(n), BF16), jnp.asarray(np.sin(ang) / np.sqrt(n), BF16))
    nr = n // LANE
    k = np.arange(n)
    a = 2.0 * np.pi * ((k[:, None] * np.arange(nr)[None, :]) % nr) / nr
    b = 2.0 * np.pi * (k[:, None] * np.arange(LANE)[None, :]) / n
    sc = 1.0 / np.sqrt(n)
    ca, sa = jnp.asarray(np.cos(a) * sc, F32)[:, :, None], jnp.asarray(np.sin(a) * sc, F32)[:, :, None]
    cb, sb = jnp.asarray(np.cos(b), F32)[:, None, :], jnp.asarray(np.sin(b), F32)[:, None, :]
    ct = (ca * cb - sa * sb).reshape(n, n).astype(BF16)
    st = (sa * cb + ca * sb).reshape(n, n).astype(BF16)
    return ct, st


def _proj_weights(w_in):
    dup = lambda w: jnp.concatenate([w.reshape(D, N_KV, HD)] * 2, axis=-1).reshape(D, 2 * N_KV * HD)
    w_big = jnp.concatenate([w_in[:, 0:1024], dup(w_in[:, 1024:1280]), dup(w_in[:, 1280:1536]),
                             w_in[:, 1536:4608], w_in[:, 4640:]], axis=1).astype(BF16)
    w_gate = jnp.pad(w_in[:, 4608:4640], ((0, 0), (0, LANE - 32))).astype(BF16)
    return w_big, w_gate


def _qk_gain(qg, kg):
    qs = jnp.tile(qg * (HD ** -0.5 * LOG2E), N_HEADS)
    return jnp.concatenate([qs, jnp.tile(kg, 2 * N_KV)]).reshape(1, -1).astype(F32)


def kernel(x, c, ctx, c_ctx, w_mod, b_mod, norm1_g, norm2_g, w_in, q_norm_g, k_norm_g, attn_sink, ml_gate_b,
           ml_norm_g, w_br_attn, w_br_four, w_br_mlstm, b_gate, w_out, w_grp, b_grp, w_exp_router,
           b_exp_router, w1, w3, w2):
    B, T, _ = x.shape
    lc = ctx.shape[1]
    S = lc + T
    R = B * S
    depth = w_mod.shape[0]
    nct = lc // ROW_T
    assert lc % ROW_T == 0 and T % ROW_T == 0 and B < 8

    xa = jnp.concatenate([ctx, x], axis=1)
    cvec = jnp.concatenate([c, c_ctx[None], jnp.zeros((7 - B, D), F32)], axis=0)
    cos_t, sin_t = _rope_tables(S, lc)
    bd = jnp.asarray(np.kron(np.eye(LANE // HD), np.ones((HD, HD))), BF16)
    wc = _chan_dft()
    ct_lat, st_lat = _pos_dft(T)
    ct_ctx, st_ctx = _pos_dft(lc)
    nchunk = S // ML_L
    n_pairs = R * 2
    n_blocks = -(-n_pairs // MOE_T) + N_EXP

    for l in range(depth):
        mod = modulation(cvec, w_mod, b_mod, l)
        mod3 = mod.reshape(8, 1, 6 * D)
        h = norm_mod(xa, norm1_g[l], mod3, 0, 1, nct)
        w_big, w_gate = _proj_weights(w_in[l])
        h2d = h.reshape(R, D)
        proj = mm(h2d, w_big, BF16, 1024).reshape(B, S, C_TOT)
        graw = mm(h2d, w_gate, F32, LANE).reshape(B, S, LANE)

        qkn = qk_prep(proj, _qk_gain(q_norm_g[l], k_norm_g[l]), cos_t, sin_t, bd)
        a = attention(qkn, proj, attn_sink[l].astype(F32) * LOG2E, lc)

        f_ctx = dft_positions(ct_ctx, st_ctx, dft_channels(proj, wc, 0, lc))
        f_lat = dft_positions(ct_lat, st_lat, dft_channels(proj, wc, lc, T))
        f = jnp.concatenate([f_ctx, f_lat], axis=1)

        gbias = jnp.pad(ml_gate_b[l].reshape(1, 4 * ML_H).astype(F32), ((0, 0), (0, LANE - 4 * ML_H)))
        gproc = gate_prep(graw, gbias)[..., :4 * ML_H]
        g5 = gproc.reshape(B, S, 2, 2, ML_H // 2, 2).transpose(0, 4, 1, 2, 3, 5).reshape(B, ML_H // 2, S, 8)
        gates = jnp.pad(g5, ((0, 0), (0, 0), (0, 0), (0, LANE - 8)))
        gates_t = g5.reshape(B, ML_H // 2, nchunk, ML_L, 8).transpose(0, 1, 2, 4, 3)
        chunk_t = lambda a, w: a.reshape(B, nchunk, ML_L, ML_H // 2, w).transpose(0, 3, 1, 4, 2)
        q_t = chunk_t(proj[..., C_MQ:C_MK] * jnp.asarray(ML_DK ** -0.5, BF16), 2 * ML_DK)
        v_t = chunk_t(proj[..., C_MV:C_MO], 2 * ML_DV)
        m = mlstm(proj, q_t, v_t, gates, gates_t, ml_norm_g[l].reshape(1, -1).astype(F32), lc)

        y = merge(a.reshape(R, -1), f.reshape(R, -1), m.reshape(R, -1), proj.reshape(R, C_TOT),
                  w_br_attn[l].astype(BF16), w_br_four[l].astype(BF16), w_br_mlstm[l].astype(BF16),
                  b_gate[l].reshape(3, 1, D).astype(F32))
        xa = out_proj_residual(y.reshape(B, S, D), w_out[l].astype(BF16), xa, mod3, 2, nct)

        wr = jnp.concatenate([w_grp[l], w_exp_router[l].reshape(D, N_EXP)], axis=1)
        wr = jnp.pad(wr, ((0, 0), (0, LANE - N_GRP - N_EXP))).astype(F32)
        wr_hi = wr.astype(BF16)
        wr_lo = (wr - wr_hi.astype(F32)).astype(BF16)
        br = jnp.pad(jnp.concatenate([b_grp[l], b_exp_router[l].reshape(N_EXP)]),
                     (0, LANE - N_GRP - N_EXP)).reshape(1, LANE).astype(F32)
        hp, eid, wts = norm_router(xa, norm2_g[l], mod3, 3, 4, nct, wr_hi, wr_lo, br)
        plan = moe_plan(jnp.concatenate([eid[..., 0].reshape(R), eid[..., 1].reshape(R)]), n_blocks, R)
        y2 = expert_ffn(hp.reshape(R, D // 2), plan, w1, w3, w2, l, n_blocks, n_pairs)
        xa = combine(y2.reshape(2, B, S, D // 2), xa, wts, mod3, 5, nct, l == depth - 1)

    return xa
```

```python
import functools

import numpy as np
import jax
import jax.numpy as jnp
from jax import lax
from jax.experimental import pallas as pl
from jax.experimental.pallas import tpu as pltpu

F32 = jnp.float32
BF16 = jnp.bfloat16

D = 2048
N_HEADS, N_KV, HD = 16, 4, 64
GRID_W = 64
ROPE_BASE = 10000.0
EPS = 1e-6
ML_H, ML_DK, ML_DV = 8, 64, 128
ML_L = 256
FG, FGD = 4, 256
N_GRP, EPG, N_EXP, FF = 4, 8, 32, 1024
NEG = -1e30
LOG2E = 1.4426950408889634
LANE = 128
ROW_T = 256
MOE_T = 256

C_Q, C_K, C_V = 0, 1024, 1536
C_MQ, C_MK, C_MV, C_MO = 2048, 2560, 3072, 4096
C_F, C_GP, C_TOT = 5120, 6144, 12288


def _cp(n_axes, vmem_mb):
    return pltpu.CompilerParams(dimension_semantics=("arbitrary",) * n_axes,
                                vmem_limit_bytes=int(vmem_mb) * 2 ** 20)


def _pick(n, cands):
    for c in cands:
        if n % c == 0:
            return c
    raise ValueError(f"no tile for {n} in {cands}")


def _dot(a, b):
    return jnp.dot(a, b, preferred_element_type=F32)


def _dot_nt(a, b):
    return lax.dot_general(a, b, (((1,), (1,)), ((), ())), preferred_element_type=F32)


def _split(x):
    hi = x.astype(BF16)
    return hi, (x - hi.astype(F32)).astype(BF16)


def _dot3(a, w):
    ah, al = _split(a)
    wh, wl = _split(w)
    return _dot(ah, wh) + _dot(al, wh) + _dot(ah, wl)


def _mod_kernel(a_ref, w_ref, b_ref, o_ref):
    a = a_ref[...]
    a = a * jax.nn.sigmoid(a)
    o_ref[...] = _dot3(a, w_ref[...]) + b_ref[...]


def modulation(cvec, w_mod, b_mod, layer):
    depth, _, n = w_mod.shape
    tn = 512
    return pl.pallas_call(
        _mod_kernel,
        out_shape=jax.ShapeDtypeStruct((8, n), F32),
        grid=(n // tn,),
        in_specs=[pl.BlockSpec((8, D), lambda j: (0, 0)),
                  pl.BlockSpec((None, D, tn), lambda j: (layer, 0, j)),
                  pl.BlockSpec((None, 1, tn), lambda j: (layer, 0, j))],
        out_specs=pl.BlockSpec((8, tn), lambda j: (0, j)),
        compiler_params=_cp(1, 24), name="modulation",
    )(cvec, w_mod, b_mod.reshape(depth, 1, n))


def _mod_spec(k, nct, nb, off=0):
    return pl.BlockSpec((None, 1, D), lambda b, j: (jnp.where(j + off < nct, nb, b), 0, k))


def _norm_kernel(x_ref, g_ref, sc_ref, sh_ref, o_ref):
    x = x_ref[...]
    y = x * lax.rsqrt(jnp.mean(x * x, axis=-1, keepdims=True) + EPS) * g_ref[...]
    o_ref[...] = (y * (1.0 + sc_ref[...]) + sh_ref[...]).astype(o_ref.dtype)


def norm_mod(xa, g, mod3, k_sh, k_sc, nct):
    B, S, _ = xa.shape
    return pl.pallas_call(
        _norm_kernel,
        out_shape=jax.ShapeDtypeStruct((B, S, D), BF16),
        grid=(B, S // ROW_T),
        in_specs=[pl.BlockSpec((None, ROW_T, D), lambda b, j: (b, j, 0)),
                  pl.BlockSpec((1, D), lambda b, j: (0, 0)),
                  _mod_spec(k_sc, nct, B), _mod_spec(k_sh, nct, B)],
        out_specs=pl.BlockSpec((None, ROW_T, D), lambda b, j: (b, j, 0)),
        compiler_params=_cp(2, 24), name="norm_mod",
    )(xa, g.reshape(1, D), mod3, mod3)


def _mm_kernel(a_ref, b_ref, o_ref):
    o_ref[...] = _dot(a_ref[...], b_ref[...]).astype(o_ref.dtype)


def mm(a, b, out_dtype, tn):
    M, K = a.shape
    N = b.shape[1]
    tm = _pick(M, (1024, 512, 256))
    return pl.pallas_call(
        _mm_kernel,
        out_shape=jax.ShapeDtypeStruct((M, N), out_dtype),
        grid=(M // tm, N // tn),
        in_specs=[pl.BlockSpec((tm, K), lambda i, j: (i, 0)),
                  pl.BlockSpec((K, tn), lambda i, j: (0, j))],
        out_specs=pl.BlockSpec((tm, tn), lambda i, j: (i, j)),
        compiler_params=_cp(2, 40), name="mm",
    )(a, b)


def _prep_kernel(x_ref, g_ref, cos_ref, sin_ref, bd_ref, o_ref):
    cosv = cos_ref[...]
    sinv = sin_ref[...]
    bd = bd_ref[...]
    lane = lax.broadcasted_iota(jnp.int32, cosv.shape, 1)
    even = (lane & 1) == 0
    for s in range((C_V - C_Q) // LANE):
        sl = slice(s * LANE, (s + 1) * LANE)
        x = x_ref[:, sl].astype(F32)
        hi, lo = _split(x * x)
        ss = _dot(hi, bd) + _dot(lo, bd)
        y = x * lax.rsqrt(ss * (1.0 / HD) + EPS) * g_ref[:, sl]
        sw = jnp.where(even, pltpu.roll(y, LANE - 1, 1), pltpu.roll(y, 1, 1))
        o_ref[:, sl] = (y * cosv + sw * sinv).astype(o_ref.dtype)


def qk_prep(proj, gain, cos_t, sin_t, bd):
    B, S, _ = proj.shape
    w = C_V - C_Q
    return pl.pallas_call(
        _prep_kernel,
        out_shape=jax.ShapeDtypeStruct((B, S, w), BF16),
        grid=(B, S // ROW_T),
        in_specs=[pl.BlockSpec((None, ROW_T, w), lambda b, j: (b, j, 0)),
                  pl.BlockSpec((1, w), lambda b, j: (0, 0)),
                  pl.BlockSpec((ROW_T, LANE), lambda b, j: (j, 0)),
                  pl.BlockSpec((ROW_T, LANE), lambda b, j: (j, 0)),
                  pl.BlockSpec((LANE, LANE), lambda b, j: (0, 0))],
        out_specs=pl.BlockSpec((None, ROW_T, w), lambda b, j: (b, j, 0)),
        compiler_params=_cp(2, 24), name="qk_prep",
    )(proj, gain, cos_t, sin_t, bd)


AB = 128


def _attn_kernel(sink_ref, q_ref, kp_ref, kc_ref, kn_ref, kx_ref, vp_ref, vc_ref, vn_ref, vx_ref, o_ref,
                 *, nctb, nblk):
    n = pl.program_id(1)
    is_lat = n >= nctb
    iq = lax.broadcasted_iota(jnp.int32, (AB, AB), 0)
    ik = lax.broadcasted_iota(jnp.int32, (AB, AB), 1)
    ok_p = jnp.logical_and(ik >= iq, jnp.logical_and(is_lat, n - 1 >= nctb))
    ok_c = jnp.logical_and(ik >= 0, is_lat)
    ok_n = jnp.logical_and(ik <= iq, jnp.logical_and(is_lat, n + 1 <= nblk - 1))
    nloc = 3 * AB
    bias = jnp.concatenate([jnp.where(ok_p, 0.0, NEG), jnp.where(ok_c, 0.0, NEG), jnp.where(ok_n, 0.0, NEG)], axis=1)
    kall = jnp.concatenate([kp_ref[...], kc_ref[...], kn_ref[...], kx_ref[...]], axis=0)
    vall = jnp.concatenate([vp_ref[...], vc_ref[...], vn_ref[...], vx_ref[...]], axis=0)
    ones = jnp.ones((kall.shape[0], LANE), BF16)
    lane = lax.broadcasted_iota(jnp.int32, (AB, LANE), 1)
    lo_half = lane < HD
    hi_half = jnp.logical_not(lo_half)
    gq = N_HEADS // N_KV
    for g in range(N_KV):
        kg = kall[:, g * LANE:(g + 1) * LANE]
        vext = jnp.concatenate([vall[:, g * LANE:(g + 1) * LANE], ones], axis=1)
        qparts = []
        for jj in range(2):
            qs = q_ref[:, (2 * g + jj) * LANE:(2 * g + jj + 1) * LANE]
            qparts += [jnp.where(lo_half, qs, jnp.zeros_like(qs)), jnp.where(hi_half, qs, jnp.zeros_like(qs))]
        s = _dot_nt(jnp.concatenate(qparts, axis=0), kg)
        ps, sink_w = [], []
        for seg in range(gq):
            sl = s[seg * AB:(seg + 1) * AB]
            s_loc = sl[:, :nloc] + bias
            s_ctx = sl[:, nloc:]
            sk = sink_ref[gq * g + seg]
            m = jnp.maximum(jnp.maximum(jnp.max(s_loc, axis=-1, keepdims=True),
                                        jnp.max(s_ctx, axis=-1, keepdims=True)), sk)
            ps.append(jnp.concatenate([jnp.exp2(s_loc - m), jnp.exp2(s_ctx - m)], axis=1).astype(BF16))
            sink_w.append(jnp.exp2(sk - m))
        o = _dot(jnp.concatenate(ps, axis=0), vext)
        for jj in range(2):
            outs = []
            for half in range(2):
                seg = 2 * jj + half
                rows = slice(seg * AB, (seg + 1) * AB)
                outs.append(o[rows, :LANE] / (o[rows, LANE:] + sink_w[seg]))
            sl_out = slice((2 * g + jj) * LANE, (2 * g + jj + 1) * LANE)
            o_ref[:, sl_out] = jnp.where(lo_half, outs[0], outs[1]).astype(o_ref.dtype)


def attention(qkn, proj, sink, lc):
    B, S, _ = qkn.shape
    nblk = S // AB
    nctb = lc // AB
    kw = C_V - C_K
    kcol, vcol = C_K // kw, C_V // kw

    def rows(fn, col):
        return pl.BlockSpec((None, AB, kw), lambda b, n: (b, fn(n), col))

    prev = lambda n: jnp.maximum(n - 1, 0)
    cur = lambda n: n
    nxt = lambda n: jnp.minimum(n + 1, nblk - 1)
    ctx = lambda col: pl.BlockSpec((None, lc, kw), lambda b, n: (b, 0, col))
    return pl.pallas_call(
        functools.partial(_attn_kernel, nctb=nctb, nblk=nblk),
        out_shape=jax.ShapeDtypeStruct((B, S, C_K), BF16),
        grid=(B, nblk),
        in_specs=[pl.BlockSpec(memory_space=pltpu.SMEM),
                  pl.BlockSpec((None, AB, C_K), lambda b, n: (b, n, 0)),
                  rows(prev, kcol), rows(cur, kcol), rows(nxt, kcol), ctx(kcol),
                  rows(prev, vcol), rows(cur, vcol), rows(nxt, vcol), ctx(vcol)],
        out_specs=pl.BlockSpec((None, AB, C_K), lambda b, n: (b, n, 0)),
        compiler_params=_cp(2, 32), name="attention",
    )(sink, qkn, qkn, qkn, qkn, qkn, proj, proj, proj, proj)


def _dftc_kernel(u_ref, w_ref, o_ref):
    o_ref[...] = _dot(u_ref[...], w_ref[...]).astype(o_ref.dtype)


def dft_channels(proj, wc, row0, nrows):
    B = proj.shape[0]
    off = row0 // ROW_T
    return pl.pallas_call(
        _dftc_kernel,
        out_shape=jax.ShapeDtypeStruct((B, nrows, 2 * FG * FGD), BF16),
        grid=(B, nrows // ROW_T),
        in_specs=[pl.BlockSpec((None, ROW_T, FG * FGD), lambda b, j: (b, j + off, C_F // (FG * FGD))),
                  pl.BlockSpec((FG * FGD, 2 * FG * FGD), lambda b, j: (0, 0))],
        out_specs=pl.BlockSpec((None, ROW_T, 2 * FG * FGD), lambda b, j: (b, j, 0)),
        compiler_params=_cp(2, 32), name="dft_channels",
    )(proj, wc)


def _dftp_kernel(c_ref, s_ref, y1_ref, y2_ref, o_ref, acc_ref):
    k = pl.program_id(2)

    @pl.when(k == 0)
    def _():
        acc_ref[...] = jnp.zeros_like(acc_ref)

    acc_ref[...] += _dot(c_ref[...], y1_ref[...]) + _dot(s_ref[...], y2_ref[...])

    @pl.when(k == pl.num_programs(2) - 1)
    def _():
        o_ref[...] = acc_ref[...].astype(o_ref.dtype)


def dft_positions(ct, st, y):
    B, n, w2 = y.shape
    w = w2 // 2
    t = _pick(n, (1024, 512, 256))
    return pl.pallas_call(
        _dftp_kernel,
        out_shape=jax.ShapeDtypeStruct((B, n, w), BF16),
        grid=(B, n // t, n // t),
        in_specs=[pl.BlockSpec((t, t), lambda b, i, k: (i, k)),
                  pl.BlockSpec((t, t), lambda b, i, k: (i, k)),
                  pl.BlockSpec((None, t, w), lambda b, i, k: (b, k, 0)),
                  pl.BlockSpec((None, t, w), lambda b, i, k: (b, k, 1))],
        out_specs=pl.BlockSpec((None, t, w), lambda b, i, k: (b, i, 0)),
        scratch_shapes=[pltpu.VMEM((t, w), F32)],
        compiler_params=_cp(3, 40), name="dft_positions",
    )(ct, st, y, y)


def _log_sigmoid(x):
    return jnp.minimum(x, 0.0) - jnp.log(1.0 + jnp.exp(-jnp.abs(x)))


def _gate_kernel(g_ref, b_ref, o_ref):
    L = ML_L
    g = g_ref[...] + b_ref[...]
    lane = lax.broadcasted_iota(jnp.int32, (L, LANE), 1)
    is_f = (lane & ML_H) == ML_H
    is_bwd = (lane & (2 * ML_H)) == 2 * ML_H
    lf = jnp.where(is_f, _log_sigmoid(g), 0.0)
    it_r = lax.broadcasted_iota(jnp.int32, (L, L), 0)
    it_c = lax.broadcasted_iota(jnp.int32, (L, L), 1)
    hp = lax.Precision.HIGHEST
    cf = jnp.dot(jnp.where(it_c <= it_r, 1.0, 0.0).astype(F32), lf, precision=hp, preferred_element_type=F32)
    cb = jnp.dot(jnp.where(it_c >= it_r, 1.0, 0.0).astype(F32), lf, precision=hp, preferred_element_type=F32)
    o_ref[...] = jnp.where(is_f, jnp.where(is_bwd, cb, cf), g)


def gate_prep(graw, bias):
    B, S, _ = graw.shape
    return pl.pallas_call(
        _gate_kernel,
        out_shape=jax.ShapeDtypeStruct((B, S, LANE), F32),
        grid=(B, S // ML_L),
        in_specs=[pl.BlockSpec((None, ML_L, LANE), lambda b, c: (b, c, 0)),
                  pl.BlockSpec((1, LANE), lambda b, c: (0, 0))],
        out_specs=pl.BlockSpec((None, ML_L, LANE), lambda b, c: (b, c, 0)),
        compiler_params=_cp(2, 16), name="gate_prep",
    )(graw, bias)


def _mlstm_kernel(k_ref, qt_ref, vt_ref, mo_ref, g_ref, gt_ref, ng_ref, o_ref,
                  hf_ref, hb_ref, c_ref, m_ref, *, nctc, nchunk):
    L = ML_L
    c_ref[...] = jnp.zeros_like(c_ref)
    m_ref[...] = jnp.zeros_like(m_ref)
    key_i = lax.broadcasted_iota(jnp.int32, (L, L), 0)
    qry_i = lax.broadcasted_iota(jnp.int32, (L, L), 1)
    tri = (key_i <= qry_i, key_i >= qry_i)
    lane = lax.broadcasted_iota(jnp.int32, (L, LANE), 1)
    head_lanes = (lane < ML_DK, lane >= ML_DK)
    ones_t = jnp.ones((ML_DV, L), BF16)
    h_refs = (hf_ref, hb_ref)

    def step(it, carry):
        stores = []
        for d in range(2):
            if d == 0:
                c = it
            else:
                c = jnp.where(it < nctc, nctc - 1 - it, nchunk - 1 - (it - nctc))
            r0 = pl.multiple_of(c * L, L)
            gc = g_ref[pl.ds(r0, L), :]
            gr = gt_ref[c]
            kp = k_ref[pl.ds(r0, L), :]
            qt = qt_ref[c]
            last = L - 1 if d == 0 else 0
            for hh in range(2):
                icol = d * 4 + hh
                fcol = d * 4 + 2 + hh
                sidx = d * 2 + hh
                key_term = gc[:, icol:icol + 1] - gc[:, fcol:fcol + 1]
                b_row = gr[fcol:fcol + 1, :]
                i_row = gr[icol:icol + 1, :]
                m_prev = m_ref[sidx:sidx + 1, 0:1]
                logd = jnp.where(tri[d], key_term + b_row, NEG)
                inter = b_row + m_prev
                m_t = jnp.maximum(inter, jnp.max(logd, axis=0, keepdims=True))
                km = jnp.where(head_lanes[hh], kp, jnp.zeros_like(kp))
                p_t = (_dot(km, qt) * jnp.exp(logd - m_t)).astype(BF16)
                q_in = (qt.astype(F32) * jnp.exp(inter - m_t)).astype(BF16)
                vext_t = jnp.concatenate([vt_ref[c, hh * ML_DV:(hh + 1) * ML_DV, :], ones_t], axis=0)
                c_t = c_ref[sidx]
                tot = _dot(jnp.concatenate([vext_t, c_t.astype(BF16)], axis=1),
                           jnp.concatenate([p_t, q_in], axis=0))
                h_t = tot[:ML_DV] / jnp.maximum(jnp.abs(tot[ML_DV:]), jnp.exp(-m_t))
                b_last = b_row[:, last:last + 1]
                log_w = b_last - b_row + i_row
                m_new = jnp.maximum(b_last + m_prev, jnp.max(log_w, axis=-1, keepdims=True))
                w_row = jnp.exp(log_w - m_new).astype(BF16)
                c_new = jnp.exp(b_last + m_prev - m_new) * c_t + _dot(vext_t * w_row, km)
                stores.append((d, hh, sidx, c, h_t, c_new, m_new))
        for d, hh, sidx, c, h_t, c_new, m_new in stores:
            h_refs[d][c, hh] = h_t
            c_ref[sidx] = c_new
            m_ref[sidx:sidx + 1, :] = jnp.broadcast_to(m_new, (1, LANE))
        return carry

    lax.fori_loop(0, nchunk, step, 0)

    def out_step(i, carry):
        r0 = pl.multiple_of(i * L, L)
        for hh in range(2):
            hs = slice(hh * ML_DV, (hh + 1) * ML_DV)
            h = jnp.transpose(hf_ref[i, hh] + hb_ref[i, hh])
            hn = h * lax.rsqrt(jnp.mean(h * h, axis=-1, keepdims=True) + EPS) * ng_ref[:, hs]
            og = jax.nn.sigmoid(mo_ref[pl.ds(r0, L), hs].astype(F32))
            o_ref[pl.ds(r0, L), hs] = (og * hn).astype(o_ref.dtype)
        return carry

    lax.fori_loop(0, nchunk, out_step, 0)


def mlstm(proj, q_t, v_t, gates, gates_t, norm_g, lc):
    B, S, _ = proj.shape
    nchunk = S // ML_L
    pw = 2 * ML_DK
    vw = 2 * ML_DV
    chunked = lambda rows: pl.BlockSpec((None, None, nchunk, rows, ML_L), lambda b, p: (b, p, 0, 0, 0))
    return pl.pallas_call(
        functools.partial(_mlstm_kernel, nctc=lc // ML_L, nchunk=nchunk),
        out_shape=jax.ShapeDtypeStruct((B, S, ML_H * ML_DV), BF16),
        grid=(B, ML_H // 2),
        in_specs=[pl.BlockSpec((None, S, pw), lambda b, p: (b, 0, C_MK // pw + p)),
                  chunked(pw), chunked(vw),
                  pl.BlockSpec((None, S, vw), lambda b, p: (b, 0, C_MO // vw + p)),
                  pl.BlockSpec((None, None, S, LANE), lambda b, p: (b, p, 0, 0)),
                  chunked(8),
                  pl.BlockSpec((1, vw), lambda b, p: (0, p))],
        out_specs=pl.BlockSpec((None, S, vw), lambda b, p: (b, 0, p)),
        scratch_shapes=[pltpu.VMEM((nchunk, 2, ML_DV, ML_L), F32), pltpu.VMEM((nchunk, 2, ML_DV, ML_L), F32),
                        pltpu.VMEM((4, 2 * ML_DV, 2 * ML_DK), F32), pltpu.VMEM((8, LANE), F32)],
        compiler_params=_cp(2, 48), name="mlstm",
    )(proj, q_t, v_t, proj, gates, gates_t, norm_g)


def _merge_kernel(a_ref, f_ref, m_ref, wa_ref, wf_ref, wm_ref, ga_ref, gf_ref, gm_ref,
                  ba_ref, bf_ref, bm_ref, o_ref):
    def br(x_ref, w_ref, g_ref, b_ref):
        return jax.nn.sigmoid(g_ref[...].astype(F32) + b_ref[...]) * _dot(x_ref[...], w_ref[...])

    y = br(a_ref, wa_ref, ga_ref, ba_ref) + br(f_ref, wf_ref, gf_ref, bf_ref) + br(m_ref, wm_ref, gm_ref, bm_ref)
    o_ref[...] = y.astype(o_ref.dtype)


def merge(a, f, m, proj, wa, wf, wm, b_gate):
    R, kin = a.shape
    tm = _pick(R, (512, 256))
    tn = 1024
    x_spec = pl.BlockSpec((tm, kin), lambda j, i: (i, 0))
    w_spec = pl.BlockSpec((kin, tn), lambda j, i: (0, j))
    gp = lambda br: pl.BlockSpec((tm, tn), lambda j, i: (i, (C_GP + br * D) // tn + j))
    bg = lambda br: pl.BlockSpec((None, 1, tn), lambda j, i: (br, 0, j))
    return pl.pallas_call(
        _merge_kernel,
        out_shape=jax.ShapeDtypeStruct((R, D), BF16),
        grid=(D // tn, R // tm),
        in_specs=[x_spec, x_spec, x_spec, w_spec, w_spec, w_spec, gp(0), gp(1), gp(2), bg(0), bg(1), bg(2)],
        out_specs=pl.BlockSpec((tm, tn), lambda j, i: (i, j)),
        compiler_params=_cp(2, 40), name="merge",
    )(a, f, m, wa, wf, wm, proj, proj, proj, b_gate, b_gate, b_gate)


def _outproj_kernel(y_ref, w_ref, x_ref, g_ref, o_ref):
    o_ref[...] = x_ref[...] + g_ref[...] * _dot(y_ref[...], w_ref[...])


def out_proj_residual(y, w_out, xa, mod3, k_gate, nct):
    B, S, _ = xa.shape
    return pl.pallas_call(
        _outproj_kernel,
        out_shape=jax.ShapeDtypeStruct((B, S, D), F32),
        grid=(B, S // ROW_T),
        in_specs=[pl.BlockSpec((None, ROW_T, D), lambda b, j: (b, j, 0)),
                  pl.BlockSpec((D, D), lambda b, j: (0, 0)),
                  pl.BlockSpec((None, ROW_T, D), lambda b, j: (b, j, 0)),
                  _mod_spec(k_gate, nct, B)],
        out_specs=pl.BlockSpec((None, ROW_T, D), lambda b, j: (b, j, 0)),
        compiler_params=_cp(2, 40), name="out_proj",
    )(y, w_out, xa, mod3)


def _pack2(lo, hi):
    lo_b = lax.bitcast_convert_type(lo.astype(BF16).astype(F32), jnp.uint32)
    hi_b = lax.bitcast_convert_type(hi.astype(BF16).astype(F32), jnp.uint32)
    return (lo_b >> 16) | (hi_b & jnp.uint32(0xFFFF0000))


def _unpack2(w):
    lo = lax.bitcast_convert_type(w << 16, F32)
    hi = lax.bitcast_convert_type(w & jnp.uint32(0xFFFF0000), F32)
    return lo, hi


def _router_kernel(x_ref, g_ref, sc_ref, sh_ref, wh_ref, wl_ref, b_ref, hp_ref, eid_ref, wt_ref):
    x = x_ref[...]
    y = x * lax.rsqrt(jnp.mean(x * x, axis=-1, keepdims=True) + EPS) * g_ref[...]
    y = y * (1.0 + sc_ref[...]) + sh_ref[...]
    hp_ref[...] = _pack2(y[:, :D // 2], y[:, D // 2:])
    yh, yl = _split(y)
    logits = _dot(yh, wh_ref[...]) + _dot(yl, wh_ref[...]) + _dot(yh, wl_ref[...]) + b_ref[...]
    lane = lax.broadcasted_iota(jnp.int32, logits.shape, 1)
    lanef = lane.astype(F32)
    big = float(LANE)
    glog = jnp.where(lane < N_GRP, logits, NEG)
    gmax = jnp.max(glog, axis=-1, keepdims=True)
    gsel = jnp.min(jnp.where(glog == gmax, lanef, big), axis=-1, keepdims=True)
    pg = 1.0 / jnp.sum(jnp.exp(glog - gmax), axis=-1, keepdims=True)
    lo = N_GRP + EPG * gsel
    el = jnp.where(jnp.logical_and(lanef >= lo, lanef < lo + EPG), logits, NEG)
    v1 = jnp.max(el, axis=-1, keepdims=True)
    i1 = jnp.min(jnp.where(el == v1, lanef, big), axis=-1, keepdims=True)
    el2 = jnp.where(lanef == i1, NEG, el)
    v2 = jnp.max(el2, axis=-1, keepdims=True)
    i2 = jnp.min(jnp.where(el2 == v2, lanef, big), axis=-1, keepdims=True)
    e = jnp.exp(v2 - v1)
    w1 = pg / (1.0 + e)
    w2 = pg * e / (1.0 + e)
    eid_ref[...] = jnp.where(lane == 0, i1 - N_GRP, jnp.where(lane == 1, i2 - N_GRP, 0.0)).astype(jnp.int32)
    wt_ref[...] = jnp.where(lane == 0, w1, jnp.where(lane == 1, w2, 0.0))


def norm_router(xa, g, mod3, k_sh, k_sc, nct, wr_hi, wr_lo, br):
    B, S, _ = xa.shape
    tile = lambda w: pl.BlockSpec((None, ROW_T, w), lambda b, j: (b, j, 0))
    const = lambda shape: pl.BlockSpec(shape, lambda b, j: (0, 0))
    return pl.pallas_call(
        _router_kernel,
        out_shape=(jax.ShapeDtypeStruct((B, S, D // 2), jnp.uint32),
                   jax.ShapeDtypeStruct((B, S, LANE), jnp.int32),
                   jax.ShapeDtypeStruct((B, S, LANE), F32)),
        grid=(B, S // ROW_T),
        in_specs=[tile(D), const((1, D)), _mod_spec(k_sc, nct, B), _mod_spec(k_sh, nct, B),
                  const((D, LANE)), const((D, LANE)), const((1, LANE))],
        out_specs=(tile(D // 2), tile(LANE), tile(LANE)),
        compiler_params=_cp(2, 32), name="norm_router",
    )(xa, g.reshape(1, D), mod3, mod3, wr_hi, wr_lo, br)


DMA_UNROLL = 8


def _row_copy(src_hbm, srow, dst_hbm, drow, sem):
    return pltpu.make_async_copy(src_hbm.at[pl.ds(srow, 1), :], dst_hbm.at[pl.ds(drow, 1), :], sem)


def _expert_kernel(tok_ref, order_ref, rstart_ref, nvalid_ref, be_ref, nexte_ref, nused_ref,
                   x_hbm, w1_hbm, w3_hbm, w2_hbm, y_hbm,
                   xbuf, ybuf, st1, st3, st2, wb1, wb3, wb2, gsem, ssem, wsem, *, layer):
    i = pl.program_id(0)
    nused = nused_ref[0]
    last = pl.num_programs(0) - 1
    used = i < nused
    weights = ((w1_hbm, st1, wb1), (w3_hbm, st3, wb3), (w2_hbm, st2, wb2))

    def gather_rows(blk):
        slot = blk & 1
        rs = rstart_ref[blk]

        def group(k, carry):
            for u in range(DMA_UNROLL):
                src = x_hbm.at[pl.ds(tok_ref[rs + k * DMA_UNROLL + u], 1), :]
                pltpu.make_async_copy(src, xbuf.at[slot, k, pl.ds(u, 1), :], gsem.at[slot]).start(priority=u % 2)
            return carry

        lax.fori_loop(0, MOE_T // DMA_UNROLL, group, 0)

    def scatter_rows(blk, wait):
        slot = blk & 1
        rs = rstart_ref[blk]
        nv = nvalid_ref[blk]

        def row(k, u, r, prio):
            dst = y_hbm.at[pl.ds(0 if wait else order_ref[rs + r], 1), :]
            cp = pltpu.make_async_copy(ybuf.at[slot, k, pl.ds(u, 1), :], dst, ssem.at[slot])
            if wait:
                cp.wait()
            else:
                cp.start(priority=prio)

        def group(k, carry):
            if wait:
                pltpu.make_async_copy(ybuf.at[slot, k], y_hbm.at[pl.ds(0, DMA_UNROLL), :], ssem.at[slot]).wait()
            else:
                for u in range(DMA_UNROLL):
                    row(k, u, k * DMA_UNROLL + u, u % 2)
            return carry

        def tail(r, carry):
            row(lax.shift_right_logical(r, 3), r & (DMA_UNROLL - 1), r, 0)
            return carry

        full = nv // DMA_UNROLL
        lax.fori_loop(0, full, group, 0)
        lax.fori_loop(full * DMA_UNROLL, nv, tail, 0)

    def fetch_weights(e):
        for k, (hbm, st, _) in enumerate(weights):
            pltpu.make_async_copy(hbm.at[layer, e], st, wsem.at[k]).start()

    @pl.when(i == 0)
    def _():
        fetch_weights(be_ref[0])
        gather_rows(0)

    @pl.when(i + 1 < nused)
    def _():
        gather_rows(i + 1)

    @pl.when(jnp.logical_and(i >= 2, i - 2 < nused))
    def _():
        scatter_rows(i - 2, True)

    @pl.when(used)
    def _():
        e = be_ref[i]

        @pl.when(jnp.logical_or(i == 0, be_ref[jnp.maximum(i - 1, 0)] != e))
        def _():
            for k, (hbm, st, wb) in enumerate(weights):
                pltpu.make_async_copy(hbm.at[layer, 0], st, wsem.at[k]).wait()
                wb[...] = st[...].astype(BF16)

            @pl.when(nexte_ref[i] >= 0)
            def _():
                fetch_weights(nexte_ref[i])

        slot = i & 1
        pltpu.make_async_copy(xbuf.at[slot], xbuf.at[slot], gsem.at[slot]).wait()
        xl, xh = _unpack2(xbuf[slot].reshape(MOE_T, D // 2))
        xl = xl.astype(BF16)
        xh = xh.astype(BF16)
        half = D // 2
        a = _dot(xl, wb1[:half, :]) + _dot(xh, wb1[half:, :])
        b = _dot(xl, wb3[:half, :]) + _dot(xh, wb3[half:, :])
        hmid = (a * jax.nn.sigmoid(a) * b).astype(BF16)
        y = _dot(hmid, wb2[...])
        ybuf[slot] = _pack2(y[:, :half], y[:, half:]).reshape(MOE_T // DMA_UNROLL, DMA_UNROLL, half)
        scatter_rows(i, False)

    @pl.when(i == last)
    def _():
        @pl.when(jnp.logical_and(i >= 1, i - 1 < nused))
        def _():
            scatter_rows(i - 1, True)

        @pl.when(used)
        def _():
            scatter_rows(i, True)


def expert_ffn(hp, plan, w1, w3, w2, layer, n_blocks, n_pairs):
    w = hp.shape[1]
    hbm = pl.BlockSpec(memory_space=pl.ANY)
    grid_spec = pltpu.PrefetchScalarGridSpec(
        num_scalar_prefetch=7, grid=(n_blocks,),
        in_specs=[hbm, hbm, hbm, hbm],
        out_specs=hbm,
        scratch_shapes=[pltpu.VMEM((2, MOE_T // DMA_UNROLL, DMA_UNROLL, w), jnp.uint32),
                        pltpu.VMEM((2, MOE_T // DMA_UNROLL, DMA_UNROLL, w), jnp.uint32),
                        pltpu.VMEM((D, FF), F32), pltpu.VMEM((D, FF), F32), pltpu.VMEM((FF, D), F32),
                        pltpu.VMEM((D, FF), BF16), pltpu.VMEM((D, FF), BF16), pltpu.VMEM((FF, D), BF16),
                        pltpu.SemaphoreType.DMA((2,)), pltpu.SemaphoreType.DMA((2,)),
                        pltpu.SemaphoreType.DMA((3,))])
    return pl.pallas_call(
        functools.partial(_expert_kernel, layer=layer),
        out_shape=jax.ShapeDtypeStruct((n_pairs, w), jnp.uint32),
        grid_spec=grid_spec, compiler_params=_cp(1, 56), name="moe_experts",
    )(*plan, hp, w1, w3, w2)


def _combine_kernel(ya_ref, yb_ref, x_ref, wt_ref, g_ref, o_ref):
    w0 = wt_ref[:, 0:1]
    w1 = wt_ref[:, 1:2]
    al, ah = _unpack2(ya_ref[...])
    bl, bh = _unpack2(yb_ref[...])
    half = D // 2
    g = g_ref[...]
    o_ref[:, :half] = x_ref[:, :half] + g[:, :half] * (w0 * al + w1 * bl)
    o_ref[:, half:] = x_ref[:, half:] + g[:, half:] * (w0 * ah + w1 * bh)


def combine(y2, xa, wts, mod3, k_gate, nct, latent_only):
    B, S, _ = xa.shape
    w = y2.shape[-1]
    off = nct if latent_only else 0
    tile = lambda wd: pl.BlockSpec((None, ROW_T, wd), lambda b, j: (b, j + off, 0))
    pick = lambda k: pl.BlockSpec((None, None, ROW_T, w), lambda b, j: (k, b, j + off, 0))
    return pl.pallas_call(
        _combine_kernel,
        out_shape=jax.ShapeDtypeStruct((B, S - off * ROW_T, D), F32),
        grid=(B, S // ROW_T - off),
        in_specs=[pick(0), pick(1), tile(D), tile(LANE), _mod_spec(k_gate, nct, B, off)],
        out_specs=pl.BlockSpec((None, ROW_T, D), lambda b, j: (b, j, 0)),
        compiler_params=_cp(2, 32), name="moe_combine",
    )(y2, y2, xa, wts, mod3)


def moe_plan(eid, n_blocks, n_tok):
    order = jnp.argsort(eid).astype(jnp.int32)
    tok = jnp.pad(jnp.where(order >= n_tok, order - n_tok, order), (0, MOE_T))
    experts = jnp.arange(N_EXP, dtype=jnp.int32)
    counts = jnp.sum((eid[:, None] == experts[None, :]).astype(jnp.int32), axis=0)
    nblk_e = (counts + MOE_T - 1) // MOE_T
    bend = jnp.cumsum(nblk_e)
    bstart = bend - nblk_e
    start = jnp.cumsum(counts) - counts
    blk = jnp.arange(n_blocks, dtype=jnp.int32)
    owner = jnp.logical_and(blk[:, None] >= bstart[None, :], blk[:, None] < bend[None, :]).astype(jnp.int32)
    take = lambda v: jnp.sum(owner * v[None, :], axis=1)
    within = blk - take(bstart)
    nused = bend[-1]
    last_e = jnp.max(jnp.where(counts > 0, experts, 0))
    block_e = jnp.where(blk < nused, take(experts), last_e).astype(jnp.int32)
    rstart = jnp.where(blk < nused, take(start) + within * MOE_T, 0).astype(jnp.int32)
    nvalid = jnp.where(blk < nused, jnp.clip(take(counts) - within * MOE_T, 0, MOE_T), 0).astype(jnp.int32)
    nxt = take(bend)
    e_at_nxt = jnp.sum((blk[None, :] == nxt[:, None]).astype(jnp.int32) * block_e[None, :], axis=1)
    next_e = jnp.where(jnp.logical_and(blk < nused, nxt < nused), e_at_nxt, -1).astype(jnp.int32)
    return tok, order, rstart, nvalid, block_e, next_e, nused.astype(jnp.int32).reshape(1)


def _rope_tables(s, lc):
    t = np.arange(s - lc)
    row = (t // GRID_W).astype(np.float64)
    col = (t % GRID_W).astype(np.float64)
    nf = HD // 4
    inv = np.float32(ROPE_BASE) ** (-np.arange(nf, dtype=np.float32) / np.float32(nf))
    ang = np.concatenate([row[:, None] * inv, col[:, None] * inv], axis=-1)
    ang = np.concatenate([np.zeros((lc, HD // 2)), ang], axis=0)
    cos = np.repeat(np.cos(ang), 2, axis=-1)
    sin = np.repeat(np.sin(ang), 2, axis=-1) * np.tile([-1.0, 1.0], HD // 2)
    return jnp.asarray(np.tile(cos, (1, LANE // HD)), F32), jnp.asarray(np.tile(sin, (1, LANE // HD)), F32)


def _chan_dft():
    j = np.arange(FGD)
    ang = 2.0 * np.pi * ((j[:, None] * j[None, :]) % FGD) / FGD
    c = np.kron(np.eye(FG), np.cos(ang)) / np.sqrt(FGD)
    s = np.kron(np.eye(FG), np.sin(ang)) / np.sqrt(FGD)
    return jnp.asarray(np.concatenate([c, -s], axis=1), BF16)


def _pos_dft(n):
    if n <= 512:
        k = np.arange(n)
        ang = 2.0 * np.pi * ((k[:, None] * k[None, :]) % n) / n
        return (jnp.asarray(np.cos(ang) / np.sqrt(n), BF16), jnp.asarray(np.sin(ang) / np.sqrt(n), BF16))
    nr = n // LANE
    k = np.arange(n)
    a = 2.0 * np.pi * ((k[:, None] * np.arange(nr)[None, :]) % nr) / nr
    b = 2.0 * np.pi * (k[:, None] * np.arange(LANE)[None, :]) / n
    sc = 1.0 / np.sqrt(n)
    ca, sa = jnp.asarray(np.cos(a) * sc, F32)[:, :, None], jnp.asarray(np.sin(a) * sc, F32)[:, :, None]
    cb, sb = jnp.asarray(np.cos(b), F32)[:, None, :], jnp.asarray(np.sin(b), F32)[:, None, :]
    ct = (ca * cb - sa * sb).reshape(n, n).astype(BF16)
    st = (sa * cb + ca * sb).reshape(n, n).astype(BF16)
    return ct, st


def _proj_weights(w_in):
    dup = lambda w: jnp.concatenate([w.reshape(D, N_KV, HD)] * 2, axis=-1).reshape(D, 2 * N_KV * HD)
    w_big = jnp.concatenate([w_in[:, 0:1024], dup(w_in[:, 1024:1280]), dup(w_in[:, 1280:1536]),
                             w_in[:, 1536:4608], w_in[:, 4640:]], axis=1).astype(BF16)
    w_gate = jnp.pad(w_in[:, 4608:4640], ((0, 0), (0, LANE - 32))).astype(BF16)
    return w_big, w_gate


def _qk_gain(qg, kg):
    qs = jnp.tile(qg * (HD ** -0.5 * LOG2E), N_HEADS)
    return jnp.concatenate([qs, jnp.tile(kg, 2 * N_KV)]).reshape(1, -1).astype(F32)


def kernel(x, c, ctx, c_ctx, w_mod, b_mod, norm1_g, norm2_g, w_in, q_norm_g, k_norm_g, attn_sink, ml_gate_b,
           ml_norm_g, w_br_attn, w_br_four, w_br_mlstm, b_gate, w_out, w_grp, b_grp, w_exp_router,
           b_exp_router, w1, w3, w2):
    B, T, _ = x.shape
    lc = ctx.shape[1]
    S = lc + T
    R = B * S
    depth = w_mod.shape[0]
    nct = lc // ROW_T
    assert lc % ROW_T == 0 and T % ROW_T == 0 and B < 8

    xa = jnp.concatenate([ctx, x], axis=1)
    cvec = jnp.concatenate([c, c_ctx[None], jnp.zeros((7 - B, D), F32)], axis=0)
    cos_t, sin_t = _rope_tables(S, lc)
    bd = jnp.asarray(np.kron(np.eye(LANE // HD), np.ones((HD, HD))), BF16)
    wc = _chan_dft()
    ct_lat, st_lat = _pos_dft(T)
    ct_ctx, st_ctx = _pos_dft(lc)
    nchunk = S // ML_L
    n_pairs = R * 2
    n_blocks = -(-n_pairs // MOE_T) + N_EXP

    for l in range(depth):
        mod = modulation(cvec, w_mod, b_mod, l)
        mod3 = mod.reshape(8, 1, 6 * D)
        h = norm_mod(xa, norm1_g[l], mod3, 0, 1, nct)
        w_big, w_gate = _proj_weights(w_in[l])
        h2d = h.reshape(R, D)
        proj = mm(h2d, w_big, BF16, 1024).reshape(B, S, C_TOT)
        graw = mm(h2d, w_gate, F32, LANE).reshape(B, S, LANE)

        qkn = qk_prep(proj, _qk_gain(q_norm_g[l], k_norm_g[l]), cos_t, sin_t, bd)
        a = attention(qkn, proj, attn_sink[l].astype(F32) * LOG2E, lc)

        f_ctx = dft_positions(ct_ctx, st_ctx, dft_channels(proj, wc, 0, lc))
        f_lat = dft_positions(ct_lat, st_lat, dft_channels(proj, wc, lc, T))
        f = jnp.concatenate([f_ctx, f_lat], axis=1)

        gbias = jnp.pad(ml_gate_b[l].reshape(1, 4 * ML_H).astype(F32), ((0, 0), (0, LANE - 4 * ML_H)))
        gproc = gate_prep(graw, gbias)[..., :4 * ML_H]
        g5 = gproc.reshape(B, S, 2, 2, ML_H // 2, 2).transpose(0, 4, 1, 2, 3, 5).reshape(B, ML_H // 2, S, 8)
        gates = jnp.pad(g5, ((0, 0), (0, 0), (0, 0), (0, LANE - 8)))
        gates_t = g5.reshape(B, ML_H // 2, nchunk, ML_L, 8).transpose(0, 1, 2, 4, 3)
        chunk_t = lambda a, w: a.reshape(B, nchunk, ML_L, ML_H // 2, w).transpose(0, 3, 1, 4, 2)
        q_t = chunk_t(proj[..., C_MQ:C_MK] * jnp.asarray(ML_DK ** -0.5, BF16), 2 * ML_DK)
        v_t = chunk_t(proj[..., C_MV:C_MO], 2 * ML_DV)
        m = mlstm(proj, q_t, v_t, gates, gates_t, ml_norm_g[l].reshape(1, -1).astype(F32), lc)

        y = merge(a.reshape(R, -1), f.reshape(R, -1), m.reshape(R, -1), proj.reshape(R, C_TOT),
                  w_br_attn[l].astype(BF16), w_br_four[l].astype(BF16), w_br_mlstm[l].astype(BF16),
                  b_gate[l].reshape(3, 1, D).astype(F32))
        xa = out_proj_residual(y.reshape(B, S, D), w_out[l].astype(BF16), xa, mod3, 2, nct)

        wr = jnp.concatenate([w_grp[l], w_exp_router[l].reshape(D, N_EXP)], axis=1)
        wr = jnp.pad(wr, ((0, 0), (0, LANE - N_GRP - N_EXP))).astype(F32)
        wr_hi = wr.astype(BF16)
        wr_lo = (wr - wr_hi.astype(F32)).astype(BF16)
        br = jnp.pad(jnp.concatenate([b_grp[l], b_exp_router[l].reshape(N_EXP)]),
                     (0, LANE - N_GRP - N_EXP)).reshape(1, LANE).astype(F32)
        hp, eid, wts = norm_router(xa, norm2_g[l], mod3, 3, 4, nct, wr_hi, wr_lo, br)
        plan = moe_plan(jnp.concatenate([eid[..., 0].reshape(R), eid[..., 1].reshape(R)]), n_blocks, R)
        y2 = expert_ffn(hp.reshape(R, D // 2), plan, w1, w3, w2, l, n_blocks, n_pairs)
        xa = combine(y2.reshape(2, B, S, D // 2), xa, wts, mod3, 5, nct, l == depth - 1)

    return xa
```

```python
import functools

import numpy as np
import jax
import jax.numpy as jnp
from jax import lax
from jax.experimental import pallas as pl
from jax.experimental.pallas import tpu as pltpu

F32 = jnp.float32
BF16 = jnp.bfloat16

D = 2048
N_HEADS, N_KV, HD = 16, 4, 64
GRID_W = 64
ROPE_BASE = 10000.0
EPS = 1e-6
ML_H, ML_DK, ML_DV = 8, 64, 128
ML_L = 256
FG, FGD = 4, 256
N_GRP, EPG, N_EXP, FF = 4, 8, 32, 1024
NEG = -1e30
LOG2E = 1.4426950408889634
LANE = 128
ROW_T = 256
MOE_T = 256

C_Q, C_K, C_V = 0, 1024, 1536
C_F, C_GP = 2048, 3072
C_MK, C_MO, C_TOT = 9216, 9728, 10752
QV_ROWS = ML_H * (ML_DK + ML_DV)


def _cp(n_axes, vmem_mb):
    return pltpu.CompilerParams(dimension_semantics=("arbitrary",) * n_axes,
                                vmem_limit_bytes=int(vmem_mb) * 2 ** 20)


def _pick(n, cands):
    for c in cands:
        if n % c == 0:
            return c
    raise ValueError(f"no tile for {n} in {cands}")


def _dot(a, b):
    return jnp.dot(a, b, preferred_element_type=F32)


def _dot_nt(a, b):
    return lax.dot_general(a, b, (((1,), (1,)), ((), ())), preferred_element_type=F32)


def _split(x):
    hi = x.astype(BF16)
    return hi, (x - hi.astype(F32)).astype(BF16)


def _dot3(a, w):
    ah, al = _split(a)
    wh, wl = _split(w)
    return _dot(ah, wh) + _dot(al, wh) + _dot(ah, wl)


def _mod_kernel(a_ref, w_ref, b_ref, o_ref):
    a = a_ref[...]
    a = a * jax.nn.sigmoid(a)
    o_ref[...] = _dot3(a, w_ref[...]) + b_ref[...]


def modulation(cvec, w_mod, b_mod, layer):
    depth, _, n = w_mod.shape
    tn = 512
    return pl.pallas_call(
        _mod_kernel,
        out_shape=jax.ShapeDtypeStruct((8, n), F32),
        grid=(n // tn,),
        in_specs=[pl.BlockSpec((8, D), lambda j: (0, 0)),
                  pl.BlockSpec((None, D, tn), lambda j: (layer, 0, j)),
                  pl.BlockSpec((None, 1, tn), lambda j: (layer, 0, j))],
        out_specs=pl.BlockSpec((8, tn), lambda j: (0, j)),
        compiler_params=_cp(1, 24), name="modulation",
    )(cvec, w_mod, b_mod.reshape(depth, 1, n))


def _mod_spec(k, nct, nb, off=0):
    return pl.BlockSpec((None, 1, D), lambda b, j: (jnp.where(j + off < nct, nb, b), 0, k))


def _norm_kernel(x_ref, g_ref, sc_ref, sh_ref, o_ref):
    x = x_ref[...]
    y = x * lax.rsqrt(jnp.mean(x * x, axis=-1, keepdims=True) + EPS) * g_ref[...]
    o_ref[...] = (y * (1.0 + sc_ref[...]) + sh_ref[...]).astype(o_ref.dtype)


def norm_mod(xa, g, mod3, k_sh, k_sc, nct):
    B, S, _ = xa.shape
    return pl.pallas_call(
        _norm_kernel,
        out_shape=jax.ShapeDtypeStruct((B, S, D), BF16),
        grid=(B, S // ROW_T),
        in_specs=[pl.BlockSpec((None, ROW_T, D), lambda b, j: (b, j, 0)),
                  pl.BlockSpec((1, D), lambda b, j: (0, 0)),
                  _mod_spec(k_sc, nct, B), _mod_spec(k_sh, nct, B)],
        out_specs=pl.BlockSpec((None, ROW_T, D), lambda b, j: (b, j, 0)),
        compiler_params=_cp(2, 24), name="norm_mod",
    )(xa, g.reshape(1, D), mod3, mod3)


def _mm_kernel(a_ref, b_ref, o_ref):
    o_ref[...] = _dot(a_ref[...], b_ref[...]).astype(o_ref.dtype)


def mm(a, b, out_dtype, tn):
    M, K = a.shape
    N = b.shape[1]
    tm = _pick(M, (1024, 512, 256))
    return pl.pallas_call(
        _mm_kernel,
        out_shape=jax.ShapeDtypeStruct((M, N), out_dtype),
        grid=(M // tm, N // tn),
        in_specs=[pl.BlockSpec((tm, K), lambda i, j: (i, 0)),
                  pl.BlockSpec((K, tn), lambda i, j: (0, j))],
        out_specs=pl.BlockSpec((tm, tn), lambda i, j: (i, j)),
        compiler_params=_cp(2, 40), name="mm",
    )(a, b)


def _mm_t_kernel(w_ref, h_ref, o_ref):
    o_ref[...] = _dot_nt(w_ref[...], h_ref[...]).astype(o_ref.dtype)


def mm_chunk_transposed(h, w_t, chunk):
    B, S, K = h.shape
    n = w_t.shape[0]
    return pl.pallas_call(
        _mm_t_kernel,
        out_shape=jax.ShapeDtypeStruct((B, S // chunk, n, chunk), BF16),
        grid=(B, S // chunk),
        in_specs=[pl.BlockSpec((n, K), lambda b, c: (0, 0)),
                  pl.BlockSpec((None, chunk, K), lambda b, c: (b, c, 0))],
        out_specs=pl.BlockSpec((None, None, n, chunk), lambda b, c: (b, c, 0, 0)),
        compiler_params=_cp(2, 40), name="mm_t",
    )(w_t, h)


def _prep_kernel(x_ref, g_ref, cos_ref, sin_ref, bd_ref, o_ref):
    cosv = cos_ref[...]
    sinv = sin_ref[...]
    bd = bd_ref[...]
    lane = lax.broadcasted_iota(jnp.int32, cosv.shape, 1)
    even = (lane & 1) == 0
    for s in range((C_V - C_Q) // LANE):
        sl = slice(s * LANE, (s + 1) * LANE)
        x = x_ref[:, sl].astype(F32)
        hi, lo = _split(x * x)
        ss = _dot(hi, bd) + _dot(lo, bd)
        y = x * lax.rsqrt(ss * (1.0 / HD) + EPS) * g_ref[:, sl]
        sw = jnp.where(even, pltpu.roll(y, LANE - 1, 1), pltpu.roll(y, 1, 1))
        o_ref[:, sl] = (y * cosv + sw * sinv).astype(o_ref.dtype)


def qk_prep(proj, gain, cos_t, sin_t, bd):
    B, S, _ = proj.shape
    w = C_V - C_Q
    return pl.pallas_call(
        _prep_kernel,
        out_shape=jax.ShapeDtypeStruct((B, S, w), BF16),
        grid=(B, S // ROW_T),
        in_specs=[pl.BlockSpec((None, ROW_T, w), lambda b, j: (b, j, 0)),
                  pl.BlockSpec((1, w), lambda b, j: (0, 0)),
                  pl.BlockSpec((ROW_T, LANE), lambda b, j: (j, 0)),
                  pl.BlockSpec((ROW_T, LANE), lambda b, j: (j, 0)),
                  pl.BlockSpec((LANE, LANE), lambda b, j: (0, 0))],
        out_specs=pl.BlockSpec((None, ROW_T, w), lambda b, j: (b, j, 0)),
        compiler_params=_cp(2, 24), name="qk_prep",
    )(proj, gain, cos_t, sin_t, bd)


AB = 128


def _attn_kernel(sink_ref, q_ref, kp_ref, kc_ref, kn_ref, kx_ref, vp_ref, vc_ref, vn_ref, vx_ref, o_ref,
                 *, nctb, nblk):
    n = pl.program_id(1)
    is_lat = n >= nctb
    iq = lax.broadcasted_iota(jnp.int32, (AB, AB), 0)
    ik = lax.broadcasted_iota(jnp.int32, (AB, AB), 1)
    ok_p = jnp.logical_and(ik >= iq, jnp.logical_and(is_lat, n - 1 >= nctb))
    ok_c = jnp.logical_and(ik >= 0, is_lat)
    ok_n = jnp.logical_and(ik <= iq, jnp.logical_and(is_lat, n + 1 <= nblk - 1))
    nloc = 3 * AB
    bias = jnp.concatenate([jnp.where(ok_p, 0.0, NEG), jnp.where(ok_c, 0.0, NEG), jnp.where(ok_n, 0.0, NEG)], axis=1)
    kall = jnp.concatenate([kp_ref[...], kc_ref[...], kn_ref[...], kx_ref[...]], axis=0)
    vall = jnp.concatenate([vp_ref[...], vc_ref[...], vn_ref[...], vx_ref[...]], axis=0)
    ones = jnp.ones((kall.shape[0], LANE), BF16)
    lane = lax.broadcasted_iota(jnp.int32, (AB, LANE), 1)
    lo_half = lane < HD
    hi_half = jnp.logical_not(lo_half)
    gq = N_HEADS // N_KV
    for g in range(N_KV):
        kg = kall[:, g * LANE:(g + 1) * LANE]
        vext = jnp.concatenate([vall[:, g * LANE:(g + 1) * LANE], ones], axis=1)
        qparts = []
        for jj in range(2):
            qs = q_ref[:, (2 * g + jj) * LANE:(2 * g + jj + 1) * LANE]
            qparts += [jnp.where(lo_half, qs, jnp.zeros_like(qs)), jnp.where(hi_half, qs, jnp.zeros_like(qs))]
        s = _dot_nt(jnp.concatenate(qparts, axis=0), kg)
        ps, sink_w = [], []
        for seg in range(gq):
            sl = s[seg * AB:(seg + 1) * AB]
            s_loc = sl[:, :nloc] + bias
            s_ctx = sl[:, nloc:]
            sk = sink_ref[gq * g + seg]
            m = jnp.maximum(jnp.maximum(jnp.max(s_loc, axis=-1, keepdims=True),
                                        jnp.max(s_ctx, axis=-1, keepdims=True)), sk)
            ps.append(jnp.concatenate([jnp.exp2(s_loc - m), jnp.exp2(s_ctx - m)], axis=1).astype(BF16))
            sink_w.append(jnp.exp2(sk - m))
        o = _dot(jnp.concatenate(ps, axis=0), vext)
        for jj in range(2):
            outs = []
            for half in range(2):
                seg = 2 * jj + half
                rows = slice(seg * AB, (seg + 1) * AB)
                outs.append(o[rows, :LANE] / (o[rows, LANE:] + sink_w[seg]))
            sl_out = slice((2 * g + jj) * LANE, (2 * g + jj + 1) * LANE)
            o_ref[:, sl_out] = jnp.where(lo_half, outs[0], outs[1]).astype(o_ref.dtype)


def attention(qkn, proj, sink, lc):
    B, S, _ = qkn.shape
    nblk = S // AB
    nctb = lc // AB
    kw = C_V - C_K
    kcol, vcol = C_K // kw, C_V // kw

    def rows(fn, col):
        return pl.BlockSpec((None, AB, kw), lambda b, n: (b, fn(n), col))

    prev = lambda n: jnp.maximum(n - 1, 0)
    cur = lambda n: n
    nxt = lambda n: jnp.minimum(n + 1, nblk - 1)
    ctx = lambda col: pl.BlockSpec((None, lc, kw), lambda b, n: (b, 0, col))
    return pl.pallas_call(
        functools.partial(_attn_kernel, nctb=nctb, nblk=nblk),
        out_shape=jax.ShapeDtypeStruct((B, S, C_K), BF16),
        grid=(B, nblk),
        in_specs=[pl.BlockSpec(memory_space=pltpu.SMEM),
                  pl.BlockSpec((None, AB, C_K), lambda b, n: (b, n, 0)),
                  rows(prev, kcol), rows(cur, kcol), rows(nxt, kcol), ctx(kcol),
                  rows(prev, vcol), rows(cur, vcol), rows(nxt, vcol), ctx(vcol)],
        out_specs=pl.BlockSpec((None, AB, C_K), lambda b, n: (b, n, 0)),
        compiler_params=_cp(2, 32), name="attention",
    )(sink, qkn, qkn, qkn, qkn, qkn, proj, proj, proj, proj)


def _dftc_kernel(u_ref, w_ref, o_ref):
    o_ref[...] = _dot(u_ref[...], w_ref[...]).astype(o_ref.dtype)


def dft_channels(proj, wc, row0, nrows):
    B = proj.shape[0]
    off = row0 // ROW_T
    return pl.pallas_call(
        _dftc_kernel,
        out_shape=jax.ShapeDtypeStruct((B, nrows, 2 * FG * FGD), BF16),
        grid=(B, nrows // ROW_T),
        in_specs=[pl.BlockSpec((None, ROW_T, FG * FGD), lambda b, j: (b, j + off, C_F // (FG * FGD))),
                  pl.BlockSpec((FG * FGD, 2 * FG * FGD), lambda b, j: (0, 0))],
        out_specs=pl.BlockSpec((None, ROW_T, 2 * FG * FGD), lambda b, j: (b, j, 0)),
        compiler_params=_cp(2, 32), name="dft_channels",
    )(proj, wc)


def _dftp_kernel(c_ref, s_ref, y1_ref, y2_ref, o_ref, acc_ref):
    k = pl.program_id(2)

    @pl.when(k == 0)
    def _():
        acc_ref[...] = jnp.zeros_like(acc_ref)

    acc_ref[...] += _dot(c_ref[...], y1_ref[...]) + _dot(s_ref[...], y2_ref[...])

    @pl.when(k == pl.num_programs(2) - 1)
    def _():
        o_ref[...] = acc_ref[...].astype(o_ref.dtype)


def dft_positions(ct, st, y):
    B, n, w2 = y.shape
    w = w2 // 2
    t = _pick(n, (1024, 512, 256))
    return pl.pallas_call(
        _dftp_kernel,
        out_shape=jax.ShapeDtypeStruct((B, n, w), BF16),
        grid=(B, n // t, n // t),
        in_specs=[pl.BlockSpec((t, t), lambda b, i, k: (i, k)),
                  pl.BlockSpec((t, t), lambda b, i, k: (i, k)),
                  pl.BlockSpec((None, t, w), lambda b, i, k: (b, k, 0)),
                  pl.BlockSpec((None, t, w), lambda b, i, k: (b, k, 1))],
        out_specs=pl.BlockSpec((None, t, w), lambda b, i, k: (b, i, 0)),
        scratch_shapes=[pltpu.VMEM((t, w), F32)],
        compiler_params=_cp(3, 40), name="dft_positions",
    )(ct, st, y, y)


def _log_sigmoid(x):
    return jnp.minimum(x, 0.0) - jnp.log(1.0 + jnp.exp(-jnp.abs(x)))


def _gate_kernel(g_ref, b_ref, o_ref):
    L = ML_L
    g = g_ref[...] + b_ref[...]
    lane = lax.broadcasted_iota(jnp.int32, (L, LANE), 1)
    is_f = (lane & ML_H) == ML_H
    is_bwd = (lane & (2 * ML_H)) == 2 * ML_H
    lf = jnp.where(is_f, _log_sigmoid(g), 0.0)
    it_r = lax.broadcasted_iota(jnp.int32, (L, L), 0)
    it_c = lax.broadcasted_iota(jnp.int32, (L, L), 1)
    hp = lax.Precision.HIGHEST
    cf = jnp.dot(jnp.where(it_c <= it_r, 1.0, 0.0).astype(F32), lf, precision=hp, preferred_element_type=F32)
    cb = jnp.dot(jnp.where(it_c >= it_r, 1.0, 0.0).astype(F32), lf, precision=hp, preferred_element_type=F32)
    o_ref[...] = jnp.where(is_f, jnp.where(is_bwd, cb, cf), g)


def gate_prep(graw, bias):
    B, S, _ = graw.shape
    return pl.pallas_call(
        _gate_kernel,
        out_shape=jax.ShapeDtypeStruct((B, S, LANE), F32),
        grid=(B, S // ML_L),
        in_specs=[pl.BlockSpec((None, ML_L, LANE), lambda b, c: (b, c, 0)),
                  pl.BlockSpec((1, LANE), lambda b, c: (0, 0))],
        out_specs=pl.BlockSpec((None, ML_L, LANE), lambda b, c: (b, c, 0)),
        compiler_params=_cp(2, 16), name="gate_prep",
    )(graw, bias)


def _mlstm_kernel(k_ref, qt_ref, vt_ref, mo_ref, g_ref, gt_ref, ng_ref, o_ref,
                  hf_ref, hb_ref, c_ref, m_ref, *, nctc, nchunk):
    L = ML_L
    c_ref[...] = jnp.zeros_like(c_ref)
    m_ref[...] = jnp.zeros_like(m_ref)
    key_i = lax.broadcasted_iota(jnp.int32, (L, L), 0)
    qry_i = lax.broadcasted_iota(jnp.int32, (L, L), 1)
    tri = (key_i <= qry_i, key_i >= qry_i)
    lane = lax.broadcasted_iota(jnp.int32, (L, LANE), 1)
    head_lanes = (lane < ML_DK, lane >= ML_DK)
    ones_t = jnp.ones((ML_DV, L), BF16)
    h_refs = (hf_ref, hb_ref)

    def step(it, carry):
        stores = []
        for d in range(2):
            if d == 0:
                c = it
            else:
                c = jnp.where(it < nctc, nctc - 1 - it, nchunk - 1 - (it - nctc))
            r0 = pl.multiple_of(c * L, L)
            gc = g_ref[pl.ds(r0, L), :]
            gr = gt_ref[c]
            kp = k_ref[pl.ds(r0, L), :]
            qt = qt_ref[c]
            last = L - 1 if d == 0 else 0
            for hh in range(2):
                icol = d * 4 + hh
                fcol = d * 4 + 2 + hh
                sidx = d * 2 + hh
                key_term = gc[:, icol:icol + 1] - gc[:, fcol:fcol + 1]
                b_row = gr[fcol:fcol + 1, :]
                i_row = gr[icol:icol + 1, :]
                m_prev = m_ref[sidx:sidx + 1, 0:1]
                logd = jnp.where(tri[d], key_term + b_row, NEG)
                inter = b_row + m_prev
                m_t = jnp.maximum(inter, jnp.max(logd, axis=0, keepdims=True))
                km = jnp.where(head_lanes[hh], kp, jnp.zeros_like(kp))
                p_t = (_dot(km, qt) * jnp.exp(logd - m_t)).astype(BF16)
                q_in = (qt.astype(F32) * jnp.exp(inter - m_t)).astype(BF16)
                vext_t = jnp.concatenate([vt_ref[c, hh * ML_DV:(hh + 1) * ML_DV, :], ones_t], axis=0)
                c_t = c_ref[sidx]
                tot = _dot(jnp.concatenate([vext_t, c_t.astype(BF16)], axis=1),
                           jnp.concatenate([p_t, q_in], axis=0))
                h_t = tot[:ML_DV] / jnp.maximum(jnp.abs(tot[ML_DV:]), jnp.exp(-m_t))
                b_last = b_row[:, last:last + 1]
                log_w = b_last - b_row + i_row
                m_new = jnp.maximum(b_last + m_prev, jnp.max(log_w, axis=-1, keepdims=True))
                w_row = jnp.exp(log_w - m_new).astype(BF16)
                c_new = jnp.exp(b_last + m_prev - m_new) * c_t + _dot(vext_t * w_row, km)
                stores.append((d, hh, sidx, c, h_t, c_new, m_new))
        for d, hh, sidx, c, h_t, c_new, m_new in stores:
            h_refs[d][c, hh] = h_t
            c_ref[sidx] = c_new
            m_ref[sidx:sidx + 1, :] = jnp.broadcast_to(m_new, (1, LANE))
        return carry

    lax.fori_loop(0, nchunk, step, 0)

    def out_step(i, carry):
        r0 = pl.multiple_of(i * L, L)
        for hh in range(2):
            hs = slice(hh * ML_DV, (hh + 1) * ML_DV)
            h = jnp.transpose(hf_ref[i, hh] + hb_ref[i, hh])
            hn = h * lax.rsqrt(jnp.mean(h * h, axis=-1, keepdims=True) + EPS) * ng_ref[:, hs]
            og = jax.nn.sigmoid(mo_ref[pl.ds(r0, L), hs].astype(F32))
            o_ref[pl.ds(r0, L), hs] = (og * hn).astype(o_ref.dtype)
        return carry

    lax.fori_loop(0, nchunk, out_step, 0)


def mlstm(proj, qv_t, gates, gates_t, norm_g, lc):
    B, S, _ = proj.shape
    nchunk = S // ML_L
    pw = 2 * ML_DK
    vw = 2 * ML_DV
    return pl.pallas_call(
        functools.partial(_mlstm_kernel, nctc=lc // ML_L, nchunk=nchunk),
        out_shape=jax.ShapeDtypeStruct((B, S, ML_H * ML_DV), BF16),
        grid=(B, ML_H // 2),
        in_specs=[pl.BlockSpec((None, S, pw), lambda b, p: (b, 0, C_MK // pw + p)),
                  pl.BlockSpec((None, nchunk, pw, ML_L), lambda b, p: (b, 0, p, 0)),
                  pl.BlockSpec((None, nchunk, vw, ML_L), lambda b, p: (b, 0, ML_H * ML_DK // vw + p, 0)),
                  pl.BlockSpec((None, S, vw), lambda b, p: (b, 0, C_MO // vw + p)),
                  pl.BlockSpec((None, None, S, LANE), lambda b, p: (b, p, 0, 0)),
                  pl.BlockSpec((None, None, nchunk, 8, ML_L), lambda b, p: (b, p, 0, 0, 0)),
                  pl.BlockSpec((1, vw), lambda b, p: (0, p))],
        out_specs=pl.BlockSpec((None, S, vw), lambda b, p: (b, 0, p)),
        scratch_shapes=[pltpu.VMEM((nchunk, 2, ML_DV, ML_L), F32), pltpu.VMEM((nchunk, 2, ML_DV, ML_L), F32),
                        pltpu.VMEM((4, 2 * ML_DV, 2 * ML_DK), F32), pltpu.VMEM((8, LANE), F32)],
        compiler_params=_cp(2, 48), name="mlstm",
    )(proj, qv_t, qv_t, proj, gates, gates_t, norm_g)


def _merge_kernel(a_ref, f_ref, m_ref, wa_ref, wf_ref, wm_ref, ga_ref, gf_ref, gm_ref,
                  ba_ref, bf_ref, bm_ref, o_ref):
    def br(x_ref, w_ref, g_ref, b_ref):
        return jax.nn.sigmoid(g_ref[...].astype(F32) + b_ref[...]) * _dot(x_ref[...], w_ref[...])

    y = br(a_ref, wa_ref, ga_ref, ba_ref) + br(f_ref, wf_ref, gf_ref, bf_ref) + br(m_ref, wm_ref, gm_ref, bm_ref)
    o_ref[...] = y.astype(o_ref.dtype)


def merge(a, f, m, proj, wa, wf, wm, b_gate):
    R, kin = a.shape
    tm = _pick(R, (512, 256))
    tn = 1024
    x_spec = pl.BlockSpec((tm, kin), lambda j, i: (i, 0))
    w_spec = pl.BlockSpec((kin, tn), lambda j, i: (0, j))
    gp = lambda br: pl.BlockSpec((tm, tn), lambda j, i: (i, (C_GP + br * D) // tn + j))
    bg = lambda br: pl.BlockSpec((None, 1, tn), lambda j, i: (br, 0, j))
    return pl.pallas_call(
        _merge_kernel,
        out_shape=jax.ShapeDtypeStruct((R, D), BF16),
        grid=(D // tn, R // tm),
        in_specs=[x_spec, x_spec, x_spec, w_spec, w_spec, w_spec, gp(0), gp(1), gp(2), bg(0), bg(1), bg(2)],
        out_specs=pl.BlockSpec((tm, tn), lambda j, i: (i, j)),
        compiler_params=_cp(2, 40), name="merge",
    )(a, f, m, wa, wf, wm, proj, proj, proj, b_gate, b_gate, b_gate)


def _outproj_kernel(y_ref, w_ref, x_ref, g_ref, o_ref):
    o_ref[...] = x_ref[...] + g_ref[...] * _dot(y_ref[...], w_ref[...])


def out_proj_residual(y, w_out, xa, mod3, k_gate, nct):
    B, S, _ = xa.shape
    return pl.pallas_call(
        _outproj_kernel,
        out_shape=jax.ShapeDtypeStruct((B, S, D), F32),
        grid=(B, S // ROW_T),
        in_specs=[pl.BlockSpec((None, ROW_T, D), lambda b, j: (b, j, 0)),
                  pl.BlockSpec((D, D), lambda b, j: (0, 0)),
                  pl.BlockSpec((None, ROW_T, D), lambda b, j: (b, j, 0)),
                  _mod_spec(k_gate, nct, B)],
        out_specs=pl.BlockSpec((None, ROW_T, D), lambda b, j: (b, j, 0)),
        compiler_params=_cp(2, 40), name="out_proj",
    )(y, w_out, xa, mod3)


def _pack2(lo, hi):
    lo_b = lax.bitcast_convert_type(lo.astype(BF16).astype(F32), jnp.uint32)
    hi_b = lax.bitcast_convert_type(hi.astype(BF16).astype(F32), jnp.uint32)
    return (lo_b >> 16) | (hi_b & jnp.uint32(0xFFFF0000))


def _unpack2(w):
    lo = lax.bitcast_convert_type(w << 16, F32)
    hi = lax.bitcast_convert_type(w & jnp.uint32(0xFFFF0000), F32)
    return lo, hi


def _router_kernel(x_ref, g_ref, sc_ref, sh_ref, wh_ref, wl_ref, b_ref, hp_ref, eid_ref, wt_ref):
    x = x_ref[...]
    y = x * lax.rsqrt(jnp.mean(x * x, axis=-1, keepdims=True) + EPS) * g_ref[...]
    y = y * (1.0 + sc_ref[...]) + sh_ref[...]
    hp_ref[...] = _pack2(y[:, :D // 2], y[:, D // 2:])
    yh, yl = _split(y)
    logits = _dot(yh, wh_ref[...]) + _dot(yl, wh_ref[...]) + _dot(yh, wl_ref[...]) + b_ref[...]
    lane = lax.broadcasted_iota(jnp.int32, logits.shape, 1)
    lanef = lane.astype(F32)
    big = float(LANE)
    glog = jnp.where(lane < N_GRP, logits, NEG)
    gmax = jnp.max(glog, axis=-1, keepdims=True)
    gsel = jnp.min(jnp.where(glog == gmax, lanef, big), axis=-1, keepdims=True)
    pg = 1.0 / jnp.sum(jnp.exp(glog - gmax), axis=-1, keepdims=True)
    lo = N_GRP + EPG * gsel
    el = jnp.where(jnp.logical_and(lanef >= lo, lanef < lo + EPG), logits, NEG)
    v1 = jnp.max(el, axis=-1, keepdims=True)
    i1 = jnp.min(jnp.where(el == v1, lanef, big), axis=-1, keepdims=True)
    el2 = jnp.where(lanef == i1, NEG, el)
    v2 = jnp.max(el2, axis=-1, keepdims=True)
    i2 = jnp.min(jnp.where(el2 == v2, lanef, big), axis=-1, keepdims=True)
    e = jnp.exp(v2 - v1)
    w1 = pg / (1.0 + e)
    w2 = pg * e / (1.0 + e)
    eid_ref[...] = jnp.where(lane == 0, i1 - N_GRP, jnp.where(lane == 1, i2 - N_GRP, 0.0)).astype(jnp.int32)
    wt_ref[...] = jnp.where(lane == 0, w1, jnp.where(lane == 1, w2, 0.0))


def norm_router(xa, g, mod3, k_sh, k_sc, nct, wr_hi, wr_lo, br):
    B, S, _ = xa.shape
    tile = lambda w: pl.BlockSpec((None, ROW_T, w), lambda b, j: (b, j, 0))
    const = lambda shape: pl.BlockSpec(shape, lambda b, j: (0, 0))
    return pl.pallas_call(
        _router_kernel,
        out_shape=(jax.ShapeDtypeStruct((B, S, D // 2), jnp.uint32),
                   jax.ShapeDtypeStruct((B, S, LANE), jnp.int32),
                   jax.ShapeDtypeStruct((B, S, LANE), F32)),
        grid=(B, S // ROW_T),
        in_specs=[tile(D), const((1, D)), _mod_spec(k_sc, nct, B), _mod_spec(k_sh, nct, B),
                  const((D, LANE)), const((D, LANE)), const((1, LANE))],
        out_specs=(tile(D // 2), tile(LANE), tile(LANE)),
        compiler_params=_cp(2, 32), name="norm_router",
    )(xa, g.reshape(1, D), mod3, mod3, wr_hi, wr_lo, br)


DMA_UNROLL = 8


def _expert_kernel(tok_ref, order_ref, rstart_ref, nvalid_ref, be_ref, nexte_ref, nused_ref,
                   x_hbm, w1_hbm, w3_hbm, w2_hbm, y_hbm,
                   xbuf, ybuf, st1, st3, st2, wb1, wb3, wb2, gsem, ssem, wsem, *, layer):
    i = pl.program_id(0)
    nused = nused_ref[0]
    last = pl.num_programs(0) - 1
    used = i < nused
    weights = ((w1_hbm, st1, wb1), (w3_hbm, st3, wb3), (w2_hbm, st2, wb2))

    def gather_rows(blk):
        slot = blk & 1
        rs = rstart_ref[blk]

        def group(k, carry):
            for u in range(DMA_UNROLL):
                src = x_hbm.at[pl.ds(tok_ref[rs + k * DMA_UNROLL + u], 1), :]
                pltpu.make_async_copy(src, xbuf.at[slot, k, pl.ds(u, 1), :], gsem.at[slot]).start(priority=u % 2)
            return carry

        lax.fori_loop(0, MOE_T // DMA_UNROLL, group, 0)

    def scatter_rows(blk, wait):
        slot = blk & 1
        rs = rstart_ref[blk]
        nv = nvalid_ref[blk]

        def row(k, u, r, prio):
            dst = y_hbm.at[pl.ds(0 if wait else order_ref[rs + r], 1), :]
            cp = pltpu.make_async_copy(ybuf.at[slot, k, pl.ds(u, 1), :], dst, ssem.at[slot])
            if wait:
                cp.wait()
            else:
                cp.start(priority=prio)

        def group(k, carry):
            if wait:
                pltpu.make_async_copy(ybuf.at[slot, k], y_hbm.at[pl.ds(0, DMA_UNROLL), :], ssem.at[slot]).wait()
            else:
                for u in range(DMA_UNROLL):
                    row(k, u, k * DMA_UNROLL + u, u % 2)
            return carry

        def tail(r, carry):
            row(lax.shift_right_logical(r, 3), r & (DMA_UNROLL - 1), r, 0)
            return carry

        full = nv // DMA_UNROLL
        lax.fori_loop(0, full, group, 0)
        lax.fori_loop(full * DMA_UNROLL, nv, tail, 0)

    def fetch_weights(e):
        for k, (hbm, st, _) in enumerate(weights):
            pltpu.make_async_copy(hbm.at[layer, e], st, wsem.at[k]).start()

    @pl.when(i == 0)
    def _():
        fetch_weights(be_ref[0])
        gather_rows(0)

    @pl.when(i + 1 < nused)
    def _():
        gather_rows(i + 1)

    @pl.when(jnp.logical_and(i >= 2, i - 2 < nused))
    def _():
        scatter_rows(i - 2, True)

    @pl.when(used)
    def _():
        e = be_ref[i]

        @pl.when(jnp.logical_or(i == 0, be_ref[jnp.maximum(i - 1, 0)] != e))
        def _():
            for k, (hbm, st, wb) in enumerate(weights):
                pltpu.make_async_copy(hbm.at[layer, 0], st, wsem.at[k]).wait()
                wb[...] = st[...].astype(BF16)

            @pl.when(nexte_ref[i] >= 0)
            def _():
                fetch_weights(nexte_ref[i])

        slot = i & 1
        pltpu.make_async_copy(xbuf.at[slot], xbuf.at[slot], gsem.at[slot]).wait()
        xl, xh = _unpack2(xbuf[slot].reshape(MOE_T, D // 2))
        xl = xl.astype(BF16)
        xh = xh.astype(BF16)
        half = D // 2
        a = _dot(xl, wb1[:half, :]) + _dot(xh, wb1[half:, :])
        b = _dot(xl, wb3[:half, :]) + _dot(xh, wb3[half:, :])
        hmid = (a * jax.nn.sigmoid(a) * b).astype(BF16)
        y = _dot(hmid, wb2[...])
        ybuf[slot] = _pack2(y[:, :half], y[:, half:]).reshape(MOE_T // DMA_UNROLL, DMA_UNROLL, half)
        scatter_rows(i, False)

    @pl.when(i == last)
    def _():
        @pl.when(jnp.logical_and(i >= 1, i - 1 < nused))
        def _():
            scatter_rows(i - 1, True)

        @pl.when(used)
        def _():
            scatter_rows(i, True)


def expert_ffn(hp, plan, w1, w3, w2, layer, n_blocks, n_pairs):
    w = hp.shape[1]
    hbm = pl.BlockSpec(memory_space=pl.ANY)
    grid_spec = pltpu.PrefetchScalarGridSpec(
        num_scalar_prefetch=7, grid=(n_blocks,),
        in_specs=[hbm, hbm, hbm, hbm],
        out_specs=hbm,
        scratch_shapes=[pltpu.VMEM((2, MOE_T // DMA_UNROLL, DMA_UNROLL, w), jnp.uint32),
                        pltpu.VMEM((2, MOE_T // DMA_UNROLL, DMA_UNROLL, w), jnp.uint32),
                        pltpu.VMEM((D, FF), F32), pltpu.VMEM((D, FF), F32), pltpu.VMEM((FF, D), F32),
                        pltpu.VMEM((D, FF), BF16), pltpu.VMEM((D, FF), BF16), pltpu.VMEM((FF, D), BF16),
                        pltpu.SemaphoreType.DMA((2,)), pltpu.SemaphoreType.DMA((2,)),
                        pltpu.SemaphoreType.DMA((3,))])
    return pl.pallas_call(
        functools.partial(_expert_kernel, layer=layer),
        out_shape=jax.ShapeDtypeStruct((n_pairs, w), jnp.uint32),
        grid_spec=grid_spec, compiler_params=_cp(1, 56), name="moe_experts",
    )(*plan, hp, w1, w3, w2)


def _combine_kernel(ya_ref, yb_ref, x_ref, wt_ref, g_ref, o_ref):
    w0 = wt_ref[:, 0:1]
    w1 = wt_ref[:, 1:2]
    al, ah = _unpack2(ya_ref[...])
    bl, bh = _unpack2(yb_ref[...])
    half = D // 2
    g = g_ref[...]
    o_ref[:, :half] = x_ref[:, :half] + g[:, :half] * (w0 * al + w1 * bl)
    o_ref[:, half:] = x_ref[:, half:] + g[:, half:] * (w0 * ah + w1 * bh)


def combine(y2, xa, wts, mod3, k_gate, nct, latent_only):
    B, S, _ = xa.shape
    w = y2.shape[-1]
    off = nct if latent_only else 0
    tile = lambda wd: pl.BlockSpec((None, ROW_T, wd), lambda b, j: (b, j + off, 0))
    pick = lambda k: pl.BlockSpec((None, None, ROW_T, w), lambda b, j: (k, b, j + off, 0))
    return pl.pallas_call(
        _combine_kernel,
        out_shape=jax.ShapeDtypeStruct((B, S - off * ROW_T, D), F32),
        grid=(B, S // ROW_T - off),
        in_specs=[pick(0), pick(1), tile(D), tile(LANE), _mod_spec(k_gate, nct, B, off)],
        out_specs=pl.BlockSpec((None, ROW_T, D), lambda b, j: (b, j, 0)),
        compiler_params=_cp(2, 32), name="moe_combine",
    )(y2, y2, xa, wts, mod3)


def moe_plan(eid, n_blocks, n_tok):
    order = jnp.argsort(eid).astype(jnp.int32)
    tok = jnp.pad(jnp.where(order >= n_tok, order - n_tok, order), (0, MOE_T))
    experts = jnp.arange(N_EXP, dtype=jnp.int32)
    counts = jnp.sum((eid[:, None] == experts[None, :]).astype(jnp.int32), axis=0)
    nblk_e = (counts + MOE_T - 1) // MOE_T
    bend = jnp.cumsum(nblk_e)
    bstart = bend - nblk_e
    start = jnp.cumsum(counts) - counts
    blk = jnp.arange(n_blocks, dtype=jnp.int32)
    owner = jnp.logical_and(blk[:, None] >= bstart[None, :], blk[:, None] < bend[None, :]).astype(jnp.int32)
    take = lambda v: jnp.sum(owner * v[None, :], axis=1)
    within = blk - take(bstart)
    nused = bend[-1]
    last_e = jnp.max(jnp.where(counts > 0, experts, 0))
    block_e = jnp.where(blk < nused, take(experts), last_e).astype(jnp.int32)
    rstart = jnp.where(blk < nused, take(start) + within * MOE_T, 0).astype(jnp.int32)
    nvalid = jnp.where(blk < nused, jnp.clip(take(counts) - within * MOE_T, 0, MOE_T), 0).astype(jnp.int32)
    nxt = take(bend)
    e_at_nxt = jnp.sum((blk[None, :] == nxt[:, None]).astype(jnp.int32) * block_e[None, :], axis=1)
    next_e = jnp.where(jnp.logical_and(blk < nused, nxt < nused), e_at_nxt, -1).astype(jnp.int32)
    return tok, order, rstart, nvalid, block_e, next_e, nused.astype(jnp.int32).reshape(1)


def _rope_tables(s, lc):
    t = np.arange(s - lc)
    row = (t // GRID_W).astype(np.float64)
    col = (t % GRID_W).astype(np.float64)
    nf = HD // 4
    inv = np.float32(ROPE_BASE) ** (-np.arange(nf, dtype=np.float32) / np.float32(nf))
    ang = np.concatenate([row[:, None] * inv, col[:, None] * inv], axis=-1)
    ang = np.concatenate([np.zeros((lc, HD // 2)), ang], axis=0)
    cos = np.repeat(np.cos(ang), 2, axis=-1)
    sin = np.repeat(np.sin(ang), 2, axis=-1) * np.tile([-1.0, 1.0], HD // 2)
    return jnp.asarray(np.tile(cos, (1, LANE // HD)), F32), jnp.asarray(np.tile(sin, (1, LANE // HD)), F32)


def _chan_dft():
    j = np.arange(FGD)
    ang = 2.0 * np.pi * ((j[:, None] * j[None, :]) % FGD) / FGD
    c = np.kron(np.eye(FG), np.cos(ang)) / np.sqrt(FGD)
    s = np.kron(np.eye(FG), np.sin(ang)) / np.sqrt(FGD)
    return jnp.asarray(np.concatenate([c, -s], axis=1), BF16)


def _pos_dft(n):
    if n <= 512:
        k = np.arange(n)
        ang = 2.0 * np.pi * ((k[:, None] * k[None, :]) % n) / n
        return (jnp.asarray(np.cos(ang) / np.sqrt(n), BF16), jnp.asarray(np.sin(ang) / np.sqrt(n), BF16))
    nr = n // LANE
    k = np.arange(n)
    a = 2.0 * np.pi * ((k[:, None] * np.arange(nr)[None, :]) % nr) / nr
    b = 2.0 * np.pi * (k[:, None] * np.arange(LANE)[None, :]) / n
    sc = 1.0 / np.sqrt(n)
    ca, sa = jnp.asarray(np.cos(a) * sc, F32)[:, :, None], jnp.asarray(np.sin(a) * sc, F32)[:, :, None]
    cb, sb = jnp.asarray(np.cos(b), F32)[:, None, :], jnp.asarray(np.sin(b), F32)[:, None, :]
    ct = (ca * cb - sa * sb).reshape(n, n).astype(BF16)
    st = (sa * cb + ca * sb).reshape(n, n).astype(BF16)
    return ct, st


def _proj_weights(w_in):
    dup = lambda w: jnp.concatenate([w.reshape(D, N_KV, HD)] * 2, axis=-1).reshape(D, 2 * N_KV * HD)
    w_big = jnp.concatenate([w_in[:, 0:1024], dup(w_in[:, 1024:1280]), dup(w_in[:, 1280:1536]),
                             w_in[:, 4640:], w_in[:, 2048:2560], w_in[:, 3584:4608]], axis=1).astype(BF16)
    w_gate = jnp.pad(w_in[:, 4608:4640], ((0, 0), (0, LANE - 32))).astype(BF16)
    w_qv_t = jnp.concatenate([w_in[:, 1536:2048] * (ML_DK ** -0.5), w_in[:, 2560:3584]], axis=1).T.astype(BF16)
    return w_big, w_gate, w_qv_t


def _qk_gain(qg, kg):
    qs = jnp.tile(qg * (HD ** -0.5 * LOG2E), N_HEADS)
    return jnp.concatenate([qs, jnp.tile(kg, 2 * N_KV)]).reshape(1, -1).astype(F32)


def kernel(x, c, ctx, c_ctx, w_mod, b_mod, norm1_g, norm2_g, w_in, q_norm_g, k_norm_g, attn_sink, ml_gate_b,
           ml_norm_g, w_br_attn, w_br_four, w_br_mlstm, b_gate, w_out, w_grp, b_grp, w_exp_router,
           b_exp_router, w1, w3, w2):
    B, T, _ = x.shape
    lc = ctx.shape[1]
    S = lc + T
    R = B * S
    depth = w_mod.shape[0]
    nct = lc // ROW_T
    assert lc % ROW_T == 0 and T % ROW_T == 0 and B < 8

    xa = jnp.concatenate([ctx, x], axis=1)
    cvec = jnp.concatenate([c, c_ctx[None], jnp.zeros((7 - B, D), F32)], axis=0)
    cos_t, sin_t = _rope_tables(S, lc)
    bd = jnp.asarray(np.kron(np.eye(LANE // HD), np.ones((HD, HD))), BF16)
    wc = _chan_dft()
    ct_lat, st_lat = _pos_dft(T)
    ct_ctx, st_ctx = _pos_dft(lc)
    nchunk = S // ML_L
    n_pairs = R * 2
    n_blocks = -(-n_pairs // MOE_T) + N_EXP

    for l in range(depth):
        mod = modulation(cvec, w_mod, b_mod, l)
        mod3 = mod.reshape(8, 1, 6 * D)
        h = norm_mod(xa, norm1_g[l], mod3, 0, 1, nct)
        w_big, w_gate, w_qv_t = _proj_weights(w_in[l])
        h2d = h.reshape(R, D)
        proj = mm(h2d, w_big, BF16, 1536).reshape(B, S, C_TOT)
        graw = mm(h2d, w_gate, F32, LANE).reshape(B, S, LANE)

        qkn = qk_prep(proj, _qk_gain(q_norm_g[l], k_norm_g[l]), cos_t, sin_t, bd)
        a = attention(qkn, proj, attn_sink[l].astype(F32) * LOG2E, lc)

        f_ctx = dft_positions(ct_ctx, st_ctx, dft_channels(proj, wc, 0, lc))
        f_lat = dft_positions(ct_lat, st_lat, dft_channels(proj, wc, lc, T))
        f = jnp.concatenate([f_ctx, f_lat], axis=1)

        gbias = jnp.pad(ml_gate_b[l].reshape(1, 4 * ML_H).astype(F32), ((0, 0), (0, LANE - 4 * ML_H)))
        gproc = gate_prep(graw, gbias)[..., :4 * ML_H]
        g5 = gproc.reshape(B, S, 2, 2, ML_H // 2, 2).transpose(0, 4, 1, 2, 3, 5).reshape(B, ML_H // 2, S, 8)
        gates = jnp.pad(g5, ((0, 0), (0, 0), (0, 0), (0, LANE - 8)))
        gates_t = g5.reshape(B, ML_H // 2, nchunk, ML_L, 8).transpose(0, 1, 2, 4, 3)
        qv_t = mm_chunk_transposed(h, w_qv_t, ML_L)
        m = mlstm(proj, qv_t, gates, gates_t, ml_norm_g[l].reshape(1, -1).astype(F32), lc)

        y = merge(a.reshape(R, -1), f.reshape(R, -1), m.reshape(R, -1), proj.reshape(R, C_TOT),
                  w_br_attn[l].astype(BF16), w_br_four[l].astype(BF16), w_br_mlstm[l].astype(BF16),
                  b_gate[l].reshape(3, 1, D).astype(F32))
        xa = out_proj_residual(y.reshape(B, S, D), w_out[l].astype(BF16), xa, mod3, 2, nct)

        wr = jnp.concatenate([w_grp[l], w_exp_router[l].reshape(D, N_EXP)], axis=1)
        wr = jnp.pad(wr, ((0, 0), (0, LANE - N_GRP - N_EXP))).astype(F32)
        wr_hi = wr.astype(BF16)
        wr_lo = (wr - wr_hi.astype(F32)).astype(BF16)
        br = jnp.pad(jnp.concatenate([b_grp[l], b_exp_router[l].reshape(N_EXP)]),
                     (0, LANE - N_GRP - N_EXP)).reshape(1, LANE).astype(F32)
        hp, eid, wts = norm_router(xa, norm2_g[l], mod3, 3, 4, nct, wr_hi, wr_lo, br)
        plan = moe_plan(jnp.concatenate([eid[..., 0].reshape(R), eid[..., 1].reshape(R)]), n_blocks, R)
        y2 = expert_ffn(hp.reshape(R, D // 2), plan, w1, w3, w2, l, n_blocks, n_pairs)
        xa = combine(y2.reshape(2, B, S, D // 2), xa, wts, mod3, 5, nct, l == depth - 1)

    return xa
```

```python
import functools

import numpy as np
import jax
import jax.numpy as jnp
from jax import lax
from jax.experimental import pallas as pl
from jax.experimental.pallas import tpu as pltpu

F32 = jnp.float32
BF16 = jnp.bfloat16

D = 2048
N_HEADS, N_KV, HD = 16, 4, 64
GRID_W = 64
ROPE_BASE = 10000.0
EPS = 1e-6
ML_H, ML_DK, ML_DV = 8, 64, 128
ML_L = 256
FG, FGD = 4, 256
N_GRP, EPG, N_EXP, FF = 4, 8, 32, 1024
NEG = -1e30
LOG2E = 1.4426950408889634
LANE = 128
ROW_T = 256
MOE_T = 256

C_Q, C_K, C_V = 0, 1024, 1536
C_F, C_GP = 2048, 3072
C_MK, C_MO, C_TOT = 9216, 9728, 10752
QV_ROWS = ML_H * (ML_DK + ML_DV)


def _cp(n_axes, vmem_mb):
    return pltpu.CompilerParams(dimension_semantics=("arbitrary",) * n_axes,
                                vmem_limit_bytes=int(vmem_mb) * 2 ** 20)


def _pick(n, cands):
    for c in cands:
        if n % c == 0:
            return c
    raise ValueError(f"no tile for {n} in {cands}")


def _dot(a, b):
    return jnp.dot(a, b, preferred_element_type=F32)


def _dot_nt(a, b):
    return lax.dot_general(a, b, (((1,), (1,)), ((), ())), preferred_element_type=F32)


def _split(x):
    hi = x.astype(BF16)
    return hi, (x - hi.astype(F32)).astype(BF16)


def _dot3(a, w):
    ah, al = _split(a)
    wh, wl = _split(w)
    return _dot(ah, wh) + _dot(al, wh) + _dot(ah, wl)


def _mod_kernel(a_ref, w_ref, b_ref, o_ref):
    a = a_ref[...]
    a = a * jax.nn.sigmoid(a)
    o_ref[...] = _dot3(a, w_ref[...]) + b_ref[...]


def modulation(cvec, w_mod, b_mod, layer):
    depth, _, n = w_mod.shape
    tn = 512
    return pl.pallas_call(
        _mod_kernel,
        out_shape=jax.ShapeDtypeStruct((8, n), F32),
        grid=(n // tn,),
        in_specs=[pl.BlockSpec((8, D), lambda j: (0, 0)),
                  pl.BlockSpec((None, D, tn), lambda j: (layer, 0, j)),
                  pl.BlockSpec((None, 1, tn), lambda j: (layer, 0, j))],
        out_specs=pl.BlockSpec((8, tn), lambda j: (0, j)),
        compiler_params=_cp(1, 24), name="modulation",
    )(cvec, w_mod, b_mod.reshape(depth, 1, n))


def _mod_spec(k, nct, nb, off=0):
    return pl.BlockSpec((None, 1, D), lambda b, j: (jnp.where(j + off < nct, nb, b), 0, k))


def _norm_kernel(x_ref, g_ref, sc_ref, sh_ref, o_ref):
    x = x_ref[...]
    y = x * lax.rsqrt(jnp.mean(x * x, axis=-1, keepdims=True) + EPS) * g_ref[...]
    o_ref[...] = (y * (1.0 + sc_ref[...]) + sh_ref[...]).astype(o_ref.dtype)


def norm_mod(xa, g, mod3, k_sh, k_sc, nct):
    B, S, _ = xa.shape
    return pl.pallas_call(
        _norm_kernel,
        out_shape=jax.ShapeDtypeStruct((B, S, D), BF16),
        grid=(B, S // ROW_T),
        in_specs=[pl.BlockSpec((None, ROW_T, D), lambda b, j: (b, j, 0)),
                  pl.BlockSpec((1, D), lambda b, j: (0, 0)),
                  _mod_spec(k_sc, nct, B), _mod_spec(k_sh, nct, B)],
        out_specs=pl.BlockSpec((None, ROW_T, D), lambda b, j: (b, j, 0)),
        compiler_params=_cp(2, 24), name="norm_mod",
    )(xa, g.reshape(1, D), mod3, mod3)


def _mm_kernel(a_ref, b_ref, o_ref):
    o_ref[...] = _dot(a_ref[...], b_ref[...]).astype(o_ref.dtype)


def mm(a, b, out_dtype, tn):
    M, K = a.shape
    N = b.shape[1]
    tm = _pick(M, (1024, 512, 256))
    return pl.pallas_call(
        _mm_kernel,
        out_shape=jax.ShapeDtypeStruct((M, N), out_dtype),
        grid=(M // tm, N // tn),
        in_specs=[pl.BlockSpec((tm, K), lambda i, j: (i, 0)),
                  pl.BlockSpec((K, tn), lambda i, j: (0, j))],
        out_specs=pl.BlockSpec((tm, tn), lambda i, j: (i, j)),
        compiler_params=_cp(2, 40), name="mm",
    )(a, b)


def _mm_t_kernel(w_ref, h_ref, o_ref):
    o_ref[...] = _dot_nt(w_ref[...], h_ref[...]).astype(o_ref.dtype)


def mm_chunk_transposed(h, w_t, chunk):
    B, S, K = h.shape
    n = w_t.shape[0]
    return pl.pallas_call(
        _mm_t_kernel,
        out_shape=jax.ShapeDtypeStruct((B, S // chunk, n, chunk), BF16),
        grid=(B, S // chunk),
        in_specs=[pl.BlockSpec((n, K), lambda b, c: (0, 0)),
                  pl.BlockSpec((None, chunk, K), lambda b, c: (b, c, 0))],
        out_specs=pl.BlockSpec((None, None, n, chunk), lambda b, c: (b, c, 0, 0)),
        compiler_params=_cp(2, 40), name="mm_t",
    )(w_t, h)


def _prep_kernel(x_ref, g_ref, cos_ref, sin_ref, bd_ref, o_ref):
    cosv = cos_ref[...]
    sinv = sin_ref[...]
    bd = bd_ref[...]
    lane = lax.broadcasted_iota(jnp.int32, cosv.shape, 1)
    even = (lane & 1) == 0
    for s in range((C_V - C_Q) // LANE):
        sl = slice(s * LANE, (s + 1) * LANE)
        x = x_ref[:, sl].astype(F32)
        hi, lo = _split(x * x)
        ss = _dot(hi, bd) + _dot(lo, bd)
        y = x * lax.rsqrt(ss * (1.0 / HD) + EPS) * g_ref[:, sl]
        sw = jnp.where(even, pltpu.roll(y, LANE - 1, 1), pltpu.roll(y, 1, 1))
        o_ref[:, sl] = (y * cosv + sw * sinv).astype(o_ref.dtype)


def qk_prep(proj, gain, cos_t, sin_t, bd):
    B, S, _ = proj.shape
    w = C_V - C_Q
    return pl.pallas_call(
        _prep_kernel,
        out_shape=jax.ShapeDtypeStruct((B, S, w), BF16),
        grid=(B, S // ROW_T),
        in_specs=[pl.BlockSpec((None, ROW_T, w), lambda b, j: (b, j, 0)),
                  pl.BlockSpec((1, w), lambda b, j: (0, 0)),
                  pl.BlockSpec((ROW_T, LANE), lambda b, j: (j, 0)),
                  pl.BlockSpec((ROW_T, LANE), lambda b, j: (j, 0)),
                  pl.BlockSpec((LANE, LANE), lambda b, j: (0, 0))],
        out_specs=pl.BlockSpec((None, ROW_T, w), lambda b, j: (b, j, 0)),
        compiler_params=_cp(2, 24), name="qk_prep",
    )(proj, gain, cos_t, sin_t, bd)


AB = 128


def _attn_kernel(sink_ref, q_ref, kp_ref, kc_ref, kn_ref, kx_ref, vp_ref, vc_ref, vn_ref, vx_ref, o_ref,
                 *, nctb, nblk):
    n = pl.program_id(1)
    is_lat = n >= nctb
    iq = lax.broadcasted_iota(jnp.int32, (AB, AB), 0)
    ik = lax.broadcasted_iota(jnp.int32, (AB, AB), 1)
    ok_p = jnp.logical_and(ik >= iq, jnp.logical_and(is_lat, n - 1 >= nctb))
    ok_c = jnp.logical_and(ik >= 0, is_lat)
    ok_n = jnp.logical_and(ik <= iq, jnp.logical_and(is_lat, n + 1 <= nblk - 1))
    nloc = 3 * AB
    bias = jnp.concatenate([jnp.where(ok_p, 0.0, NEG), jnp.where(ok_c, 0.0, NEG), jnp.where(ok_n, 0.0, NEG)], axis=1)
    kall = jnp.concatenate([kp_ref[...], kc_ref[...], kn_ref[...], kx_ref[...]], axis=0)
    vall = jnp.concatenate([vp_ref[...], vc_ref[...], vn_ref[...], vx_ref[...]], axis=0)
    ones = jnp.ones((kall.shape[0], LANE), BF16)
    lane = lax.broadcasted_iota(jnp.int32, (AB, LANE), 1)
    lo_half = lane < HD
    hi_half = jnp.logical_not(lo_half)
    gq = N_HEADS // N_KV
    for g in range(N_KV):
        kg = kall[:, g * LANE:(g + 1) * LANE]
        vext = jnp.concatenate([vall[:, g * LANE:(g + 1) * LANE], ones], axis=1)
        qparts = []
        for jj in range(2):
            qs = q_ref[:, (2 * g + jj) * LANE:(2 * g + jj + 1) * LANE]
            qparts += [jnp.where(lo_half, qs, jnp.zeros_like(qs)), jnp.where(hi_half, qs, jnp.zeros_like(qs))]
        s = _dot_nt(jnp.concatenate(qparts, axis=0), kg)
        ps, sink_w = [], []
        for seg in range(gq):
            sl = s[seg * AB:(seg + 1) * AB]
            s_loc = sl[:, :nloc] + bias
            s_ctx = sl[:, nloc:]
            sk = sink_ref[gq * g + seg]
            m = jnp.maximum(jnp.maximum(jnp.max(s_loc, axis=-1, keepdims=True),
                                        jnp.max(s_ctx, axis=-1, keepdims=True)), sk)
            ps.append(jnp.concatenate([jnp.exp2(s_loc - m), jnp.exp2(s_ctx - m)], axis=1).astype(BF16))
            sink_w.append(jnp.exp2(sk - m))
        o = _dot(jnp.concatenate(ps, axis=0), vext)
        for jj in range(2):
            outs = []
            for half in range(2):
                seg = 2 * jj + half
                rows = slice(seg * AB, (seg + 1) * AB)
                outs.append(o[rows, :LANE] / (o[rows, LANE:] + sink_w[seg]))
            sl_out = slice((2 * g + jj) * LANE, (2 * g + jj + 1) * LANE)
            o_ref[:, sl_out] = jnp.where(lo_half, outs[0], outs[1]).astype(o_ref.dtype)


def attention(qkn, proj, sink, lc):
    B, S, _ = qkn.shape
    nblk = S // AB
    nctb = lc // AB
    kw = C_V - C_K
    kcol, vcol = C_K // kw, C_V // kw

    def rows(fn, col):
        return pl.BlockSpec((None, AB, kw), lambda b, n: (b, fn(n), col))

    prev = lambda n: jnp.maximum(n - 1, 0)
    cur = lambda n: n
    nxt = lambda n: jnp.minimum(n + 1, nblk - 1)
    ctx = lambda col: pl.BlockSpec((None, lc, kw), lambda b, n: (b, 0, col))
    return pl.pallas_call(
        functools.partial(_attn_kernel, nctb=nctb, nblk=nblk),
        out_shape=jax.ShapeDtypeStruct((B, S, C_K), BF16),
        grid=(B, nblk),
        in_specs=[pl.BlockSpec(memory_space=pltpu.SMEM),
                  pl.BlockSpec((None, AB, C_K), lambda b, n: (b, n, 0)),
                  rows(prev, kcol), rows(cur, kcol), rows(nxt, kcol), ctx(kcol),
                  rows(prev, vcol), rows(cur, vcol), rows(nxt, vcol), ctx(vcol)],
        out_specs=pl.BlockSpec((None, AB, C_K), lambda b, n: (b, n, 0)),
        compiler_params=_cp(2, 32), name="attention",
    )(sink, qkn, qkn, qkn, qkn, qkn, proj, proj, proj, proj)


def _dftc_kernel(u_ref, w_ref, o_ref):
    o_ref[...] = _dot(u_ref[...], w_ref[...]).astype(o_ref.dtype)


def dft_channels(proj, wc, row0, nrows):
    B = proj.shape[0]
    off = row0 // ROW_T
    return pl.pallas_call(
        _dftc_kernel,
        out_shape=jax.ShapeDtypeStruct((B, nrows, 2 * FG * FGD), BF16),
        grid=(B, nrows // ROW_T),
        in_specs=[pl.BlockSpec((None, ROW_T, FG * FGD), lambda b, j: (b, j + off, C_F // (FG * FGD))),
                  pl.BlockSpec((FG * FGD, 2 * FG * FGD), lambda b, j: (0, 0))],
        out_specs=pl.BlockSpec((None, ROW_T, 2 * FG * FGD), lambda b, j: (b, j, 0)),
        compiler_params=_cp(2, 32), name="dft_channels",
    )(proj, wc)


def _dftp_kernel(c_ref, s_ref, y1_ref, y2_ref, o_ref, acc_ref):
    k = pl.program_id(2)

    @pl.when(k == 0)
    def _():
        acc_ref[...] = jnp.zeros_like(acc_ref)

    acc_ref[...] += _dot(c_ref[...], y1_ref[...]) + _dot(s_ref[...], y2_ref[...])

    @pl.when(k == pl.num_programs(2) - 1)
    def _():
        o_ref[...] = acc_ref[...].astype(o_ref.dtype)


def dft_positions(ct, st, y):
    B, n, w2 = y.shape
    w = w2 // 2
    t = _pick(n, (1024, 512, 256))
    return pl.pallas_call(
        _dftp_kernel,
        out_shape=jax.ShapeDtypeStruct((B, n, w), BF16),
        grid=(B, n // t, n // t),
        in_specs=[pl.BlockSpec((t, t), lambda b, i, k: (i, k)),
                  pl.BlockSpec((t, t), lambda b, i, k: (i, k)),
                  pl.BlockSpec((None, t, w), lambda b, i, k: (b, k, 0)),
                  pl.BlockSpec((None, t, w), lambda b, i, k: (b, k, 1))],
        out_specs=pl.BlockSpec((None, t, w), lambda b, i, k: (b, i, 0)),
        scratch_shapes=[pltpu.VMEM((t, w), F32)],
        compiler_params=_cp(3, 40), name="dft_positions",
    )(ct, st, y, y)


def _log_sigmoid(x):
    return jnp.minimum(x, 0.0) - jnp.log(1.0 + jnp.exp(-jnp.abs(x)))


def _gate_kernel(g_ref, b_ref, o_ref):
    L = ML_L
    g = g_ref[...] + b_ref[...]
    lane = lax.broadcasted_iota(jnp.int32, (L, LANE), 1)
    is_f = (lane & ML_H) == ML_H
    is_bwd = (lane & (2 * ML_H)) == 2 * ML_H
    lf = jnp.where(is_f, _log_sigmoid(g), 0.0)
    it_r = lax.broadcasted_iota(jnp.int32, (L, L), 0)
    it_c = lax.broadcasted_iota(jnp.int32, (L, L), 1)
    hp = lax.Precision.HIGHEST
    cf = jnp.dot(jnp.where(it_c <= it_r, 1.0, 0.0).astype(F32), lf, precision=hp, preferred_element_type=F32)
    cb = jnp.dot(jnp.where(it_c >= it_r, 1.0, 0.0).astype(F32), lf, precision=hp, preferred_element_type=F32)
    o_ref[...] = jnp.where(is_f, jnp.where(is_bwd, cb, cf), g)


def gate_prep(graw, bias):
    B, S, _ = graw.shape
    return pl.pallas_call(
        _gate_kernel,
        out_shape=jax.ShapeDtypeStruct((B, S, LANE), F32),
        grid=(B, S // ML_L),
        in_specs=[pl.BlockSpec((None, ML_L, LANE), lambda b, c: (b, c, 0)),
                  pl.BlockSpec((1, LANE), lambda b, c: (0, 0))],
        out_specs=pl.BlockSpec((None, ML_L, LANE), lambda b, c: (b, c, 0)),
        compiler_params=_cp(2, 16), name="gate_prep",
    )(graw, bias)


def _mlstm_kernel(k_ref, qt_ref, vt_ref, mo_ref, g_ref, gt_ref, ng_ref, o_ref,
                  hf_ref, hb_ref, c_ref, m_ref, *, nctc, nchunk):
    L = ML_L
    c_ref[...] = jnp.zeros_like(c_ref)
    m_ref[...] = jnp.zeros_like(m_ref)
    key_i = lax.broadcasted_iota(jnp.int32, (L, L), 0)
    qry_i = lax.broadcasted_iota(jnp.int32, (L, L), 1)
    tri = (key_i <= qry_i, key_i >= qry_i)
    lane = lax.broadcasted_iota(jnp.int32, (L, LANE), 1)
    head_lanes = (lane < ML_DK, lane >= ML_DK)
    ones_t = jnp.ones((ML_DV, L), BF16)
    h_refs = (hf_ref, hb_ref)

    def step(it, carry):
        stores = []
        for d in range(2):
            if d == 0:
                c = it
            else:
                c = jnp.where(it < nctc, nctc - 1 - it, nchunk - 1 - (it - nctc))
            r0 = pl.multiple_of(c * L, L)
            gc = g_ref[pl.ds(r0, L), :]
            gr = gt_ref[c]
            kp = k_ref[pl.ds(r0, L), :]
            qt = qt_ref[c]
            last = L - 1 if d == 0 else 0
            for hh in range(2):
                icol = d * 4 + hh
                fcol = d * 4 + 2 + hh
                sidx = d * 2 + hh
                key_term = gc[:, icol:icol + 1] - gc[:, fcol:fcol + 1]
                b_row = gr[fcol:fcol + 1, :]
                i_row = gr[icol:icol + 1, :]
                m_prev = m_ref[sidx:sidx + 1, 0:1]
                logd = jnp.where(tri[d], key_term + b_row, NEG)
                inter = b_row + m_prev
                m_t = jnp.maximum(inter, jnp.max(logd, axis=0, keepdims=True))
                km = jnp.where(head_lanes[hh], kp, jnp.zeros_like(kp))
                p_t = (_dot(km, qt) * jnp.exp(logd - m_t)).astype(BF16)
                q_in = (qt.astype(F32) * jnp.exp(inter - m_t)).astype(BF16)
                vext_t = jnp.concatenate([vt_ref[c, hh * ML_DV:(hh + 1) * ML_DV, :], ones_t], axis=0)
                c_t = c_ref[sidx]
                tot = _dot(jnp.concatenate([vext_t, c_t.astype(BF16)], axis=1),
                           jnp.concatenate([p_t, q_in], axis=0))
                h_t = tot[:ML_DV] / jnp.maximum(jnp.abs(tot[ML_DV:]), jnp.exp(-m_t))
                b_last = b_row[:, last:last + 1]
                log_w = b_last - b_row + i_row
                m_new = jnp.maximum(b_last + m_prev, jnp.max(log_w, axis=-1, keepdims=True))
                w_row = jnp.exp(log_w - m_new).astype(BF16)
                c_new = jnp.exp(b_last + m_prev - m_new) * c_t + _dot(vext_t * w_row, km)
                stores.append((d, hh, sidx, c, h_t, c_new, m_new))
        for d, hh, sidx, c, h_t, c_new, m_new in stores:
            h_refs[d][c, hh] = h_t
            c_ref[sidx] = c_new
            m_ref[sidx:sidx + 1, :] = jnp.broadcast_to(m_new, (1, LANE))
        return carry

    lax.fori_loop(0, nchunk, step, 0)

    def out_step(i, carry):
        r0 = pl.multiple_of(i * L, L)
        for hh in range(2):
            hs = slice(hh * ML_DV, (hh + 1) * ML_DV)
            h = jnp.transpose(hf_ref[i, hh] + hb_ref[i, hh])
            hn = h * lax.rsqrt(jnp.mean(h * h, axis=-1, keepdims=True) + EPS) * ng_ref[:, hs]
            og = jax.nn.sigmoid(mo_ref[pl.ds(r0, L), hs].astype(F32))
            o_ref[pl.ds(r0, L), hs] = (og * hn).astype(o_ref.dtype)
        return carry

    lax.fori_loop(0, nchunk, out_step, 0)


def mlstm(proj, qv_t, gates, gates_t, norm_g, lc):
    B, S, _ = proj.shape
    nchunk = S // ML_L
    pw = 2 * ML_DK
    vw = 2 * ML_DV
    return pl.pallas_call(
        functools.partial(_mlstm_kernel, nctc=lc // ML_L, nchunk=nchunk),
        out_shape=jax.ShapeDtypeStruct((B, S, ML_H * ML_DV), BF16),
        grid=(B, ML_H // 2),
        in_specs=[pl.BlockSpec((None, S, pw), lambda b, p: (b, 0, C_MK // pw + p)),
                  pl.BlockSpec((None, nchunk, pw, ML_L), lambda b, p: (b, 0, p, 0)),
                  pl.BlockSpec((None, nchunk, vw, ML_L), lambda b, p: (b, 0, ML_H * ML_DK // vw + p, 0)),
                  pl.BlockSpec((None, S, vw), lambda b, p: (b, 0, C_MO // vw + p)),
                  pl.BlockSpec((None, None, S, LANE), lambda b, p: (b, p, 0, 0)),
                  pl.BlockSpec((None, None, nchunk, 8, ML_L), lambda b, p: (b, p, 0, 0, 0)),
                  pl.BlockSpec((1, vw), lambda b, p: (0, p))],
        out_specs=pl.BlockSpec((None, S, vw), lambda b, p: (b, 0, p)),
        scratch_shapes=[pltpu.VMEM((nchunk, 2, ML_DV, ML_L), F32), pltpu.VMEM((nchunk, 2, ML_DV, ML_L), F32),
                        pltpu.VMEM((4, 2 * ML_DV, 2 * ML_DK), F32), pltpu.VMEM((8, LANE), F32)],
        compiler_params=_cp(2, 48), name="mlstm",
    )(proj, qv_t, qv_t, proj, gates, gates_t, norm_g)


def _merge_kernel(a_ref, f_ref, m_ref, wa_ref, wf_ref, wm_ref, ga_ref, gf_ref, gm_ref,
                  ba_ref, bf_ref, bm_ref, o_ref):
    def br(x_ref, w_ref, g_ref, b_ref):
        return jax.nn.sigmoid(g_ref[...].astype(F32) + b_ref[...]) * _dot(x_ref[...], w_ref[...])

    y = br(a_ref, wa_ref, ga_ref, ba_ref) + br(f_ref, wf_ref, gf_ref, bf_ref) + br(m_ref, wm_ref, gm_ref, bm_ref)
    o_ref[...] = y.astype(o_ref.dtype)


def merge(a, f, m, proj, wa, wf, wm, b_gate):
    R, kin = a.shape
    tm = _pick(R, (512, 256))
    tn = 1024
    x_spec = pl.BlockSpec((tm, kin), lambda j, i: (i, 0))
    w_spec = pl.BlockSpec((kin, tn), lambda j, i: (0, j))
    gp = lambda br: pl.BlockSpec((tm, tn), lambda j, i: (i, (C_GP + br * D) // tn + j))
    bg = lambda br: pl.BlockSpec((None, 1, tn), lambda j, i: (br, 0, j))
    return pl.pallas_call(
        _merge_kernel,
        out_shape=jax.ShapeDtypeStruct((R, D), BF16),
        grid=(D // tn, R // tm),
        in_specs=[x_spec, x_spec, x_spec, w_spec, w_spec, w_spec, gp(0), gp(1), gp(2), bg(0), bg(1), bg(2)],
        out_specs=pl.BlockSpec((tm, tn), lambda j, i: (i, j)),
        compiler_params=_cp(2, 40), name="merge",
    )(a, f, m, wa, wf, wm, proj, proj, proj, b_gate, b_gate, b_gate)


def _pack2(lo, hi):
    lo_b = lax.bitcast_convert_type(lo.astype(BF16).astype(F32), jnp.uint32)
    hi_b = lax.bitcast_convert_type(hi.astype(BF16).astype(F32), jnp.uint32)
    return (lo_b >> 16) | (hi_b & jnp.uint32(0xFFFF0000))


def _unpack2(w):
    lo = lax.bitcast_convert_type(w << 16, F32)
    hi = lax.bitcast_convert_type(w & jnp.uint32(0xFFFF0000), F32)
    return lo, hi


def _outproj_router_kernel(y_ref, w_ref, x_ref, g1_ref, g_ref, sc_ref, sh_ref, wh_ref, wl_ref, b_ref,
                           xo_ref, hp_ref, eid_ref, wt_ref):
    x = x_ref[...] + g1_ref[...] * _dot(y_ref[...], w_ref[...])
    xo_ref[...] = x
    y = x * lax.rsqrt(jnp.mean(x * x, axis=-1, keepdims=True) + EPS) * g_ref[...]
    y = y * (1.0 + sc_ref[...]) + sh_ref[...]
    hp_ref[...] = _pack2(y[:, :D // 2], y[:, D // 2:])
    yh, yl = _split(y)
    logits = _dot(yh, wh_ref[...]) + _dot(yl, wh_ref[...]) + _dot(yh, wl_ref[...]) + b_ref[...]
    lane = lax.broadcasted_iota(jnp.int32, logits.shape, 1)
    lanef = lane.astype(F32)
    big = float(LANE)
    glog = jnp.where(lane < N_GRP, logits, NEG)
    gmax = jnp.max(glog, axis=-1, keepdims=True)
    gsel = jnp.min(jnp.where(glog == gmax, lanef, big), axis=-1, keepdims=True)
    pg = 1.0 / jnp.sum(jnp.exp(glog - gmax), axis=-1, keepdims=True)
    lo = N_GRP + EPG * gsel
    el = jnp.where(jnp.logical_and(lanef >= lo, lanef < lo + EPG), logits, NEG)
    v1 = jnp.max(el, axis=-1, keepdims=True)
    i1 = jnp.min(jnp.where(el == v1, lanef, big), axis=-1, keepdims=True)
    el2 = jnp.where(lanef == i1, NEG, el)
    v2 = jnp.max(el2, axis=-1, keepdims=True)
    i2 = jnp.min(jnp.where(el2 == v2, lanef, big), axis=-1, keepdims=True)
    e = jnp.exp(v2 - v1)
    w1 = pg / (1.0 + e)
    w2 = pg * e / (1.0 + e)
    eid_ref[...] = jnp.where(lane == 0, i1 - N_GRP, jnp.where(lane == 1, i2 - N_GRP, 0.0)).astype(jnp.int32)
    wt_ref[...] = jnp.where(lane == 0, w1, jnp.where(lane == 1, w2, 0.0))


def out_proj_router(y, w_out, xa, g, mod3, k_gate, k_sh, k_sc, nct, wr_hi, wr_lo, br):
    B, S, _ = xa.shape
    tile = lambda w: pl.BlockSpec((None, ROW_T, w), lambda b, j: (b, j, 0))
    const = lambda shape: pl.BlockSpec(shape, lambda b, j: (0, 0))
    return pl.pallas_call(
        _outproj_router_kernel,
        out_shape=(jax.ShapeDtypeStruct((B, S, D), F32),
                   jax.ShapeDtypeStruct((B, S, D // 2), jnp.uint32),
                   jax.ShapeDtypeStruct((B, S, LANE), jnp.int32),
                   jax.ShapeDtypeStruct((B, S, LANE), F32)),
        grid=(B, S // ROW_T),
        in_specs=[tile(D), const((D, D)), tile(D), _mod_spec(k_gate, nct, B),
                  const((1, D)), _mod_spec(k_sc, nct, B), _mod_spec(k_sh, nct, B),
                  const((D, LANE)), const((D, LANE)), const((1, LANE))],
        out_specs=(tile(D), tile(D // 2), tile(LANE), tile(LANE)),
        compiler_params=_cp(2, 48), name="out_proj_router",
    )(y, w_out, xa, mod3, g.reshape(1, D), mod3, mod3, wr_hi, wr_lo, br)


DMA_UNROLL = 8


def _expert_kernel(tok_ref, order_ref, rstart_ref, nvalid_ref, be_ref, nexte_ref, nused_ref,
                   x_hbm, w1_hbm, w3_hbm, w2_hbm, y_hbm,
                   xbuf, ybuf, st1, st3, st2, wb1, wb3, wb2, gsem, ssem, wsem, *, layer):
    i = pl.program_id(0)
    nused = nused_ref[0]
    last = pl.num_programs(0) - 1
    used = i < nused
    weights = ((w1_hbm, st1, wb1), (w3_hbm, st3, wb3), (w2_hbm, st2, wb2))

    def gather_rows(blk):
        slot = blk & 1
        rs = rstart_ref[blk]

        def group(k, carry):
            for u in range(DMA_UNROLL):
                src = x_hbm.at[pl.ds(tok_ref[rs + k * DMA_UNROLL + u], 1), :]
                pltpu.make_async_copy(src, xbuf.at[slot, k, pl.ds(u, 1), :], gsem.at[slot]).start(priority=u % 2)
            return carry

        lax.fori_loop(0, MOE_T // DMA_UNROLL, group, 0)

    def scatter_rows(blk, wait):
        slot = blk & 1
        rs = rstart_ref[blk]
        nv = nvalid_ref[blk]

        def row(k, u, r, prio):
            dst = y_hbm.at[pl.ds(0 if wait else order_ref[rs + r], 1), :]
            cp = pltpu.make_async_copy(ybuf.at[slot, k, pl.ds(u, 1), :], dst, ssem.at[slot])
            if wait:
                cp.wait()
            else:
                cp.start(priority=prio)

        def group(k, carry):
            if wait:
                pltpu.make_async_copy(ybuf.at[slot, k], y_hbm.at[pl.ds(0, DMA_UNROLL), :], ssem.at[slot]).wait()
            else:
                for u in range(DMA_UNROLL):
                    row(k, u, k * DMA_UNROLL + u, u % 2)
            return carry

        def tail(r, carry):
            row(lax.shift_right_logical(r, 3), r & (DMA_UNROLL - 1), r, 0)
            return carry

        full = nv // DMA_UNROLL
        lax.fori_loop(0, full, group, 0)
        lax.fori_loop(full * DMA_UNROLL, nv, tail, 0)

    def fetch_weights(e):
        for k, (hbm, st, _) in enumerate(weights):
            pltpu.make_async_copy(hbm.at[layer, e], st, wsem.at[k]).start()

    @pl.when(i == 0)
    def _():
        fetch_weights(be_ref[0])
        gather_rows(0)

    @pl.when(i + 1 < nused)
    def _():
        gather_rows(i + 1)

    @pl.when(jnp.logical_and(i >= 2, i - 2 < nused))
    def _():
        scatter_rows(i - 2, True)

    @pl.when(used)
    def _():
        e = be_ref[i]

        @pl.when(jnp.logical_or(i == 0, be_ref[jnp.maximum(i - 1, 0)] != e))
        def _():
            for k, (hbm, st, wb) in enumerate(weights):
                pltpu.make_async_copy(hbm.at[layer, 0], st, wsem.at[k]).wait()
                wb[...] = st[...].astype(BF16)

            @pl.when(nexte_ref[i] >= 0)
            def _():
                fetch_weights(nexte_ref[i])

        slot = i & 1
        pltpu.make_async_copy(xbuf.at[slot], xbuf.at[slot], gsem.at[slot]).wait()
        xl, xh = _unpack2(xbuf[slot].reshape(MOE_T, D // 2))
        xl = xl.astype(BF16)
        xh = xh.astype(BF16)
        half = D // 2
        a = _dot(xl, wb1[:half, :]) + _dot(xh, wb1[half:, :])
        b = _dot(xl, wb3[:half, :]) + _dot(xh, wb3[half:, :])
        hmid = (a * jax.nn.sigmoid(a) * b).astype(BF16)
        y = _dot(hmid, wb2[...])
        ybuf[slot] = _pack2(y[:, :half], y[:, half:]).reshape(MOE_T // DMA_UNROLL, DMA_UNROLL, half)
        scatter_rows(i, False)

    @pl.when(i == last)
    def _():
        @pl.when(jnp.logical_and(i >= 1, i - 1 < nused))
        def _():
            scatter_rows(i - 1, True)

        @pl.when(used)
        def _():
            scatter_rows(i, True)


def expert_ffn(hp, plan, w1, w3, w2, layer, n_blocks, n_pairs):
    w = hp.shape[1]
    hbm = pl.BlockSpec(memory_space=pl.ANY)
    grid_spec = pltpu.PrefetchScalarGridSpec(
        num_scalar_prefetch=7, grid=(n_blocks,),
        in_specs=[hbm, hbm, hbm, hbm],
        out_specs=hbm,
        scratch_shapes=[pltpu.VMEM((2, MOE_T // DMA_UNROLL, DMA_UNROLL, w), jnp.uint32),
                        pltpu.VMEM((2, MOE_T // DMA_UNROLL, DMA_UNROLL, w), jnp.uint32),
                        pltpu.VMEM((D, FF), F32), pltpu.VMEM((D, FF), F32), pltpu.VMEM((FF, D), F32),
                        pltpu.VMEM((D, FF), BF16), pltpu.VMEM((D, FF), BF16), pltpu.VMEM((FF, D), BF16),
                        pltpu.SemaphoreType.DMA((2,)), pltpu.SemaphoreType.DMA((2,)),
                        pltpu.SemaphoreType.DMA((3,))])
    return pl.pallas_call(
        functools.partial(_expert_kernel, layer=layer),
        out_shape=jax.ShapeDtypeStruct((n_pairs, w), jnp.uint32),
        grid_spec=grid_spec, compiler_params=_cp(1, 56), name="moe_experts",
    )(*plan, hp, w1, w3, w2)


def _combine(ya_ref, yb_ref, x_ref, wt_ref, g_ref):
    w0 = wt_ref[:, 0:1]
    w1 = wt_ref[:, 1:2]
    al, ah = _unpack2(ya_ref[...])
    bl, bh = _unpack2(yb_ref[...])
    half = D // 2
    g = g_ref[...]
    return (x_ref[:, :half] + g[:, :half] * (w0 * al + w1 * bl),
            x_ref[:, half:] + g[:, half:] * (w0 * ah + w1 * bh))


def _combine_kernel(ya_ref, yb_ref, x_ref, wt_ref, g_ref, o_ref):
    lo, hi = _combine(ya_ref, yb_ref, x_ref, wt_ref, g_ref)
    o_ref[:, :D // 2] = lo
    o_ref[:, D // 2:] = hi


def _combine_norm_kernel(ya_ref, yb_ref, x_ref, wt_ref, g_ref, ng_ref, sc_ref, sh_ref, o_ref, h_ref):
    lo, hi = _combine(ya_ref, yb_ref, x_ref, wt_ref, g_ref)
    o_ref[:, :D // 2] = lo
    o_ref[:, D // 2:] = hi
    x = o_ref[...]
    y = x * lax.rsqrt(jnp.mean(x * x, axis=-1, keepdims=True) + EPS) * ng_ref[...]
    h_ref[...] = (y * (1.0 + sc_ref[...]) + sh_ref[...]).astype(h_ref.dtype)


def combine(y2, xa, wts, mod3, k_gate, nct, latent_only):
    B, S, _ = xa.shape
    w = y2.shape[-1]
    off = nct if latent_only else 0
    tile = lambda wd: pl.BlockSpec((None, ROW_T, wd), lambda b, j: (b, j + off, 0))
    pick = lambda k: pl.BlockSpec((None, None, ROW_T, w), lambda b, j: (k, b, j + off, 0))
    return pl.pallas_call(
        _combine_kernel,
        out_shape=jax.ShapeDtypeStruct((B, S - off * ROW_T, D), F32),
        grid=(B, S // ROW_T - off),
        in_specs=[pick(0), pick(1), tile(D), tile(LANE), _mod_spec(k_gate, nct, B, off)],
        out_specs=pl.BlockSpec((None, ROW_T, D), lambda b, j: (b, j, 0)),
        compiler_params=_cp(2, 32), name="moe_combine",
    )(y2, y2, xa, wts, mod3)


def combine_norm(y2, xa, wts, mod3, k_gate, nct, g_next, mod3_next, k_sh, k_sc):
    B, S, _ = xa.shape
    w = y2.shape[-1]
    tile = lambda wd: pl.BlockSpec((None, ROW_T, wd), lambda b, j: (b, j, 0))
    pick = lambda k: pl.BlockSpec((None, None, ROW_T, w), lambda b, j: (k, b, j, 0))
    return pl.pallas_call(
        _combine_norm_kernel,
        out_shape=(jax.ShapeDtypeStruct((B, S, D), F32), jax.ShapeDtypeStruct((B, S, D), BF16)),
        grid=(B, S // ROW_T),
        in_specs=[pick(0), pick(1), tile(D), tile(LANE), _mod_spec(k_gate, nct, B),
                  pl.BlockSpec((1, D), lambda b, j: (0, 0)), _mod_spec(k_sc, nct, B), _mod_spec(k_sh, nct, B)],
        out_specs=(tile(D), tile(D)),
        compiler_params=_cp(2, 32), name="moe_combine_norm",
    )(y2, y2, xa, wts, mod3, g_next.reshape(1, D), mod3_next, mod3_next)


def moe_plan(eid, n_blocks, n_tok):
    order = jnp.argsort(eid).astype(jnp.int32)
    tok = jnp.pad(jnp.where(order >= n_tok, order - n_tok, order), (0, MOE_T))
    experts = jnp.arange(N_EXP, dtype=jnp.int32)
    counts = jnp.sum((eid[:, None] == experts[None, :]).astype(jnp.int32), axis=0)
    nblk_e = (counts + MOE_T - 1) // MOE_T
    bend = jnp.cumsum(nblk_e)
    bstart = bend - nblk_e
    start = jnp.cumsum(counts) - counts
    blk = jnp.arange(n_blocks, dtype=jnp.int32)
    owner = jnp.logical_and(blk[:, None] >= bstart[None, :], blk[:, None] < bend[None, :]).astype(jnp.int32)
    take = lambda v: jnp.sum(owner * v[None, :], axis=1)
    within = blk - take(bstart)
    nused = bend[-1]
    last_e = jnp.max(jnp.where(counts > 0, experts, 0))
    block_e = jnp.where(blk < nused, take(experts), last_e).astype(jnp.int32)
    rstart = jnp.where(blk < nused, take(start) + within * MOE_T, 0).astype(jnp.int32)
    nvalid = jnp.where(blk < nused, jnp.clip(take(counts) - within * MOE_T, 0, MOE_T), 0).astype(jnp.int32)
    nxt = take(bend)
    e_at_nxt = jnp.sum((blk[None, :] == nxt[:, None]).astype(jnp.int32) * block_e[None, :], axis=1)
    next_e = jnp.where(jnp.logical_and(blk < nused, nxt < nused), e_at_nxt, -1).astype(jnp.int32)
    return tok, order, rstart, nvalid, block_e, next_e, nused.astype(jnp.int32).reshape(1)


def _rope_tables(s, lc):
    t = np.arange(s - lc)
    row = (t // GRID_W).astype(np.float64)
    col = (t % GRID_W).astype(np.float64)
    nf = HD // 4
    inv = np.float32(ROPE_BASE) ** (-np.arange(nf, dtype=np.float32) / np.float32(nf))
    ang = np.concatenate([row[:, None] * inv, col[:, None] * inv], axis=-1)
    ang = np.concatenate([np.zeros((lc, HD // 2)), ang], axis=0)
    cos = np.repeat(np.cos(ang), 2, axis=-1)
    sin = np.repeat(np.sin(ang), 2, axis=-1) * np.tile([-1.0, 1.0], HD // 2)
    return jnp.asarray(np.tile(cos, (1, LANE // HD)), F32), jnp.asarray(np.tile(sin, (1, LANE // HD)), F32)


def _chan_dft():
    j = np.arange(FGD)
    ang = 2.0 * np.pi * ((j[:, None] * j[None, :]) % FGD) / FGD
    c = np.kron(np.eye(FG), np.cos(ang)) / np.sqrt(FGD)
    s = np.kron(np.eye(FG), np.sin(ang)) / np.sqrt(FGD)
    return jnp.asarray(np.concatenate([c, -s], axis=1), BF16)


def _pos_dft(n):
    if n <= 512:
        k = np.arange(n)
        ang = 2.0 * np.pi * ((k[:, None] * k[None, :]) % n) / n
        return (jnp.asarray(np.cos(ang) / np.sqrt(n), BF16), jnp.asarray(np.sin(ang) / np.sqrt(n), BF16))
    nr = n // LANE
    k = np.arange(n)
    a = 2.0 * np.pi * ((k[:, None] * np.arange(nr)[None, :]) % nr) / nr
    b = 2.0 * np.pi * (k[:, None] * np.arange(LANE)[None, :]) / n
    sc = 1.0 / np.sqrt(n)
    ca, sa = jnp.asarray(np.cos(a) * sc, F32)[:, :, None], jnp.asarray(np.sin(a) * sc, F32)[:, :, None]
    cb, sb = jnp.asarray(np.cos(b), F32)[:, None, :], jnp.asarray(np.sin(b), F32)[:, None, :]
    ct = (ca * cb - sa * sb).reshape(n, n).astype(BF16)
    st = (sa * cb + ca * sb).reshape(n, n).astype(BF16)
    return ct, st


def _proj_weights(w_in):
    dup = lambda w: jnp.concatenate([w.reshape(D, N_KV, HD)] * 2, axis=-1).reshape(D, 2 * N_KV * HD)
    w_big = jnp.concatenate([w_in[:, 0:1024], dup(w_in[:, 1024:1280]), dup(w_in[:, 1280:1536]),
                             w_in[:, 4640:], w_in[:, 2048:2560], w_in[:, 3584:4608]], axis=1).astype(BF16)
    w_gate = jnp.pad(w_in[:, 4608:4640], ((0, 0), (0, LANE - 32))).astype(BF16)
    return w_big, w_gate


WT_T = 512


def _wt_kernel(w_ref, o_ref):
    scale = jnp.where(pl.program_id(0) == 0, ML_DK ** -0.5, 1.0)
    o_ref[...] = (jnp.transpose(w_ref[...]) * scale).astype(o_ref.dtype)


def qv_weights_t(w_in, layer):
    q_blk, v_blk = 1536 // WT_T, 2560 // WT_T
    col = lambda n: jnp.where(n == 0, q_blk, v_blk + n - 1)
    return pl.pallas_call(
        _wt_kernel,
        out_shape=jax.ShapeDtypeStruct((QV_ROWS, D), BF16),
        grid=(QV_ROWS // WT_T, D // WT_T),
        in_specs=[pl.BlockSpec((None, WT_T, WT_T), lambda n, k: (layer, k, col(n)))],
        out_specs=pl.BlockSpec((WT_T, WT_T), lambda n, k: (n, k)),
        compiler_params=_cp(2, 16), name="qv_weights_t",
    )(w_in)


def _qk_gain(qg, kg):
    qs = jnp.tile(qg * (HD ** -0.5 * LOG2E), N_HEADS)
    return jnp.concatenate([qs, jnp.tile(kg, 2 * N_KV)]).reshape(1, -1).astype(F32)


def kernel(x, c, ctx, c_ctx, w_mod, b_mod, norm1_g, norm2_g, w_in, q_norm_g, k_norm_g, attn_sink, ml_gate_b,
           ml_norm_g, w_br_attn, w_br_four, w_br_mlstm, b_gate, w_out, w_grp, b_grp, w_exp_router,
           b_exp_router, w1, w3, w2):
    B, T, _ = x.shape
    lc = ctx.shape[1]
    S = lc + T
    R = B * S
    depth = w_mod.shape[0]
    nct = lc // ROW_T
    assert lc % ROW_T == 0 and T % ROW_T == 0 and B < 8

    xa = jnp.concatenate([ctx, x], axis=1)
    cvec = jnp.concatenate([c, c_ctx[None], jnp.zeros((7 - B, D), F32)], axis=0)
    cos_t, sin_t = _rope_tables(S, lc)
    bd = jnp.asarray(np.kron(np.eye(LANE // HD), np.ones((HD, HD))), BF16)
    wc = _chan_dft()
    ct_lat, st_lat = _pos_dft(T)
    ct_ctx, st_ctx = _pos_dft(lc)
    nchunk = S // ML_L
    n_pairs = R * 2
    n_blocks = -(-n_pairs // MOE_T) + N_EXP

    mods = [modulation(cvec, w_mod, b_mod, l).reshape(8, 1, 6 * D) for l in range(depth)]
    h = norm_mod(xa, norm1_g[0], mods[0], 0, 1, nct)
    for l in range(depth):
        mod3 = mods[l]
        w_big, w_gate = _proj_weights(w_in[l])
        w_qv_t = qv_weights_t(w_in, l)
        h2d = h.reshape(R, D)
        proj = mm(h2d, w_big, BF16, 1536).reshape(B, S, C_TOT)
        graw = mm(h2d, w_gate, F32, LANE).reshape(B, S, LANE)

        qkn = qk_prep(proj, _qk_gain(q_norm_g[l], k_norm_g[l]), cos_t, sin_t, bd)
        a = attention(qkn, proj, attn_sink[l].astype(F32) * LOG2E, lc)

        f_ctx = dft_positions(ct_ctx, st_ctx, dft_channels(proj, wc, 0, lc))
        f_lat = dft_positions(ct_lat, st_lat, dft_channels(proj, wc, lc, T))
        f = jnp.concatenate([f_ctx, f_lat], axis=1)

        gbias = jnp.pad(ml_gate_b[l].reshape(1, 4 * ML_H).astype(F32), ((0, 0), (0, LANE - 4 * ML_H)))
        gproc = gate_prep(graw, gbias)[..., :4 * ML_H]
        g5 = gproc.reshape(B, S, 2, 2, ML_H // 2, 2).transpose(0, 4, 1, 2, 3, 5).reshape(B, ML_H // 2, S, 8)
        gates = jnp.pad(g5, ((0, 0), (0, 0), (0, 0), (0, LANE - 8)))
        gates_t = g5.reshape(B, ML_H // 2, nchunk, ML_L, 8).transpose(0, 1, 2, 4, 3)
        qv_t = mm_chunk_transposed(h, w_qv_t, ML_L)
        m = mlstm(proj, qv_t, gates, gates_t, ml_norm_g[l].reshape(1, -1).astype(F32), lc)

        y = merge(a.reshape(R, -1), f.reshape(R, -1), m.reshape(R, -1), proj.reshape(R, C_TOT),
                  w_br_attn[l].astype(BF16), w_br_four[l].astype(BF16), w_br_mlstm[l].astype(BF16),
                  b_gate[l].reshape(3, 1, D).astype(F32))
        wr = jnp.concatenate([w_grp[l], w_exp_router[l].reshape(D, N_EXP)], axis=1)
        wr = jnp.pad(wr, ((0, 0), (0, LANE - N_GRP - N_EXP))).astype(F32)
        wr_hi = wr.astype(BF16)
        wr_lo = (wr - wr_hi.astype(F32)).astype(BF16)
        br = jnp.pad(jnp.concatenate([b_grp[l], b_exp_router[l].reshape(N_EXP)]),
                     (0, LANE - N_GRP - N_EXP)).reshape(1, LANE).astype(F32)
        xa, hp, eid, wts = out_proj_router(y.reshape(B, S, D), w_out[l].astype(BF16), xa, norm2_g[l], mod3, 2, 3, 4,
                                           nct, wr_hi, wr_lo, br)
        plan = moe_plan(jnp.concatenate([eid[..., 0].reshape(R), eid[..., 1].reshape(R)]), n_blocks, R)
        y2 = expert_ffn(hp.reshape(R, D // 2), plan, w1, w3, w2, l, n_blocks, n_pairs).reshape(2, B, S, D // 2)
        if l == depth - 1:
            xa = combine(y2, xa, wts, mod3, 5, nct, True)
        else:
            xa, h = combine_norm(y2, xa, wts, mod3, 5, nct, norm1_g[l + 1], mods[l + 1], 0, 1)

    return xa
```

```python
import functools

import numpy as np
import jax
import jax.numpy as jnp
from jax import lax
from jax.experimental import pallas as pl
from jax.experimental.pallas import tpu as pltpu

F32 = jnp.float32
BF16 = jnp.bfloat16

D = 2048
N_HEADS, N_KV, HD = 16, 4, 64
GRID_W = 64
ROPE_BASE = 10000.0
EPS = 1e-6
ML_H, ML_DK, ML_DV = 8, 64, 128
ML_L = 256
FG, FGD = 4, 256
N_GRP, EPG, N_EXP, FF = 4, 8, 32, 1024
NEG = -1e30
LOG2E = 1.4426950408889634
LANE = 128
ROW_T = 256
MOE_T = 256

C_Q, C_K, C_V = 0, 1024, 1536
C_F, C_GP = 2048, 3072
C_MK, C_MO, C_TOT = 9216, 9728, 10752
QV_ROWS = ML_H * (ML_DK + ML_DV)


def _cp(n_axes, vmem_mb):
    return pltpu.CompilerParams(dimension_semantics=("arbitrary",) * n_axes,
                                vmem_limit_bytes=int(vmem_mb) * 2 ** 20)


def _pick(n, cands):
    for c in cands:
        if n % c == 0:
            return c
    raise ValueError(f"no tile for {n} in {cands}")


def _dot(a, b):
    return jnp.dot(a, b, preferred_element_type=F32)


def _dot_nt(a, b):
    return lax.dot_general(a, b, (((1,), (1,)), ((), ())), preferred_element_type=F32)


def _split(x):
    hi = x.astype(BF16)
    return hi, (x - hi.astype(F32)).astype(BF16)


def _dot3(a, w):
    ah, al = _split(a)
    wh, wl = _split(w)
    return _dot(ah, wh) + _dot(al, wh) + _dot(ah, wl)


def _mod_kernel(a_ref, w_ref, b_ref, o_ref):
    a = a_ref[...]
    a = a * jax.nn.sigmoid(a)
    o_ref[...] = _dot3(a, w_ref[...]) + b_ref[...]


def modulation(cvec, w_mod, b_mod, layer):
    depth, _, n = w_mod.shape
    tn = 512
    return pl.pallas_call(
        _mod_kernel,
        out_shape=jax.ShapeDtypeStruct((8, n), F32),
        grid=(n // tn,),
        in_specs=[pl.BlockSpec((8, D), lambda j: (0, 0)),
                  pl.BlockSpec((None, D, tn), lambda j: (layer, 0, j)),
                  pl.BlockSpec((None, 1, tn), lambda j: (layer, 0, j))],
        out_specs=pl.BlockSpec((8, tn), lambda j: (0, j)),
        compiler_params=_cp(1, 24), name="modulation",
    )(cvec, w_mod, b_mod.reshape(depth, 1, n))


def _mod_spec(k, nct, nb, off=0):
    return pl.BlockSpec((None, 1, D), lambda b, j: (jnp.where(j + off < nct, nb, b), 0, k))


def _norm_kernel(x_ref, g_ref, sc_ref, sh_ref, o_ref):
    x = x_ref[...]
    y = x * lax.rsqrt(jnp.mean(x * x, axis=-1, keepdims=True) + EPS) * g_ref[...]
    o_ref[...] = (y * (1.0 + sc_ref[...]) + sh_ref[...]).astype(o_ref.dtype)


def norm_mod(xa, g, mod3, k_sh, k_sc, nct):
    B, S, _ = xa.shape
    return pl.pallas_call(
        _norm_kernel,
        out_shape=jax.ShapeDtypeStruct((B, S, D), BF16),
        grid=(B, S // ROW_T),
        in_specs=[pl.BlockSpec((None, ROW_T, D), lambda b, j: (b, j, 0)),
                  pl.BlockSpec((1, D), lambda b, j: (0, 0)),
                  _mod_spec(k_sc, nct, B), _mod_spec(k_sh, nct, B)],
        out_specs=pl.BlockSpec((None, ROW_T, D), lambda b, j: (b, j, 0)),
        compiler_params=_cp(2, 24), name="norm_mod",
    )(xa, g.reshape(1, D), mod3, mod3)


def _mm_kernel(a_ref, b_ref, o_ref):
    o_ref[...] = _dot(a_ref[...], b_ref[...]).astype(o_ref.dtype)


def mm(a, b, out_dtype, tn):
    M, K = a.shape
    N = b.shape[1]
    tm = _pick(M, (1024, 512, 256))
    return pl.pallas_call(
        _mm_kernel,
        out_shape=jax.ShapeDtypeStruct((M, N), out_dtype),
        grid=(M // tm, N // tn),
        in_specs=[pl.BlockSpec((tm, K), lambda i, j: (i, 0)),
                  pl.BlockSpec((K, tn), lambda i, j: (0, j))],
        out_specs=pl.BlockSpec((tm, tn), lambda i, j: (i, j)),
        compiler_params=_cp(2, 40), name="mm",
    )(a, b)


def _mm_t_kernel(w_ref, h_ref, o_ref):
    o_ref[...] = _dot_nt(w_ref[...], h_ref[...]).astype(o_ref.dtype)


def mm_chunk_transposed(h, w_t, chunk):
    B, S, K = h.shape
    n = w_t.shape[0]
    return pl.pallas_call(
        _mm_t_kernel,
        out_shape=jax.ShapeDtypeStruct((B, S // chunk, n, chunk), BF16),
        grid=(B, S // chunk),
        in_specs=[pl.BlockSpec((n, K), lambda b, c: (0, 0)),
                  pl.BlockSpec((None, chunk, K), lambda b, c: (b, c, 0))],
        out_specs=pl.BlockSpec((None, None, n, chunk), lambda b, c: (b, c, 0, 0)),
        compiler_params=_cp(2, 40), name="mm_t",
    )(w_t, h)


def _prep_kernel(x_ref, g_ref, cos_ref, sin_ref, bd_ref, o_ref):
    cosv = cos_ref[...]
    sinv = sin_ref[...]
    bd = bd_ref[...]
    lane = lax.broadcasted_iota(jnp.int32, cosv.shape, 1)
    even = (lane & 1) == 0
    for s in range((C_V - C_Q) // LANE):
        sl = slice(s * LANE, (s + 1) * LANE)
        x = x_ref[:, sl].astype(F32)
        hi, lo = _split(x * x)
        ss = _dot(hi, bd) + _dot(lo, bd)
        y = x * lax.rsqrt(ss * (1.0 / HD) + EPS) * g_ref[:, sl]
        sw = jnp.where(even, pltpu.roll(y, LANE - 1, 1), pltpu.roll(y, 1, 1))
        o_ref[:, sl] = (y * cosv + sw * sinv).astype(o_ref.dtype)


def qk_prep(proj, gain, cos_t, sin_t, bd):
    B, S, _ = proj.shape
    w = C_V - C_Q
    return pl.pallas_call(
        _prep_kernel,
        out_shape=jax.ShapeDtypeStruct((B, S, w), BF16),
        grid=(B, S // ROW_T),
        in_specs=[pl.BlockSpec((None, ROW_T, w), lambda b, j: (b, j, 0)),
                  pl.BlockSpec((1, w), lambda b, j: (0, 0)),
                  pl.BlockSpec((ROW_T, LANE), lambda b, j: (j, 0)),
                  pl.BlockSpec((ROW_T, LANE), lambda b, j: (j, 0)),
                  pl.BlockSpec((LANE, LANE), lambda b, j: (0, 0))],
        out_specs=pl.BlockSpec((None, ROW_T, w), lambda b, j: (b, j, 0)),
        compiler_params=_cp(2, 24), name="qk_prep",
    )(proj, gain, cos_t, sin_t, bd)


AB = 128


def _attn_kernel(sink_ref, q_ref, kp_ref, kc_ref, kn_ref, kx_ref, vp_ref, vc_ref, vn_ref, vx_ref, o_ref,
                 *, nctb, nblk):
    n = pl.program_id(1)
    is_lat = n >= nctb
    iq = lax.broadcasted_iota(jnp.int32, (AB, AB), 0)
    ik = lax.broadcasted_iota(jnp.int32, (AB, AB), 1)
    ok_p = jnp.logical_and(ik >= iq, jnp.logical_and(is_lat, n - 1 >= nctb))
    ok_c = jnp.logical_and(ik >= 0, is_lat)
    ok_n = jnp.logical_and(ik <= iq, jnp.logical_and(is_lat, n + 1 <= nblk - 1))
    nloc = 3 * AB
    bias = jnp.concatenate([jnp.where(ok_p, 0.0, NEG), jnp.where(ok_c, 0.0, NEG), jnp.where(ok_n, 0.0, NEG)], axis=1)
    kall = jnp.concatenate([kp_ref[...], kc_ref[...], kn_ref[...], kx_ref[...]], axis=0)
    vall = jnp.concatenate([vp_ref[...], vc_ref[...], vn_ref[...], vx_ref[...]], axis=0)
    ones = jnp.ones((kall.shape[0], LANE), BF16)
    lane = lax.broadcasted_iota(jnp.int32, (AB, LANE), 1)
    lo_half = lane < HD
    hi_half = jnp.logical_not(lo_half)
    gq = N_HEADS // N_KV
    for g in range(N_KV):
        kg = kall[:, g * LANE:(g + 1) * LANE]
        vext = jnp.concatenate([vall[:, g * LANE:(g + 1) * LANE], ones], axis=1)
        qparts = []
        for jj in range(2):
            qs = q_ref[:, (2 * g + jj) * LANE:(2 * g + jj + 1) * LANE]
            qparts += [jnp.where(lo_half, qs, jnp.zeros_like(qs)), jnp.where(hi_half, qs, jnp.zeros_like(qs))]
        s = _dot_nt(jnp.concatenate(qparts, axis=0), kg)
        ps, sink_w = [], []
        for seg in range(gq):
            sl = s[seg * AB:(seg + 1) * AB]
            s_loc = sl[:, :nloc] + bias
            s_ctx = sl[:, nloc:]
            sk = sink_ref[gq * g + seg]
            m = jnp.maximum(jnp.maximum(jnp.max(s_loc, axis=-1, keepdims=True),
                                        jnp.max(s_ctx, axis=-1, keepdims=True)), sk)
            ps.append(jnp.concatenate([jnp.exp2(s_loc - m), jnp.exp2(s_ctx - m)], axis=1).astype(BF16))
            sink_w.append(jnp.exp2(sk - m))
        o = _dot(jnp.concatenate(ps, axis=0), vext)
        for jj in range(2):
            outs = []
            for half in range(2):
                seg = 2 * jj + half
                rows = slice(seg * AB, (seg + 1) * AB)
                outs.append(o[rows, :LANE] / (o[rows, LANE:] + sink_w[seg]))
            sl_out = slice((2 * g + jj) * LANE, (2 * g + jj + 1) * LANE)
            o_ref[:, sl_out] = jnp.where(lo_half, outs[0], outs[1]).astype(o_ref.dtype)


def attention(qkn, proj, sink, lc):
    B, S, _ = qkn.shape
    nblk = S // AB
    nctb = lc // AB
    kw = C_V - C_K
    kcol, vcol = C_K // kw, C_V // kw

    def rows(fn, col):
        return pl.BlockSpec((None, AB, kw), lambda b, n: (b, fn(n), col))

    prev = lambda n: jnp.maximum(n - 1, 0)
    cur = lambda n: n
    nxt = lambda n: jnp.minimum(n + 1, nblk - 1)
    ctx = lambda col: pl.BlockSpec((None, lc, kw), lambda b, n: (b, 0, col))
    return pl.pallas_call(
        functools.partial(_attn_kernel, nctb=nctb, nblk=nblk),
        out_shape=jax.ShapeDtypeStruct((B, S, C_K), BF16),
        grid=(B, nblk),
        in_specs=[pl.BlockSpec(memory_space=pltpu.SMEM),
                  pl.BlockSpec((None, AB, C_K), lambda b, n: (b, n, 0)),
                  rows(prev, kcol), rows(cur, kcol), rows(nxt, kcol), ctx(kcol),
                  rows(prev, vcol), rows(cur, vcol), rows(nxt, vcol), ctx(vcol)],
        out_specs=pl.BlockSpec((None, AB, C_K), lambda b, n: (b, n, 0)),
        compiler_params=_cp(2, 32), name="attention",
    )(sink, qkn, qkn, qkn, qkn, qkn, proj, proj, proj, proj)


def _dftc_kernel(u_ref, w_ref, o_ref):
    o_ref[...] = _dot(u_ref[...], w_ref[...]).astype(o_ref.dtype)


def dft_channels(proj, wc, row0, nrows):
    B = proj.shape[0]
    off = row0 // ROW_T
    return pl.pallas_call(
        _dftc_kernel,
        out_shape=jax.ShapeDtypeStruct((B, nrows, 2 * FG * FGD), BF16),
        grid=(B, nrows // ROW_T),
        in_specs=[pl.BlockSpec((None, ROW_T, FG * FGD), lambda b, j: (b, j + off, C_F // (FG * FGD))),
                  pl.BlockSpec((FG * FGD, 2 * FG * FGD), lambda b, j: (0, 0))],
        out_specs=pl.BlockSpec((None, ROW_T, 2 * FG * FGD), lambda b, j: (b, j, 0)),
        compiler_params=_cp(2, 32), name="dft_channels",
    )(proj, wc)


def _dftp_kernel(c_ref, s_ref, y1_ref, y2_ref, o_ref, acc_ref):
    k = pl.program_id(2)

    @pl.when(k == 0)
    def _():
        acc_ref[...] = jnp.zeros_like(acc_ref)

    acc_ref[...] += _dot(c_ref[...], y1_ref[...]) + _dot(s_ref[...], y2_ref[...])

    @pl.when(k == pl.num_programs(2) - 1)
    def _():
        o_ref[...] = acc_ref[...].astype(o_ref.dtype)


def dft_positions(ct, st, y):
    B, n, w2 = y.shape
    w = w2 // 2
    t = _pick(n, (1024, 512, 256))
    return pl.pallas_call(
        _dftp_kernel,
        out_shape=jax.ShapeDtypeStruct((B, n, w), BF16),
        grid=(B, n // t, n // t),
        in_specs=[pl.BlockSpec((t, t), lambda b, i, k: (i, k)),
                  pl.BlockSpec((t, t), lambda b, i, k: (i, k)),
                  pl.BlockSpec((None, t, w), lambda b, i, k: (b, k, 0)),
                  pl.BlockSpec((None, t, w), lambda b, i, k: (b, k, 1))],
        out_specs=pl.BlockSpec((None, t, w), lambda b, i, k: (b, i, 0)),
        scratch_shapes=[pltpu.VMEM((t, w), F32)],
        compiler_params=_cp(3, 40), name="dft_positions",
    )(ct, st, y, y)


def _log_sigmoid(x):
    return jnp.minimum(x, 0.0) - jnp.log(1.0 + jnp.exp(-jnp.abs(x)))


def _gate_kernel(g_ref, b_ref, o_ref):
    L = ML_L
    g = g_ref[...] + b_ref[...]
    lane = lax.broadcasted_iota(jnp.int32, (L, LANE), 1)
    is_f = (lane & ML_H) == ML_H
    is_bwd = (lane & (2 * ML_H)) == 2 * ML_H
    lf = jnp.where(is_f, _log_sigmoid(g), 0.0)
    it_r = lax.broadcasted_iota(jnp.int32, (L, L), 0)
    it_c = lax.broadcasted_iota(jnp.int32, (L, L), 1)
    hp = lax.Precision.HIGHEST
    cf = jnp.dot(jnp.where(it_c <= it_r, 1.0, 0.0).astype(F32), lf, precision=hp, preferred_element_type=F32)
    cb = jnp.dot(jnp.where(it_c >= it_r, 1.0, 0.0).astype(F32), lf, precision=hp, preferred_element_type=F32)
    o_ref[...] = jnp.where(is_f, jnp.where(is_bwd, cb, cf), g)


def gate_prep(graw, bias):
    B, S, _ = graw.shape
    return pl.pallas_call(
        _gate_kernel,
        out_shape=jax.ShapeDtypeStruct((B, S, LANE), F32),
        grid=(B, S // ML_L),
        in_specs=[pl.BlockSpec((None, ML_L, LANE), lambda b, c: (b, c, 0)),
                  pl.BlockSpec((1, LANE), lambda b, c: (0, 0))],
        out_specs=pl.BlockSpec((None, ML_L, LANE), lambda b, c: (b, c, 0)),
        compiler_params=_cp(2, 16), name="gate_prep",
    )(graw, bias)


def _mlstm_kernel(k_ref, qt_ref, vt_ref, mo_ref, g_ref, gt_ref, ng_ref, o_ref,
                  hf_ref, hb_ref, c_ref, m_ref, *, nctc, nchunk):
    L = ML_L
    c_ref[...] = jnp.zeros_like(c_ref)
    m_ref[...] = jnp.zeros_like(m_ref)
    key_i = lax.broadcasted_iota(jnp.int32, (L, L), 0)
    qry_i = lax.broadcasted_iota(jnp.int32, (L, L), 1)
    tri = (key_i <= qry_i, key_i >= qry_i)
    lane = lax.broadcasted_iota(jnp.int32, (L, LANE), 1)
    head_lanes = (lane < ML_DK, lane >= ML_DK)
    ones_t = jnp.ones((ML_DV, L), BF16)
    h_refs = (hf_ref, hb_ref)

    def step(it, carry):
        stores = []
        for d in range(2):
            if d == 0:
                c = it
            else:
                c = jnp.where(it < nctc, nctc - 1 - it, nchunk - 1 - (it - nctc))
            r0 = pl.multiple_of(c * L, L)
            gc = g_ref[pl.ds(r0, L), :]
            gr = gt_ref[c]
            kp = k_ref[pl.ds(r0, L), :]
            qt = qt_ref[c]
            last = L - 1 if d == 0 else 0
            for hh in range(2):
                icol = d * 4 + hh
                fcol = d * 4 + 2 + hh
                sidx = d * 2 + hh
                key_term = gc[:, icol:icol + 1] - gc[:, fcol:fcol + 1]
                b_row = gr[fcol:fcol + 1, :]
                i_row = gr[icol:icol + 1, :]
                m_prev = m_ref[sidx:sidx + 1, 0:1]
                logd = jnp.where(tri[d], key_term + b_row, NEG)
                inter = b_row + m_prev
                m_t = jnp.maximum(inter, jnp.max(logd, axis=0, keepdims=True))
                km = jnp.where(head_lanes[hh], kp, jnp.zeros_like(kp))
                p_t = (_dot(km, qt) * jnp.exp(logd - m_t)).astype(BF16)
                q_in = (qt.astype(F32) * jnp.exp(inter - m_t)).astype(BF16)
                vext_t = jnp.concatenate([vt_ref[c, hh * ML_DV:(hh + 1) * ML_DV, :], ones_t], axis=0)
                c_t = c_ref[sidx]
                tot = _dot(jnp.concatenate([vext_t, c_t.astype(BF16)], axis=1),
                           jnp.concatenate([p_t, q_in], axis=0))
                h_t = tot[:ML_DV] / jnp.maximum(jnp.abs(tot[ML_DV:]), jnp.exp(-m_t))
                b_last = b_row[:, last:last + 1]
                log_w = b_last - b_row + i_row
                m_new = jnp.maximum(b_last + m_prev, jnp.max(log_w, axis=-1, keepdims=True))
                w_row = jnp.exp(log_w - m_new).astype(BF16)
                c_new = jnp.exp(b_last + m_prev - m_new) * c_t + _dot(vext_t * w_row, km)
                stores.append((d, hh, sidx, c, h_t, c_new, m_new))
        for d, hh, sidx, c, h_t, c_new, m_new in stores:
            h_refs[d][c, hh] = h_t
            c_ref[sidx] = c_new
            m_ref[sidx:sidx + 1, :] = jnp.broadcast_to(m_new, (1, LANE))
        return carry

    lax.fori_loop(0, nchunk, step, 0)

    def out_step(i, carry):
        r0 = pl.multiple_of(i * L, L)
        for hh in range(2):
            hs = slice(hh * ML_DV, (hh + 1) * ML_DV)
            h = jnp.transpose(hf_ref[i, hh] + hb_ref[i, hh])
            hn = h * lax.rsqrt(jnp.mean(h * h, axis=-1, keepdims=True) + EPS) * ng_ref[:, hs]
            og = jax.nn.sigmoid(mo_ref[pl.ds(r0, L), hs].astype(F32))
            o_ref[pl.ds(r0, L), hs] = (og * hn).astype(o_ref.dtype)
        return carry

    lax.fori_loop(0, nchunk, out_step, 0)


def mlstm(proj, qv_t, gates, gates_t, norm_g, lc):
    B, S, _ = proj.shape
    nchunk = S // ML_L
    pw = 2 * ML_DK
    vw = 2 * ML_DV
    return pl.pallas_call(
        functools.partial(_mlstm_kernel, nctc=lc // ML_L, nchunk=nchunk),
        out_shape=jax.ShapeDtypeStruct((B, S, ML_H * ML_DV), BF16),
        grid=(B, ML_H // 2),
        in_specs=[pl.BlockSpec((None, S, pw), lambda b, p: (b, 0, C_MK // pw + p)),
                  pl.BlockSpec((None, nchunk, pw, ML_L), lambda b, p: (b, 0, p, 0)),
                  pl.BlockSpec((None, nchunk, vw, ML_L), lambda b, p: (b, 0, ML_H * ML_DK // vw + p, 0)),
                  pl.BlockSpec((None, S, vw), lambda b, p: (b, 0, C_MO // vw + p)),
                  pl.BlockSpec((None, None, S, LANE), lambda b, p: (b, p, 0, 0)),
                  pl.BlockSpec((None, None, nchunk, 8, ML_L), lambda b, p: (b, p, 0, 0, 0)),
                  pl.BlockSpec((1, vw), lambda b, p: (0, p))],
        out_specs=pl.BlockSpec((None, S, vw), lambda b, p: (b, 0, p)),
        scratch_shapes=[pltpu.VMEM((nchunk, 2, ML_DV, ML_L), F32), pltpu.VMEM((nchunk, 2, ML_DV, ML_L), F32),
                        pltpu.VMEM((4, 2 * ML_DV, 2 * ML_DK), F32), pltpu.VMEM((8, LANE), F32)],
        compiler_params=_cp(2, 48), name="mlstm",
    )(proj, qv_t, qv_t, proj, gates, gates_t, norm_g)


def _merge_kernel(a_ref, f_ref, m_ref, wa_ref, wf_ref, wm_ref, ga_ref, gf_ref, gm_ref,
                  ba_ref, bf_ref, bm_ref, o_ref):
    def br(x_ref, w_ref, g_ref, b_ref):
        return jax.nn.sigmoid(g_ref[...].astype(F32) + b_ref[...]) * _dot(x_ref[...], w_ref[...])

    y = br(a_ref, wa_ref, ga_ref, ba_ref) + br(f_ref, wf_ref, gf_ref, bf_ref) + br(m_ref, wm_ref, gm_ref, bm_ref)
    o_ref[...] = y.astype(o_ref.dtype)


def merge(a, f, m, proj, wa, wf, wm, b_gate):
    R, kin = a.shape
    tm = _pick(R, (512, 256))
    tn = 1024
    x_spec = pl.BlockSpec((tm, kin), lambda j, i: (i, 0))
    w_spec = pl.BlockSpec((kin, tn), lambda j, i: (0, j))
    gp = lambda br: pl.BlockSpec((tm, tn), lambda j, i: (i, (C_GP + br * D) // tn + j))
    bg = lambda br: pl.BlockSpec((None, 1, tn), lambda j, i: (br, 0, j))
    return pl.pallas_call(
        _merge_kernel,
        out_shape=jax.ShapeDtypeStruct((R, D), BF16),
        grid=(D // tn, R // tm),
        in_specs=[x_spec, x_spec, x_spec, w_spec, w_spec, w_spec, gp(0), gp(1), gp(2), bg(0), bg(1), bg(2)],
        out_specs=pl.BlockSpec((tm, tn), lambda j, i: (i, j)),
        compiler_params=_cp(2, 40), name="merge",
    )(a, f, m, wa, wf, wm, proj, proj, proj, b_gate, b_gate, b_gate)


def _pack2(lo, hi):
    lo_b = lax.bitcast_convert_type(lo.astype(BF16).astype(F32), jnp.uint32)
    hi_b = lax.bitcast_convert_type(hi.astype(BF16).astype(F32), jnp.uint32)
    return (lo_b >> 16) | (hi_b & jnp.uint32(0xFFFF0000))


def _unpack2(w):
    lo = lax.bitcast_convert_type(w << 16, F32)
    hi = lax.bitcast_convert_type(w & jnp.uint32(0xFFFF0000), F32)
    return lo, hi


def _outproj_router_kernel(y_ref, w_ref, x_ref, g1_ref, g_ref, sc_ref, sh_ref, wh_ref, wl_ref, b_ref,
                           xo_ref, hp_ref, eid_ref, wt_ref):
    x = x_ref[...] + g1_ref[...] * _dot(y_ref[...], w_ref[...])
    xo_ref[...] = x
    y = x * lax.rsqrt(jnp.mean(x * x, axis=-1, keepdims=True) + EPS) * g_ref[...]
    y = y * (1.0 + sc_ref[...]) + sh_ref[...]
    hp_ref[...] = _pack2(y[:, :D // 2], y[:, D // 2:])
    yh, yl = _split(y)
    logits = _dot(yh, wh_ref[...]) + _dot(yl, wh_ref[...]) + _dot(yh, wl_ref[...]) + b_ref[...]
    lane = lax.broadcasted_iota(jnp.int32, logits.shape, 1)
    lanef = lane.astype(F32)
    big = float(LANE)
    glog = jnp.where(lane < N_GRP, logits, NEG)
    gmax = jnp.max(glog, axis=-1, keepdims=True)
    gsel = jnp.min(jnp.where(glog == gmax, lanef, big), axis=-1, keepdims=True)
    pg = 1.0 / jnp.sum(jnp.exp(glog - gmax), axis=-1, keepdims=True)
    lo = N_GRP + EPG * gsel
    el = jnp.where(jnp.logical_and(lanef >= lo, lanef < lo + EPG), logits, NEG)
    v1 = jnp.max(el, axis=-1, keepdims=True)
    i1 = jnp.min(jnp.where(el == v1, lanef, big), axis=-1, keepdims=True)
    el2 = jnp.where(lanef == i1, NEG, el)
    v2 = jnp.max(el2, axis=-1, keepdims=True)
    i2 = jnp.min(jnp.where(el2 == v2, lanef, big), axis=-1, keepdims=True)
    e = jnp.exp(v2 - v1)
    w1 = pg / (1.0 + e)
    w2 = pg * e / (1.0 + e)
    eid_ref[...] = jnp.where(lane == 0, i1 - N_GRP, jnp.where(lane == 1, i2 - N_GRP, 0.0)).astype(jnp.int32)
    wt_ref[...] = jnp.where(lane == 0, w1, jnp.where(lane == 1, w2, 0.0))


def out_proj_router(y, w_out, xa, g, mod3, k_gate, k_sh, k_sc, nct, wr_hi, wr_lo, br):
    B, S, _ = xa.shape
    tile = lambda w: pl.BlockSpec((None, ROW_T, w), lambda b, j: (b, j, 0))
    const = lambda shape: pl.BlockSpec(shape, lambda b, j: (0, 0))
    return pl.pallas_call(
        _outproj_router_kernel,
        out_shape=(jax.ShapeDtypeStruct((B, S, D), F32),
                   jax.ShapeDtypeStruct((B, S, D // 2), jnp.uint32),
                   jax.ShapeDtypeStruct((B, S, LANE), jnp.int32),
                   jax.ShapeDtypeStruct((B, S, LANE), F32)),
        grid=(B, S // ROW_T),
        in_specs=[tile(D), const((D, D)), tile(D), _mod_spec(k_gate, nct, B),
                  const((1, D)), _mod_spec(k_sc, nct, B), _mod_spec(k_sh, nct, B),
                  const((D, LANE)), const((D, LANE)), const((1, LANE))],
        out_specs=(tile(D), tile(D // 2), tile(LANE), tile(LANE)),
        compiler_params=_cp(2, 48), name="out_proj_router",
    )(y, w_out, xa, mod3, g.reshape(1, D), mod3, mod3, wr_hi, wr_lo, br)


DMA_UNROLL = 8


def _expert_kernel(tok_ref, order_ref, rstart_ref, nvalid_ref, be_ref, nexte_ref, nused_ref,
                   x_hbm, w1_hbm, w3_hbm, w2_hbm, y_hbm,
                   xbuf, ybuf, st1, st3, st2, wb1, wb3, wb2, gsem, ssem, wsem, *, layer):
    i = pl.program_id(0)
    nused = nused_ref[0]
    last = pl.num_programs(0) - 1
    used = i < nused
    weights = ((w1_hbm, st1, wb1), (w3_hbm, st3, wb3), (w2_hbm, st2, wb2))

    def gather_rows(blk):
        slot = blk & 1
        rs = rstart_ref[blk]

        def group(k, carry):
            for u in range(DMA_UNROLL):
                src = x_hbm.at[pl.ds(tok_ref[rs + k * DMA_UNROLL + u], 1), :]
                pltpu.make_async_copy(src, xbuf.at[slot, k, pl.ds(u, 1), :], gsem.at[slot]).start(priority=u % 2)
            return carry

        lax.fori_loop(0, MOE_T // DMA_UNROLL, group, 0)

    def scatter_rows(blk, wait):
        slot = blk & 1
        rs = rstart_ref[blk]
        nv = nvalid_ref[blk]

        def row(k, u, r, prio):
            dst = y_hbm.at[pl.ds(0 if wait else order_ref[rs + r], 1), :]
            cp = pltpu.make_async_copy(ybuf.at[slot, k, pl.ds(u, 1), :], dst, ssem.at[slot])
            if wait:
                cp.wait()
            else:
                cp.start(priority=prio)

        def group(k, carry):
            if wait:
                pltpu.make_async_copy(ybuf.at[slot, k], y_hbm.at[pl.ds(0, DMA_UNROLL), :], ssem.at[slot]).wait()
            else:
                for u in range(DMA_UNROLL):
                    row(k, u, k * DMA_UNROLL + u, u % 2)
            return carry

        def tail(r, carry):
            row(lax.shift_right_logical(r, 3), r & (DMA_UNROLL - 1), r, 0)
            return carry

        full = nv // DMA_UNROLL
        lax.fori_loop(0, full, group, 0)
        lax.fori_loop(full * DMA_UNROLL, nv, tail, 0)

    def fetch_weights(e):
        for k, (hbm, st, _) in enumerate(weights):
            pltpu.make_async_copy(hbm.at[layer, e], st, wsem.at[k]).start()

    @pl.when(i == 0)
    def _():
        fetch_weights(be_ref[0])
        gather_rows(0)

    @pl.when(i + 1 < nused)
    def _():
        gather_rows(i + 1)

    @pl.when(jnp.logical_and(i >= 2, i - 2 < nused))
    def _():
        scatter_rows(i - 2, True)

    @pl.when(used)
    def _():
        e = be_ref[i]

        @pl.when(jnp.logical_or(i == 0, be_ref[jnp.maximum(i - 1, 0)] != e))
        def _():
            for k, (hbm, st, wb) in enumerate(weights):
                pltpu.make_async_copy(hbm.at[layer, 0], st, wsem.at[k]).wait()
                wb[...] = st[...].astype(BF16)

            @pl.when(nexte_ref[i] >= 0)
            def _():
                fetch_weights(nexte_ref[i])

        slot = i & 1
        pltpu.make_async_copy(xbuf.at[slot], xbuf.at[slot], gsem.at[slot]).wait()
        xl, xh = _unpack2(xbuf[slot].reshape(MOE_T, D // 2))
        xl = xl.astype(BF16)
        xh = xh.astype(BF16)
        half = D // 2
        a = _dot(xl, wb1[:half, :]) + _dot(xh, wb1[half:, :])
        b = _dot(xl, wb3[:half, :]) + _dot(xh, wb3[half:, :])
        hmid = (a * jax.nn.sigmoid(a) * b).astype(BF16)
        y = _dot(hmid, wb2[...])
        ybuf[slot] = _pack2(y[:, :half], y[:, half:]).reshape(MOE_T // DMA_UNROLL, DMA_UNROLL, half)
        scatter_rows(i, False)

    @pl.when(i == last)
    def _():
        @pl.when(jnp.logical_and(i >= 1, i - 1 < nused))
        def _():
            scatter_rows(i - 1, True)

        @pl.when(used)
        def _():
            scatter_rows(i, True)


def expert_ffn(hp, plan, w1, w3, w2, layer, n_blocks, n_pairs):
    w = hp.shape[1]
    hbm = pl.BlockSpec(memory_space=pl.ANY)
    grid_spec = pltpu.PrefetchScalarGridSpec(
        num_scalar_prefetch=7, grid=(n_blocks,),
        in_specs=[hbm, hbm, hbm, hbm],
        out_specs=hbm,
        scratch_shapes=[pltpu.VMEM((2, MOE_T // DMA_UNROLL, DMA_UNROLL, w), jnp.uint32),
                        pltpu.VMEM((2, MOE_T // DMA_UNROLL, DMA_UNROLL, w), jnp.uint32),
                        pltpu.VMEM((D, FF), F32), pltpu.VMEM((D, FF), F32), pltpu.VMEM((FF, D), F32),
                        pltpu.VMEM((D, FF), BF16), pltpu.VMEM((D, FF), BF16), pltpu.VMEM((FF, D), BF16),
                        pltpu.SemaphoreType.DMA((2,)), pltpu.SemaphoreType.DMA((2,)),
                        pltpu.SemaphoreType.DMA((3,))])
    return pl.pallas_call(
        functools.partial(_expert_kernel, layer=layer),
        out_shape=jax.ShapeDtypeStruct((n_pairs, w), jnp.uint32),
        grid_spec=grid_spec, compiler_params=_cp(1, 56), name="moe_experts",
    )(*plan, hp, w1, w3, w2)


def _combine(ya_ref, yb_ref, x_ref, wt_ref, g_ref):
    w0 = wt_ref[:, 0:1]
    w1 = wt_ref[:, 1:2]
    al, ah = _unpack2(ya_ref[...])
    bl, bh = _unpack2(yb_ref[...])
    half = D // 2
    g = g_ref[...]
    return (x_ref[:, :half] + g[:, :half] * (w0 * al + w1 * bl),
            x_ref[:, half:] + g[:, half:] * (w0 * ah + w1 * bh))


def _combine_kernel(ya_ref, yb_ref, x_ref, wt_ref, g_ref, o_ref):
    lo, hi = _combine(ya_ref, yb_ref, x_ref, wt_ref, g_ref)
    o_ref[:, :D // 2] = lo
    o_ref[:, D // 2:] = hi


def _combine_norm_kernel(ya_ref, yb_ref, x_ref, wt_ref, g_ref, ng_ref, sc_ref, sh_ref, o_ref, h_ref):
    lo, hi = _combine(ya_ref, yb_ref, x_ref, wt_ref, g_ref)
    o_ref[:, :D // 2] = lo
    o_ref[:, D // 2:] = hi
    x = o_ref[...]
    y = x * lax.rsqrt(jnp.mean(x * x, axis=-1, keepdims=True) + EPS) * ng_ref[...]
    h_ref[...] = (y * (1.0 + sc_ref[...]) + sh_ref[...]).astype(h_ref.dtype)


def combine(y2, xa, wts, mod3, k_gate, nct, latent_only):
    B, S, _ = xa.shape
    w = y2.shape[-1]
    off = nct if latent_only else 0
    tile = lambda wd: pl.BlockSpec((None, ROW_T, wd), lambda b, j: (b, j + off, 0))
    pick = lambda k: pl.BlockSpec((None, None, ROW_T, w), lambda b, j: (k, b, j + off, 0))
    return pl.pallas_call(
        _combine_kernel,
        out_shape=jax.ShapeDtypeStruct((B, S - off * ROW_T, D), F32),
        grid=(B, S // ROW_T - off),
        in_specs=[pick(0), pick(1), tile(D), tile(LANE), _mod_spec(k_gate, nct, B, off)],
        out_specs=pl.BlockSpec((None, ROW_T, D), lambda b, j: (b, j, 0)),
        compiler_params=_cp(2, 32), name="moe_combine",
    )(y2, y2, xa, wts, mod3)


def combine_norm(y2, xa, wts, mod3, k_gate, nct, g_next, mod3_next, k_sh, k_sc):
    B, S, _ = xa.shape
    w = y2.shape[-1]
    tile = lambda wd: pl.BlockSpec((None, ROW_T, wd), lambda b, j: (b, j, 0))
    pick = lambda k: pl.BlockSpec((None, None, ROW_T, w), lambda b, j: (k, b, j, 0))
    return pl.pallas_call(
        _combine_norm_kernel,
        out_shape=(jax.ShapeDtypeStruct((B, S, D), F32), jax.ShapeDtypeStruct((B, S, D), BF16)),
        grid=(B, S // ROW_T),
        in_specs=[pick(0), pick(1), tile(D), tile(LANE), _mod_spec(k_gate, nct, B),
                  pl.BlockSpec((1, D), lambda b, j: (0, 0)), _mod_spec(k_sc, nct, B), _mod_spec(k_sh, nct, B)],
        out_specs=(tile(D), tile(D)),
        compiler_params=_cp(2, 32), name="moe_combine_norm",
    )(y2, y2, xa, wts, mod3, g_next.reshape(1, D), mod3_next, mod3_next)


def moe_plan(eid, n_blocks, n_tok):
    order = jnp.argsort(eid).astype(jnp.int32)
    tok = jnp.pad(jnp.where(order >= n_tok, order - n_tok, order), (0, MOE_T))
    experts = jnp.arange(N_EXP, dtype=jnp.int32)
    counts = jnp.sum((eid[:, None] == experts[None, :]).astype(jnp.int32), axis=0)
    nblk_e = (counts + MOE_T - 1) // MOE_T
    bend = jnp.cumsum(nblk_e)
    bstart = bend - nblk_e
    start = jnp.cumsum(counts) - counts
    blk = jnp.arange(n_blocks, dtype=jnp.int32)
    owner = jnp.logical_and(blk[:, None] >= bstart[None, :], blk[:, None] < bend[None, :]).astype(jnp.int32)
    take = lambda v: jnp.sum(owner * v[None, :], axis=1)
    within = blk - take(bstart)
    nused = bend[-1]
    last_e = jnp.max(jnp.where(counts > 0, experts, 0))
    block_e = jnp.where(blk < nused, take(experts), last_e).astype(jnp.int32)
    rstart = jnp.where(blk < nused, take(start) + within * MOE_T, 0).astype(jnp.int32)
    nvalid = jnp.where(blk < nused, jnp.clip(take(counts) - within * MOE_T, 0, MOE_T), 0).astype(jnp.int32)
    nxt = take(bend)
    e_at_nxt = jnp.sum((blk[None, :] == nxt[:, None]).astype(jnp.int32) * block_e[None, :], axis=1)
    next_e = jnp.where(jnp.logical_and(blk < nused, nxt < nused), e_at_nxt, -1).astype(jnp.int32)
    return tok, order, rstart, nvalid, block_e, next_e, nused.astype(jnp.int32).reshape(1)


def _rope_tables(s, lc):
    t = np.arange(s - lc)
    row = (t // GRID_W).astype(np.float64)
    col = (t % GRID_W).astype(np.float64)
    nf = HD // 4
    inv = np.float32(ROPE_BASE) ** (-np.arange(nf, dtype=np.float32) / np.float32(nf))
    ang = np.concatenate([row[:, None] * inv, col[:, None] * inv], axis=-1)
    ang = np.concatenate([np.zeros((lc, HD // 2)), ang], axis=0)
    cos = np.repeat(np.cos(ang), 2, axis=-1)
    sin = np.repeat(np.sin(ang), 2, axis=-1) * np.tile([-1.0, 1.0], HD // 2)
    return jnp.asarray(np.tile(cos, (1, LANE // HD)), F32), jnp.asarray(np.tile(sin, (1, LANE // HD)), F32)


def _chan_dft():
    j = np.arange(FGD)
    ang = 2.0 * np.pi * ((j[:, None] * j[None, :]) % FGD) / FGD
    c = np.kron(np.eye(FG), np.cos(ang)) / np.sqrt(FGD)
    s = np.kron(np.eye(FG), np.sin(ang)) / np.sqrt(FGD)
    return jnp.asarray(np.concatenate([c, -s], axis=1), BF16)


def _pos_dft(n):
    k = np.arange(n)
    ang = 2.0 * np.pi * ((k[:, None] * k[None, :]) % n) / n
    return (jnp.asarray(np.cos(ang) / np.sqrt(n), BF16), jnp.asarray(np.sin(ang) / np.sqrt(n), BF16))


def _proj_weights(w_in):
    dup = lambda w: jnp.concatenate([w.reshape(D, N_KV, HD)] * 2, axis=-1).reshape(D, 2 * N_KV * HD)
    w_big = jnp.concatenate([w_in[:, 0:1024], dup(w_in[:, 1024:1280]), dup(w_in[:, 1280:1536]),
                             w_in[:, 4640:], w_in[:, 2048:2560], w_in[:, 3584:4608]], axis=1).astype(BF16)
    w_gate = jnp.pad(w_in[:, 4608:4640], ((0, 0), (0, LANE - 32))).astype(BF16)
    return w_big, w_gate


WT_T = 512


def _wt_kernel(w_ref, o_ref):
    scale = jnp.where(pl.program_id(0) == 0, ML_DK ** -0.5, 1.0)
    o_ref[...] = (jnp.transpose(w_ref[...]) * scale).astype(o_ref.dtype)


def qv_weights_t(w_in):
    w_qv = jnp.concatenate([w_in[:, 1536:2048], w_in[:, 2560:3584]], axis=1)
    return pl.pallas_call(
        _wt_kernel,
        out_shape=jax.ShapeDtypeStruct((QV_ROWS, D), BF16),
        grid=(QV_ROWS // WT_T, D // WT_T),
        in_specs=[pl.BlockSpec((WT_T, WT_T), lambda n, k: (k, n))],
        out_specs=pl.BlockSpec((WT_T, WT_T), lambda n, k: (n, k)),
        compiler_params=_cp(2, 16), name="qv_weights_t",
    )(w_qv)


def _qk_gain(qg, kg):
    qs = jnp.tile(qg * (HD ** -0.5 * LOG2E), N_HEADS)
    return jnp.concatenate([qs, jnp.tile(kg, 2 * N_KV)]).reshape(1, -1).astype(F32)


def kernel(x, c, ctx, c_ctx, w_mod, b_mod, norm1_g, norm2_g, w_in, q_norm_g, k_norm_g, attn_sink, ml_gate_b,
           ml_norm_g, w_br_attn, w_br_four, w_br_mlstm, b_gate, w_out, w_grp, b_grp, w_exp_router,
           b_exp_router, w1, w3, w2):
    B, T, _ = x.shape
    lc = ctx.shape[1]
    S = lc + T
    R = B * S
    depth = w_mod.shape[0]
    nct = lc // ROW_T
    assert lc % ROW_T == 0 and T % ROW_T == 0 and B < 8

    xa = jnp.concatenate([ctx, x], axis=1)
    cvec = jnp.concatenate([c, c_ctx[None], jnp.zeros((7 - B, D), F32)], axis=0)
    cos_t, sin_t = _rope_tables(S, lc)
    bd = jnp.asarray(np.kron(np.eye(LANE // HD), np.ones((HD, HD))), BF16)
    wc = _chan_dft()
    ct_lat, st_lat = _pos_dft(T)
    ct_ctx, st_ctx = _pos_dft(lc)
    nchunk = S // ML_L
    n_pairs = R * 2
    n_blocks = -(-n_pairs // MOE_T) + N_EXP

    mods = [modulation(cvec, w_mod, b_mod, l).reshape(8, 1, 6 * D) for l in range(depth)]
    h = norm_mod(xa, norm1_g[0], mods[0], 0, 1, nct)
    for l in range(depth):
        mod3 = mods[l]
        w_big, w_gate = _proj_weights(w_in[l])
        w_qv_t = qv_weights_t(w_in[l])
        h2d = h.reshape(R, D)
        proj = mm(h2d, w_big, BF16, 1536).reshape(B, S, C_TOT)
        graw = mm(h2d, w_gate, F32, LANE).reshape(B, S, LANE)

        qkn = qk_prep(proj, _qk_gain(q_norm_g[l], k_norm_g[l]), cos_t, sin_t, bd)
        a = attention(qkn, proj, attn_sink[l].astype(F32) * LOG2E, lc)

        f_ctx = dft_positions(ct_ctx, st_ctx, dft_channels(proj, wc, 0, lc))
        f_lat = dft_positions(ct_lat, st_lat, dft_channels(proj, wc, lc, T))
        f = jnp.concatenate([f_ctx, f_lat], axis=1)

        gbias = jnp.pad(ml_gate_b[l].reshape(1, 4 * ML_H).astype(F32), ((0, 0), (0, LANE - 4 * ML_H)))
        gproc = gate_prep(graw, gbias)[..., :4 * ML_H]
        g5 = gproc.reshape(B, S, 2, 2, ML_H // 2, 2).transpose(0, 4, 1, 2, 3, 5).reshape(B, ML_H // 2, S, 8)
        gates = jnp.pad(g5, ((0, 0), (0, 0), (0, 0), (0, LANE - 8)))
        gates_t = g5.reshape(B, ML_H // 2, nchunk, ML_L, 8).transpose(0, 1, 2, 4, 3)
        qv_t = mm_chunk_transposed(h, w_qv_t, ML_L)
        m = mlstm(proj, qv_t, gates, gates_t, ml_norm_g[l].reshape(1, -1).astype(F32), lc)

        y = merge(a.reshape(R, -1), f.reshape(R, -1), m.reshape(R, -1), proj.reshape(R, C_TOT),
                  w_br_attn[l].astype(BF16), w_br_four[l].astype(BF16), w_br_mlstm[l].astype(BF16),
                  b_gate[l].reshape(3, 1, D).astype(F32))
        wr = jnp.concatenate([w_grp[l], w_exp_router[l].reshape(D, N_EXP)], axis=1)
        wr = jnp.pad(wr, ((0, 0), (0, LANE - N_GRP - N_EXP))).astype(F32)
        wr_hi = wr.astype(BF16)
        wr_lo = (wr - wr_hi.astype(F32)).astype(BF16)
        br = jnp.pad(jnp.concatenate([b_grp[l], b_exp_router[l].reshape(N_EXP)]),
                     (0, LANE - N_GRP - N_EXP)).reshape(1, LANE).astype(F32)
        xa, hp, eid, wts = out_proj_router(y.reshape(B, S, D), w_out[l].astype(BF16), xa, norm2_g[l], mod3, 2, 3, 4,
                                           nct, wr_hi, wr_lo, br)
        plan = moe_plan(jnp.concatenate([eid[..., 0].reshape(R), eid[..., 1].reshape(R)]), n_blocks, R)
        y2 = expert_ffn(hp.reshape(R, D // 2), plan, w1, w3, w2, l, n_blocks, n_pairs).reshape(2, B, S, D // 2)
        if l == depth - 1:
            xa = combine(y2, xa, wts, mod3, 5, nct, True)
        else:
            xa, h = combine_norm(y2, xa, wts, mod3, 5, nct, norm1_g[l + 1], mods[l + 1], 0, 1)

    return xa
```

```python
import functools

import numpy as np
import jax
import jax.numpy as jnp
from jax import lax
from jax.experimental import pallas as pl
from jax.experimental.pallas import tpu as pltpu

F32 = jnp.float32
BF16 = jnp.bfloat16

D = 2048
N_HEADS, N_KV, HD = 16, 4, 64
GRID_W = 64
ROPE_BASE = 10000.0
EPS = 1e-6
ML_H, ML_DK, ML_DV = 8, 64, 128
ML_L = 256
FG, FGD = 4, 256
N_GRP, EPG, N_EXP, FF = 4, 8, 32, 1024
NEG = -1e30
LOG2E = 1.4426950408889634
LANE = 128
ROW_T = 256
MOE_T = 256

Q_W = N_HEADS * HD
KV_W = N_KV * HD
KD_W = 2 * KV_W
C_Q, C_F, C_GP = 0, 1024, 2048
C_MK, C_MO, C_K, C_V, C_TOT = 8192, 8704, 9728, 9984, 10240
QV_ROWS = ML_H * (ML_DK + ML_DV)


def _cp(n_axes, vmem_mb):
    return pltpu.CompilerParams(dimension_semantics=("arbitrary",) * n_axes,
                                vmem_limit_bytes=int(vmem_mb) * 2 ** 20)


def _pick(n, cands):
    for c in cands:
        if n % c == 0:
            return c
    raise ValueError(f"no tile for {n} in {cands}")


def _dot(a, b):
    return jnp.dot(a, b, preferred_element_type=F32)


def _dot_nt(a, b):
    return lax.dot_general(a, b, (((1,), (1,)), ((), ())), preferred_element_type=F32)


def _split(x):
    hi = x.astype(BF16)
    return hi, (x - hi.astype(F32)).astype(BF16)


def _dot3(a, w):
    ah, al = _split(a)
    wh, wl = _split(w)
    return _dot(ah, wh) + _dot(al, wh) + _dot(ah, wl)


def _mod_kernel(a_ref, w_ref, b_ref, o_ref):
    a = a_ref[...]
    a = a * jax.nn.sigmoid(a)
    o_ref[...] = _dot3(a, w_ref[...]) + b_ref[...]


def modulation(cvec, w_mod, b_mod, layer):
    depth, _, n = w_mod.shape
    tn = 512
    return pl.pallas_call(
        _mod_kernel,
        out_shape=jax.ShapeDtypeStruct((8, n), F32),
        grid=(n // tn,),
        in_specs=[pl.BlockSpec((8, D), lambda j: (0, 0)),
                  pl.BlockSpec((None, D, tn), lambda j: (layer, 0, j)),
                  pl.BlockSpec((None, 1, tn), lambda j: (layer, 0, j))],
        out_specs=pl.BlockSpec((8, tn), lambda j: (0, j)),
        compiler_params=_cp(1, 24), name="modulation",
    )(cvec, w_mod, b_mod.reshape(depth, 1, n))


def _mod_spec(k, nct, nb, off=0):
    return pl.BlockSpec((None, 1, D), lambda b, j: (jnp.where(j + off < nct, nb, b), 0, k))


def _norm_kernel(x_ref, g_ref, sc_ref, sh_ref, o_ref):
    x = x_ref[...]
    y = x * lax.rsqrt(jnp.mean(x * x, axis=-1, keepdims=True) + EPS) * g_ref[...]
    o_ref[...] = (y * (1.0 + sc_ref[...]) + sh_ref[...]).astype(o_ref.dtype)


def norm_mod(xa, g, mod3, k_sh, k_sc, nct):
    B, S, _ = xa.shape
    return pl.pallas_call(
        _norm_kernel,
        out_shape=jax.ShapeDtypeStruct((B, S, D), BF16),
        grid=(B, S // ROW_T),
        in_specs=[pl.BlockSpec((None, ROW_T, D), lambda b, j: (b, j, 0)),
                  pl.BlockSpec((1, D), lambda b, j: (0, 0)),
                  _mod_spec(k_sc, nct, B), _mod_spec(k_sh, nct, B)],
        out_specs=pl.BlockSpec((None, ROW_T, D), lambda b, j: (b, j, 0)),
        compiler_params=_cp(2, 24), name="norm_mod",
    )(xa, g.reshape(1, D), mod3, mod3)


def _mm_kernel(a_ref, b_ref, o_ref):
    o_ref[...] = _dot(a_ref[...], b_ref[...]).astype(o_ref.dtype)


def mm(a, b, out_dtype, tn):
    M, K = a.shape
    N = b.shape[1]
    tm = _pick(M, (1024, 512, 256))
    return pl.pallas_call(
        _mm_kernel,
        out_shape=jax.ShapeDtypeStruct((M, N), out_dtype),
        grid=(M // tm, N // tn),
        in_specs=[pl.BlockSpec((tm, K), lambda i, j: (i, 0)),
                  pl.BlockSpec((K, tn), lambda i, j: (0, j))],
        out_specs=pl.BlockSpec((tm, tn), lambda i, j: (i, j)),
        compiler_params=_cp(2, 40), name="mm",
    )(a, b)


def _mm_t_kernel(w_ref, h_ref, o_ref):
    o_ref[...] = _dot_nt(w_ref[...], h_ref[...]).astype(o_ref.dtype)


def mm_chunk_transposed(h, w_t, chunk):
    B, S, K = h.shape
    n = w_t.shape[0]
    return pl.pallas_call(
        _mm_t_kernel,
        out_shape=jax.ShapeDtypeStruct((B, S // chunk, n, chunk), BF16),
        grid=(B, S // chunk),
        in_specs=[pl.BlockSpec((n, K), lambda b, c: (0, 0)),
                  pl.BlockSpec((None, chunk, K), lambda b, c: (b, c, 0))],
        out_specs=pl.BlockSpec((None, None, n, chunk), lambda b, c: (b, c, 0, 0)),
        compiler_params=_cp(2, 40), name="mm_t",
    )(w_t, h)


def _prep_kernel(q_ref, kv_ref, g_ref, cos_ref, sin_ref, bd_ref, o_ref, v_ref):
    cosv = cos_ref[...]
    sinv = sin_ref[...]
    bd = bd_ref[...]
    lane = lax.broadcasted_iota(jnp.int32, cosv.shape, 1)
    even = (lane & 1) == 0
    lo_half = lane < HD

    def norm_rope(x, g):
        hi, lo = _split(x * x)
        ss = _dot(hi, bd) + _dot(lo, bd)
        y = x * lax.rsqrt(ss * (1.0 / HD) + EPS) * g
        sw = jnp.where(even, pltpu.roll(y, LANE - 1, 1), pltpu.roll(y, 1, 1))
        return y * cosv + sw * sinv

    def dup(y):
        r = pltpu.roll(y, HD, 1)
        return jnp.where(lo_half, y, r), jnp.where(lo_half, r, y)

    for s in range(Q_W // LANE):
        sl = slice(s * LANE, (s + 1) * LANE)
        o_ref[:, sl] = norm_rope(q_ref[:, sl].astype(F32), g_ref[:, sl]).astype(o_ref.dtype)
    for s in range(KV_W // LANE):
        sl = slice(s * LANE, (s + 1) * LANE)
        k2 = dup(norm_rope(kv_ref[:, sl].astype(F32), g_ref[:, Q_W + s * LANE:Q_W + (s + 1) * LANE]))
        v2 = dup(kv_ref[:, KV_W + s * LANE:KV_W + (s + 1) * LANE].astype(F32))
        for t in range(2):
            dst = slice((2 * s + t) * LANE, (2 * s + t + 1) * LANE)
            o_ref[:, Q_W + dst.start:Q_W + dst.stop] = k2[t].astype(o_ref.dtype)
            v_ref[:, dst] = v2[t].astype(v_ref.dtype)


def qk_prep(proj, gain, cos_t, sin_t, bd):
    B, S, _ = proj.shape
    return pl.pallas_call(
        _prep_kernel,
        out_shape=(jax.ShapeDtypeStruct((B, S, Q_W + KD_W), BF16), jax.ShapeDtypeStruct((B, S, KD_W), BF16)),
        grid=(B, S // ROW_T),
        in_specs=[pl.BlockSpec((None, ROW_T, Q_W), lambda b, j: (b, j, C_Q // Q_W)),
                  pl.BlockSpec((None, ROW_T, 2 * KV_W), lambda b, j: (b, j, C_K // (2 * KV_W))),
                  pl.BlockSpec((1, Q_W + KV_W), lambda b, j: (0, 0)),
                  pl.BlockSpec((ROW_T, LANE), lambda b, j: (j, 0)),
                  pl.BlockSpec((ROW_T, LANE), lambda b, j: (j, 0)),
                  pl.BlockSpec((LANE, LANE), lambda b, j: (0, 0))],
        out_specs=(pl.BlockSpec((None, ROW_T, Q_W + KD_W), lambda b, j: (b, j, 0)),
                   pl.BlockSpec((None, ROW_T, KD_W), lambda b, j: (b, j, 0))),
        compiler_params=_cp(2, 24), name="qk_prep",
    )(proj, proj, gain, cos_t, sin_t, bd)


AB = 128


def _attn_kernel(sink_ref, q_ref, kp_ref, kc_ref, kn_ref, kx_ref, vp_ref, vc_ref, vn_ref, vx_ref, o_ref,
                 *, nctb, nblk):
    n = pl.program_id(1)
    is_lat = n >= nctb
    iq = lax.broadcasted_iota(jnp.int32, (AB, AB), 0)
    ik = lax.broadcasted_iota(jnp.int32, (AB, AB), 1)
    ok_p = jnp.logical_and(ik >= iq, jnp.logical_and(is_lat, n - 1 >= nctb))
    ok_c = jnp.logical_and(ik >= 0, is_lat)
    ok_n = jnp.logical_and(ik <= iq, jnp.logical_and(is_lat, n + 1 <= nblk - 1))
    nloc = 3 * AB
    bias = jnp.concatenate([jnp.where(ok_p, 0.0, NEG), jnp.where(ok_c, 0.0, NEG), jnp.where(ok_n, 0.0, NEG)], axis=1)
    kall = jnp.concatenate([kp_ref[...], kc_ref[...], kn_ref[...], kx_ref[...]], axis=0)
    vall = jnp.concatenate([vp_ref[...], vc_ref[...], vn_ref[...], vx_ref[...]], axis=0)
    ones = jnp.ones((kall.shape[0], LANE), BF16)
    lane = lax.broadcasted_iota(jnp.int32, (AB, LANE), 1)
    lo_half = lane < HD
    hi_half = jnp.logical_not(lo_half)
    gq = N_HEADS // N_KV
    for g in range(N_KV):
        kg = kall[:, g * LANE:(g + 1) * LANE]
        vext = jnp.concatenate([vall[:, g * LANE:(g + 1) * LANE], ones], axis=1)
        qparts = []
        for jj in range(2):
            qs = q_ref[:, (2 * g + jj) * LANE:(2 * g + jj + 1) * LANE]
            qparts += [jnp.where(lo_half, qs, jnp.zeros_like(qs)), jnp.where(hi_half, qs, jnp.zeros_like(qs))]
        s = _dot_nt(jnp.concatenate(qparts, axis=0), kg)
        ps, sink_w = [], []
        for seg in range(gq):
            sl = s[seg * AB:(seg + 1) * AB]
            s_loc = sl[:, :nloc] + bias
            s_ctx = sl[:, nloc:]
            sk = sink_ref[gq * g + seg]
            m = jnp.maximum(jnp.maximum(jnp.max(s_loc, axis=-1, keepdims=True),
                                        jnp.max(s_ctx, axis=-1, keepdims=True)), sk)
            ps.append(jnp.concatenate([jnp.exp2(s_loc - m), jnp.exp2(s_ctx - m)], axis=1).astype(BF16))
            sink_w.append(jnp.exp2(sk - m))
        o = _dot(jnp.concatenate(ps, axis=0), vext)
        for jj in range(2):
            outs = []
            for half in range(2):
                seg = 2 * jj + half
                rows = slice(seg * AB, (seg + 1) * AB)
                outs.append(o[rows, :LANE] / (o[rows, LANE:] + sink_w[seg]))
            sl_out = slice((2 * g + jj) * LANE, (2 * g + jj + 1) * LANE)
            o_ref[:, sl_out] = jnp.where(lo_half, outs[0], outs[1]).astype(o_ref.dtype)


def attention(qkn, vdup, sink, lc):
    B, S, _ = qkn.shape
    nblk = S // AB
    nctb = lc // AB
    kcol, vcol = Q_W // KD_W, 0

    def rows(fn, col):
        return pl.BlockSpec((None, AB, KD_W), lambda b, n: (b, fn(n), col))

    prev = lambda n: jnp.maximum(n - 1, 0)
    cur = lambda n: n
    nxt = lambda n: jnp.minimum(n + 1, nblk - 1)
    ctx = lambda col: pl.BlockSpec((None, lc, KD_W), lambda b, n: (b, 0, col))
    return pl.pallas_call(
        functools.partial(_attn_kernel, nctb=nctb, nblk=nblk),
        out_shape=jax.ShapeDtypeStruct((B, S, Q_W), BF16),
        grid=(B, nblk),
        in_specs=[pl.BlockSpec(memory_space=pltpu.SMEM),
                  pl.BlockSpec((None, AB, Q_W), lambda b, n: (b, n, 0)),
                  rows(prev, kcol), rows(cur, kcol), rows(nxt, kcol), ctx(kcol),
                  rows(prev, vcol), rows(cur, vcol), rows(nxt, vcol), ctx(vcol)],
        out_specs=pl.BlockSpec((None, AB, Q_W), lambda b, n: (b, n, 0)),
        compiler_params=_cp(2, 32), name="attention",
    )(sink, qkn, qkn, qkn, qkn, qkn, vdup, vdup, vdup, vdup)


def _dftc_kernel(u_ref, w_ref, o_ref):
    o_ref[...] = _dot(u_ref[...], w_ref[...]).astype(o_ref.dtype)


def dft_channels(proj, wc, row0, nrows):
    B = proj.shape[0]
    off = row0 // ROW_T
    return pl.pallas_call(
        _dftc_kernel,
        out_shape=jax.ShapeDtypeStruct((B, nrows, 2 * FG * FGD), BF16),
        grid=(B, nrows // ROW_T),
        in_specs=[pl.BlockSpec((None, ROW_T, FG * FGD), lambda b, j: (b, j + off, C_F // (FG * FGD))),
                  pl.BlockSpec((FG * FGD, 2 * FG * FGD), lambda b, j: (0, 0))],
        out_specs=pl.BlockSpec((None, ROW_T, 2 * FG * FGD), lambda b, j: (b, j, 0)),
        compiler_params=_cp(2, 32), name="dft_channels",
    )(proj, wc)


def _dftp_kernel(c_ref, s_ref, y1_ref, y2_ref, o_ref, acc_ref):
    k = pl.program_id(2)

    @pl.when(k == 0)
    def _():
        acc_ref[...] = jnp.zeros_like(acc_ref)

    acc_ref[...] += _dot(c_ref[...], y1_ref[...]) + _dot(s_ref[...], y2_ref[...])

    @pl.when(k == pl.num_programs(2) - 1)
    def _():
        o_ref[...] = acc_ref[...].astype(o_ref.dtype)


def dft_positions(ct, st, y):
    B, n, w2 = y.shape
    w = w2 // 2
    t = _pick(n, (1024, 512, 256))
    return pl.pallas_call(
        _dftp_kernel,
        out_shape=jax.ShapeDtypeStruct((B, n, w), BF16),
        grid=(B, n // t, n // t),
        in_specs=[pl.BlockSpec((t, t), lambda b, i, k: (i, k)),
                  pl.BlockSpec((t, t), lambda b, i, k: (i, k)),
                  pl.BlockSpec((None, t, w), lambda b, i, k: (b, k, 0)),
                  pl.BlockSpec((None, t, w), lambda b, i, k: (b, k, 1))],
        out_specs=pl.BlockSpec((None, t, w), lambda b, i, k: (b, i, 0)),
        scratch_shapes=[pltpu.VMEM((t, w), F32)],
        compiler_params=_cp(3, 40), name="dft_positions",
    )(ct, st, y, y)


def _log_sigmoid(x):
    return jnp.minimum(x, 0.0) - jnp.log(1.0 + jnp.exp(-jnp.abs(x)))


def _gate_kernel(g_ref, b_ref, o_ref):
    L = ML_L
    g = g_ref[...] + b_ref[...]
    lane = lax.broadcasted_iota(jnp.int32, (L, LANE), 1)
    is_f = (lane & ML_H) == ML_H
    is_bwd = (lane & (2 * ML_H)) == 2 * ML_H
    lf = jnp.where(is_f, _log_sigmoid(g), 0.0)
    it_r = lax.broadcasted_iota(jnp.int32, (L, L), 0)
    it_c = lax.broadcasted_iota(jnp.int32, (L, L), 1)
    hp = lax.Precision.HIGHEST
    cf = jnp.dot(jnp.where(it_c <= it_r, 1.0, 0.0).astype(F32), lf, precision=hp, preferred_element_type=F32)
    cb = jnp.dot(jnp.where(it_c >= it_r, 1.0, 0.0).astype(F32), lf, precision=hp, preferred_element_type=F32)
    o_ref[...] = jnp.where(is_f, jnp.where(is_bwd, cb, cf), g)


def gate_prep(graw, bias):
    B, S, _ = graw.shape
    return pl.pallas_call(
        _gate_kernel,
        out_shape=jax.ShapeDtypeStruct((B, S, LANE), F32),
        grid=(B, S // ML_L),
        in_specs=[pl.BlockSpec((None, ML_L, LANE), lambda b, c: (b, c, 0)),
                  pl.BlockSpec((1, LANE), lambda b, c: (0, 0))],
        out_specs=pl.BlockSpec((None, ML_L, LANE), lambda b, c: (b, c, 0)),
        compiler_params=_cp(2, 16), name="gate_prep",
    )(graw, bias)


def _mlstm_kernel(k_ref, qt_ref, vt_ref, mo_ref, g_ref, gt_ref, ng_ref, o_ref,
                  hf_ref, hb_ref, c_ref, m_ref, *, nctc, nchunk):
    L = ML_L
    c_ref[...] = jnp.zeros_like(c_ref)
    m_ref[...] = jnp.zeros_like(m_ref)
    key_i = lax.broadcasted_iota(jnp.int32, (L, L), 0)
    qry_i = lax.broadcasted_iota(jnp.int32, (L, L), 1)
    tri = (key_i <= qry_i, key_i >= qry_i)
    lane = lax.broadcasted_iota(jnp.int32, (L, LANE), 1)
    head_lanes = (lane < ML_DK, lane >= ML_DK)
    ones_t = jnp.ones((ML_DV, L), BF16)
    h_refs = (hf_ref, hb_ref)

    def step(it, carry):
        stores = []
        for d in range(2):
            if d == 0:
                c = it
            else:
                c = jnp.where(it < nctc, nctc - 1 - it, nchunk - 1 - (it - nctc))
            r0 = pl.multiple_of(c * L, L)
            gc = g_ref[pl.ds(r0, L), :]
            gr = gt_ref[c]
            kp = k_ref[pl.ds(r0, L), :]
            qt = qt_ref[c]
            last = L - 1 if d == 0 else 0
            for hh in range(2):
                icol = d * 4 + hh
                fcol = d * 4 + 2 + hh
                sidx = d * 2 + hh
                key_term = gc[:, icol:icol + 1] - gc[:, fcol:fcol + 1]
                b_row = gr[fcol:fcol + 1, :]
                i_row = gr[icol:icol + 1, :]
                m_prev = m_ref[sidx:sidx + 1, 0:1]
                logd = jnp.where(tri[d], key_term + b_row, NEG)
                inter = b_row + m_prev
                m_t = jnp.maximum(inter, jnp.max(logd, axis=0, keepdims=True))
                km = jnp.where(head_lanes[hh], kp, jnp.zeros_like(kp))
                p_t = (_dot(km, qt) * jnp.exp(logd - m_t)).astype(BF16)
                q_in = (qt.astype(F32) * jnp.exp(inter - m_t)).astype(BF16)
                vext_t = jnp.concatenate([vt_ref[c, hh * ML_DV:(hh + 1) * ML_DV, :], ones_t], axis=0)
                c_t = c_ref[sidx]
                tot = _dot(jnp.concatenate([vext_t, c_t.astype(BF16)], axis=1),
                           jnp.concatenate([p_t, q_in], axis=0))
                h_t = tot[:ML_DV] / jnp.maximum(jnp.abs(tot[ML_DV:]), jnp.exp(-m_t))
                b_last = b_row[:, last:last + 1]
                log_w = b_last - b_row + i_row
                m_new = jnp.maximum(b_last + m_prev, jnp.max(log_w, axis=-1, keepdims=True))
                w_row = jnp.exp(log_w - m_new).astype(BF16)
                c_new = jnp.exp(b_last + m_prev - m_new) * c_t + _dot(vext_t * w_row, km)
                stores.append((d, hh, sidx, c, h_t, c_new, m_new))
        for d, hh, sidx, c, h_t, c_new, m_new in stores:
            h_refs[d][c, hh] = h_t
            c_ref[sidx] = c_new
            m_ref[sidx:sidx + 1, :] = jnp.broadcast_to(m_new, (1, LANE))
        return carry

    lax.fori_loop(0, nchunk, step, 0)

    def out_step(i, carry):
        r0 = pl.multiple_of(i * L, L)
        for hh in range(2):
            hs = slice(hh * ML_DV, (hh + 1) * ML_DV)
            h = jnp.transpose(hf_ref[i, hh] + hb_ref[i, hh])
            hn = h * lax.rsqrt(jnp.mean(h * h, axis=-1, keepdims=True) + EPS) * ng_ref[:, hs]
            og = jax.nn.sigmoid(mo_ref[pl.ds(r0, L), hs].astype(F32))
            o_ref[pl.ds(r0, L), hs] = (og * hn).astype(o_ref.dtype)
        return carry

    lax.fori_loop(0, nchunk, out_step, 0)


def mlstm(proj, qv_t, gates, gates_t, norm_g, lc):
    B, S, _ = proj.shape
    nchunk = S // ML_L
    pw = 2 * ML_DK
    vw = 2 * ML_DV
    return pl.pallas_call(
        functools.partial(_mlstm_kernel, nctc=lc // ML_L, nchunk=nchunk),
        out_shape=jax.ShapeDtypeStruct((B, S, ML_H * ML_DV), BF16),
        grid=(B, ML_H // 2),
        in_specs=[pl.BlockSpec((None, S, pw), lambda b, p: (b, 0, C_MK // pw + p)),
                  pl.BlockSpec((None, nchunk, pw, ML_L), lambda b, p: (b, 0, p, 0)),
                  pl.BlockSpec((None, nchunk, vw, ML_L), lambda b, p: (b, 0, ML_H * ML_DK // vw + p, 0)),
                  pl.BlockSpec((None, S, vw), lambda b, p: (b, 0, C_MO // vw + p)),
                  pl.BlockSpec((None, None, S, LANE), lambda b, p: (b, p, 0, 0)),
                  pl.BlockSpec((None, None, nchunk, 8, ML_L), lambda b, p: (b, p, 0, 0, 0)),
                  pl.BlockSpec((1, vw), lambda b, p: (0, p))],
        out_specs=pl.BlockSpec((None, S, vw), lambda b, p: (b, 0, p)),
        scratch_shapes=[pltpu.VMEM((nchunk, 2, ML_DV, ML_L), F32), pltpu.VMEM((nchunk, 2, ML_DV, ML_L), F32),
                        pltpu.VMEM((4, 2 * ML_DV, 2 * ML_DK), F32), pltpu.VMEM((8, LANE), F32)],
        compiler_params=_cp(2, 48), name="mlstm",
    )(proj, qv_t, qv_t, proj, gates, gates_t, norm_g)


def _merge_kernel(a_ref, f_ref, m_ref, wa_ref, wf_ref, wm_ref, ga_ref, gf_ref, gm_ref,
                  ba_ref, bf_ref, bm_ref, o_ref):
    def br(x_ref, w_ref, g_ref, b_ref):
        return jax.nn.sigmoid(g_ref[...].astype(F32) + b_ref[...]) * _dot(x_ref[...], w_ref[...])

    y = br(a_ref, wa_ref, ga_ref, ba_ref) + br(f_ref, wf_ref, gf_ref, bf_ref) + br(m_ref, wm_ref, gm_ref, bm_ref)
    o_ref[...] = y.astype(o_ref.dtype)


def merge(a, f, m, proj, wa, wf, wm, b_gate):
    R, kin = a.shape
    tm = _pick(R, (512, 256))
    tn = 1024
    x_spec = pl.BlockSpec((tm, kin), lambda j, i: (i, 0))
    w_spec = pl.BlockSpec((kin, tn), lambda j, i: (0, j))
    gp = lambda br: pl.BlockSpec((tm, tn), lambda j, i: (i, (C_GP + br * D) // tn + j))
    bg = lambda br: pl.BlockSpec((None, 1, tn), lambda j, i: (br, 0, j))
    return pl.pallas_call(
        _merge_kernel,
        out_shape=jax.ShapeDtypeStruct((R, D), BF16),
        grid=(D // tn, R // tm),
        in_specs=[x_spec, x_spec, x_spec, w_spec, w_spec, w_spec, gp(0), gp(1), gp(2), bg(0), bg(1), bg(2)],
        out_specs=pl.BlockSpec((tm, tn), lambda j, i: (i, j)),
        compiler_params=_cp(2, 40), name="merge",
    )(a, f, m, wa, wf, wm, proj, proj, proj, b_gate, b_gate, b_gate)


def _pack2(lo, hi):
    lo_b = lax.bitcast_convert_type(lo.astype(BF16).astype(F32), jnp.uint32)
    hi_b = lax.bitcast_convert_type(hi.astype(BF16).astype(F32), jnp.uint32)
    return (lo_b >> 16) | (hi_b & jnp.uint32(0xFFFF0000))


def _unpack2(w):
    lo = lax.bitcast_convert_type(w << 16, F32)
    hi = lax.bitcast_convert_type(w & jnp.uint32(0xFFFF0000), F32)
    return lo, hi


def _outproj_router_kernel(y_ref, w_ref, x_ref, g1_ref, g_ref, sc_ref, sh_ref, wh_ref, wl_ref, b_ref,
                           xo_ref, hp_ref, eid_ref, wt_ref):
    x = x_ref[...] + g1_ref[...] * _dot(y_ref[...], w_ref[...])
    xo_ref[...] = x
    y = x * lax.rsqrt(jnp.mean(x * x, axis=-1, keepdims=True) + EPS) * g_ref[...]
    y = y * (1.0 + sc_ref[...]) + sh_ref[...]
    hp_ref[...] = _pack2(y[:, :D // 2], y[:, D // 2:])
    yh, yl = _split(y)
    logits = _dot(yh, wh_ref[...]) + _dot(yl, wh_ref[...]) + _dot(yh, wl_ref[...]) + b_ref[...]
    lane = lax.broadcasted_iota(jnp.int32, logits.shape, 1)
    lanef = lane.astype(F32)
    big = float(LANE)
    glog = jnp.where(lane < N_GRP, logits, NEG)
    gmax = jnp.max(glog, axis=-1, keepdims=True)
    gsel = jnp.min(jnp.where(glog == gmax, lanef, big), axis=-1, keepdims=True)
    pg = 1.0 / jnp.sum(jnp.exp(glog - gmax), axis=-1, keepdims=True)
    lo = N_GRP + EPG * gsel
    el = jnp.where(jnp.logical_and(lanef >= lo, lanef < lo + EPG), logits, NEG)
    v1 = jnp.max(el, axis=-1, keepdims=True)
    i1 = jnp.min(jnp.where(el == v1, lanef, big), axis=-1, keepdims=True)
    el2 = jnp.where(lanef == i1, NEG, el)
    v2 = jnp.max(el2, axis=-1, keepdims=True)
    i2 = jnp.min(jnp.where(el2 == v2, lanef, big), axis=-1, keepdims=True)
    e = jnp.exp(v2 - v1)
    w1 = pg / (1.0 + e)
    w2 = pg * e / (1.0 + e)
    eid_ref[...] = jnp.where(lane == 0, i1 - N_GRP, jnp.where(lane == 1, i2 - N_GRP, 0.0)).astype(jnp.int32)
    wt_ref[...] = jnp.where(lane == 0, w1, jnp.where(lane == 1, w2, 0.0))


def out_proj_router(y, w_out, xa, g, mod3, k_gate, k_sh, k_sc, nct, wr_hi, wr_lo, br):
    B, S, _ = xa.shape
    tile = lambda w: pl.BlockSpec((None, ROW_T, w), lambda b, j: (b, j, 0))
    const = lambda shape: pl.BlockSpec(shape, lambda b, j: (0, 0))
    return pl.pallas_call(
        _outproj_router_kernel,
        out_shape=(jax.ShapeDtypeStruct((B, S, D), F32),
                   jax.ShapeDtypeStruct((B, S, D // 2), jnp.uint32),
                   jax.ShapeDtypeStruct((B, S, LANE), jnp.int32),
                   jax.ShapeDtypeStruct((B, S, LANE), F32)),
        grid=(B, S // ROW_T),
        in_specs=[tile(D), const((D, D)), tile(D), _mod_spec(k_gate, nct, B),
                  const((1, D)), _mod_spec(k_sc, nct, B), _mod_spec(k_sh, nct, B),
                  const((D, LANE)), const((D, LANE)), const((1, LANE))],
        out_specs=(tile(D), tile(D // 2), tile(LANE), tile(LANE)),
        compiler_params=_cp(2, 48), name="out_proj_router",
    )(y, w_out, xa, mod3, g.reshape(1, D), mod3, mod3, wr_hi, wr_lo, br)


DMA_UNROLL = 8


def _expert_kernel(tok_ref, order_ref, rstart_ref, nvalid_ref, be_ref, nexte_ref, nused_ref,
                   x_hbm, w1_hbm, w3_hbm, w2_hbm, y_hbm,
                   xbuf, ybuf, st1, st3, st2, wb1, wb3, wb2, gsem, ssem, wsem, *, layer):
    i = pl.program_id(0)
    nused = nused_ref[0]
    last = pl.num_programs(0) - 1
    used = i < nused
    weights = ((w1_hbm, st1, wb1), (w3_hbm, st3, wb3), (w2_hbm, st2, wb2))

    def gather_rows(blk):
        slot = blk & 1
        rs = rstart_ref[blk]

        def group(k, carry):
            for u in range(DMA_UNROLL):
                src = x_hbm.at[pl.ds(tok_ref[rs + k * DMA_UNROLL + u], 1), :]
                pltpu.make_async_copy(src, xbuf.at[slot, k, pl.ds(u, 1), :], gsem.at[slot]).start(priority=u % 2)
            return carry

        lax.fori_loop(0, MOE_T // DMA_UNROLL, group, 0)

    def scatter_rows(blk, wait):
        slot = blk & 1
        rs = rstart_ref[blk]
        nv = nvalid_ref[blk]

        def row(k, u, r, prio):
            dst = y_hbm.at[pl.ds(0 if wait else order_ref[rs + r], 1), :]
            cp = pltpu.make_async_copy(ybuf.at[slot, k, pl.ds(u, 1), :], dst, ssem.at[slot])
            if wait:
                cp.wait()
            else:
                cp.start(priority=prio)

        def group(k, carry):
            if wait:
                pltpu.make_async_copy(ybuf.at[slot, k], y_hbm.at[pl.ds(0, DMA_UNROLL), :], ssem.at[slot]).wait()
            else:
                for u in range(DMA_UNROLL):
                    row(k, u, k * DMA_UNROLL + u, u % 2)
            return carry

        def tail(r, carry):
            row(lax.shift_right_logical(r, 3), r & (DMA_UNROLL - 1), r, 0)
            return carry

        full = nv // DMA_UNROLL
        lax.fori_loop(0, full, group, 0)
        lax.fori_loop(full * DMA_UNROLL, nv, tail, 0)

    def fetch_weights(e):
        for k, (hbm, st, _) in enumerate(weights):
            pltpu.make_async_copy(hbm.at[layer, e], st, wsem.at[k]).start()

    @pl.when(i == 0)
    def _():
        fetch_weights(be_ref[0])
        gather_rows(0)

    @pl.when(i + 1 < nused)
    def _():
        gather_rows(i + 1)

    @pl.when(jnp.logical_and(i >= 2, i - 2 < nused))
    def _():
        scatter_rows(i - 2, True)

    @pl.when(used)
    def _():
        e = be_ref[i]

        @pl.when(jnp.logical_or(i == 0, be_ref[jnp.maximum(i - 1, 0)] != e))
        def _():
            for k, (hbm, st, wb) in enumerate(weights):
                pltpu.make_async_copy(hbm.at[layer, 0], st, wsem.at[k]).wait()
                wb[...] = st[...].astype(BF16)

            @pl.when(nexte_ref[i] >= 0)
            def _():
                fetch_weights(nexte_ref[i])

        slot = i & 1
        pltpu.make_async_copy(xbuf.at[slot], xbuf.at[slot], gsem.at[slot]).wait()
        xl, xh = _unpack2(xbuf[slot].reshape(MOE_T, D // 2))
        xl = xl.astype(BF16)
        xh = xh.astype(BF16)
        half = D // 2
        a = _dot(xl, wb1[:half, :]) + _dot(xh, wb1[half:, :])
        b = _dot(xl, wb3[:half, :]) + _dot(xh, wb3[half:, :])
        hmid = (a * jax.nn.sigmoid(a) * b).astype(BF16)
        y = _dot(hmid, wb2[...])
        ybuf[slot] = _pack2(y[:, :half], y[:, half:]).reshape(MOE_T // DMA_UNROLL, DMA_UNROLL, half)
        scatter_rows(i, False)

    @pl.when(i == last)
    def _():
        @pl.when(jnp.logical_and(i >= 1, i - 1 < nused))
        def _():
            scatter_rows(i - 1, True)

        @pl.when(used)
        def _():
            scatter_rows(i, True)


def expert_ffn(hp, plan, w1, w3, w2, layer, n_blocks, n_pairs):
    w = hp.shape[1]
    hbm = pl.BlockSpec(memory_space=pl.ANY)
    grid_spec = pltpu.PrefetchScalarGridSpec(
        num_scalar_prefetch=7, grid=(n_blocks,),
        in_specs=[hbm, hbm, hbm, hbm],
        out_specs=hbm,
        scratch_shapes=[pltpu.VMEM((2, MOE_T // DMA_UNROLL, DMA_UNROLL, w), jnp.uint32),
                        pltpu.VMEM((2, MOE_T // DMA_UNROLL, DMA_UNROLL, w), jnp.uint32),
                        pltpu.VMEM((D, FF), F32), pltpu.VMEM((D, FF), F32), pltpu.VMEM((FF, D), F32),
                        pltpu.VMEM((D, FF), BF16), pltpu.VMEM((D, FF), BF16), pltpu.VMEM((FF, D), BF16),
                        pltpu.SemaphoreType.DMA((2,)), pltpu.SemaphoreType.DMA((2,)),
                        pltpu.SemaphoreType.DMA((3,))])
    return pl.pallas_call(
        functools.partial(_expert_kernel, layer=layer),
        out_shape=jax.ShapeDtypeStruct((n_pairs, w), jnp.uint32),
        grid_spec=grid_spec, compiler_params=_cp(1, 56), name="moe_experts",
    )(*plan, hp, w1, w3, w2)


def _combine(ya_ref, yb_ref, x_ref, wt_ref, g_ref):
    w0 = wt_ref[:, 0:1]
    w1 = wt_ref[:, 1:2]
    al, ah = _unpack2(ya_ref[...])
    bl, bh = _unpack2(yb_ref[...])
    half = D // 2
    g = g_ref[...]
    return (x_ref[:, :half] + g[:, :half] * (w0 * al + w1 * bl),
            x_ref[:, half:] + g[:, half:] * (w0 * ah + w1 * bh))


def _combine_kernel(ya_ref, yb_ref, x_ref, wt_ref, g_ref, o_ref):
    lo, hi = _combine(ya_ref, yb_ref, x_ref, wt_ref, g_ref)
    o_ref[:, :D // 2] = lo
    o_ref[:, D // 2:] = hi


def _combine_norm_kernel(ya_ref, yb_ref, x_ref, wt_ref, g_ref, ng_ref, sc_ref, sh_ref, o_ref, h_ref):
    lo, hi = _combine(ya_ref, yb_ref, x_ref, wt_ref, g_ref)
    o_ref[:, :D // 2] = lo
    o_ref[:, D // 2:] = hi
    x = o_ref[...]
    y = x * lax.rsqrt(jnp.mean(x * x, axis=-1, keepdims=True) + EPS) * ng_ref[...]
    h_ref[...] = (y * (1.0 + sc_ref[...]) + sh_ref[...]).astype(h_ref.dtype)


def combine(y2, xa, wts, mod3, k_gate, nct, latent_only):
    B, S, _ = xa.shape
    w = y2.shape[-1]
    off = nct if latent_only else 0
    tile = lambda wd: pl.BlockSpec((None, ROW_T, wd), lambda b, j: (b, j + off, 0))
    pick = lambda k: pl.BlockSpec((None, None, ROW_T, w), lambda b, j: (k, b, j + off, 0))
    return pl.pallas_call(
        _combine_kernel,
        out_shape=jax.ShapeDtypeStruct((B, S - off * ROW_T, D), F32),
        grid=(B, S // ROW_T - off),
        in_specs=[pick(0), pick(1), tile(D), tile(LANE), _mod_spec(k_gate, nct, B, off)],
        out_specs=pl.BlockSpec((None, ROW_T, D), lambda b, j: (b, j, 0)),
        compiler_params=_cp(2, 32), name="moe_combine",
    )(y2, y2, xa, wts, mod3)


def combine_norm(y2, xa, wts, mod3, k_gate, nct, g_next, mod3_next, k_sh, k_sc):
    B, S, _ = xa.shape
    w = y2.shape[-1]
    tile = lambda wd: pl.BlockSpec((None, ROW_T, wd), lambda b, j: (b, j, 0))
    pick = lambda k: pl.BlockSpec((None, None, ROW_T, w), lambda b, j: (k, b, j, 0))
    return pl.pallas_call(
        _combine_norm_kernel,
        out_shape=(jax.ShapeDtypeStruct((B, S, D), F32), jax.ShapeDtypeStruct((B, S, D), BF16)),
        grid=(B, S // ROW_T),
        in_specs=[pick(0), pick(1), tile(D), tile(LANE), _mod_spec(k_gate, nct, B),
                  pl.BlockSpec((1, D), lambda b, j: (0, 0)), _mod_spec(k_sc, nct, B), _mod_spec(k_sh, nct, B)],
        out_specs=(tile(D), tile(D)),
        compiler_params=_cp(2, 32), name="moe_combine_norm",
    )(y2, y2, xa, wts, mod3, g_next.reshape(1, D), mod3_next, mod3_next)


def moe_plan(eid, n_blocks, n_tok):
    order = jnp.argsort(eid).astype(jnp.int32)
    tok = jnp.pad(jnp.where(order >= n_tok, order - n_tok, order), (0, MOE_T))
    experts = jnp.arange(N_EXP, dtype=jnp.int32)
    counts = jnp.sum((eid[:, None] == experts[None, :]).astype(jnp.int32), axis=0)
    nblk_e = (counts + MOE_T - 1) // MOE_T
    bend = jnp.cumsum(nblk_e)
    bstart = bend - nblk_e
    start = jnp.cumsum(counts) - counts
    blk = jnp.arange(n_blocks, dtype=jnp.int32)
    owner = jnp.logical_and(blk[:, None] >= bstart[None, :], blk[:, None] < bend[None, :]).astype(jnp.int32)
    take = lambda v: jnp.sum(owner * v[None, :], axis=1)
    within = blk - take(bstart)
    nused = bend[-1]
    last_e = jnp.max(jnp.where(counts > 0, experts, 0))
    block_e = jnp.where(blk < nused, take(experts), last_e).astype(jnp.int32)
    rstart = jnp.where(blk < nused, take(start) + within * MOE_T, 0).astype(jnp.int32)
    nvalid = jnp.where(blk < nused, jnp.clip(take(counts) - within * MOE_T, 0, MOE_T), 0).astype(jnp.int32)
    nxt = take(bend)
    e_at_nxt = jnp.sum((blk[None, :] == nxt[:, None]).astype(jnp.int32) * block_e[None, :], axis=1)
    next_e = jnp.where(jnp.logical_and(blk < nused, nxt < nused), e_at_nxt, -1).astype(jnp.int32)
    return tok, order, rstart, nvalid, block_e, next_e, nused.astype(jnp.int32).reshape(1)


def _rope_tables(s, lc):
    t = np.arange(s - lc)
    row = (t // GRID_W).astype(np.float64)
    col = (t % GRID_W).astype(np.float64)
    nf = HD // 4
    inv = np.float32(ROPE_BASE) ** (-np.arange(nf, dtype=np.float32) / np.float32(nf))
    ang = np.concatenate([row[:, None] * inv, col[:, None] * inv], axis=-1)
    ang = np.concatenate([np.zeros((lc, HD // 2)), ang], axis=0)
    cos = np.repeat(np.cos(ang), 2, axis=-1)
    sin = np.repeat(np.sin(ang), 2, axis=-1) * np.tile([-1.0, 1.0], HD // 2)
    return jnp.asarray(np.tile(cos, (1, LANE // HD)), F32), jnp.asarray(np.tile(sin, (1, LANE // HD)), F32)


def _chan_dft():
    j = np.arange(FGD)
    ang = 2.0 * np.pi * ((j[:, None] * j[None, :]) % FGD) / FGD
    c = np.kron(np.eye(FG), np.cos(ang)) / np.sqrt(FGD)
    s = np.kron(np.eye(FG), np.sin(ang)) / np.sqrt(FGD)
    return jnp.asarray(np.concatenate([c, -s], axis=1), BF16)


def _pos_dft(n):
    k = np.arange(n)
    ang = 2.0 * np.pi * ((k[:, None] * k[None, :]) % n) / n
    return (jnp.asarray(np.cos(ang) / np.sqrt(n), BF16), jnp.asarray(np.sin(ang) / np.sqrt(n), BF16))


def _proj_weights(w_in):
    w_big = jnp.concatenate([w_in[:, 0:1024], w_in[:, 4640:], w_in[:, 2048:2560], w_in[:, 3584:4608],
                             w_in[:, 1024:1536]], axis=1).astype(BF16)
    w_gate = jnp.pad(w_in[:, 4608:4640], ((0, 0), (0, LANE - 32))).astype(BF16)
    return w_big, w_gate


WT_T = 512


def _wt_kernel(w_ref, o_ref):
    scale = jnp.where(pl.program_id(0) == 0, ML_DK ** -0.5, 1.0)
    o_ref[...] = (jnp.transpose(w_ref[...].astype(F32)) * scale).astype(o_ref.dtype)


def qv_weights_t(w_in):
    w_qv = jnp.concatenate([w_in[:, 1536:2048], w_in[:, 2560:3584]], axis=1).astype(BF16)
    return pl.pallas_call(
        _wt_kernel,
        out_shape=jax.ShapeDtypeStruct((QV_ROWS, D), BF16),
        grid=(QV_ROWS // WT_T, D // WT_T),
        in_specs=[pl.BlockSpec((WT_T, WT_T), lambda n, k: (k, n))],
        out_specs=pl.BlockSpec((WT_T, WT_T), lambda n, k: (n, k)),
        compiler_params=_cp(2, 16), name="qv_weights_t",
    )(w_qv)


def _qk_gain(qg, kg):
    qs = jnp.tile(qg * (HD ** -0.5 * LOG2E), N_HEADS)
    return jnp.concatenate([qs, jnp.tile(kg, N_KV)]).reshape(1, -1).astype(F32)


def kernel(x, c, ctx, c_ctx, w_mod, b_mod, norm1_g, norm2_g, w_in, q_norm_g, k_norm_g, attn_sink, ml_gate_b,
           ml_norm_g, w_br_attn, w_br_four, w_br_mlstm, b_gate, w_out, w_grp, b_grp, w_exp_router,
           b_exp_router, w1, w3, w2):
    B, T, _ = x.shape
    lc = ctx.shape[1]
    S = lc + T
    R = B * S
    depth = w_mod.shape[0]
    nct = lc // ROW_T
    assert lc % ROW_T == 0 and T % ROW_T == 0 and B < 8

    xa = jnp.concatenate([ctx, x], axis=1)
    cvec = jnp.concatenate([c, c_ctx[None], jnp.zeros((7 - B, D), F32)], axis=0)
    cos_t, sin_t = _rope_tables(S, lc)
    bd = jnp.asarray(np.kron(np.eye(LANE // HD), np.ones((HD, HD))), BF16)
    wc = _chan_dft()
    ct_lat, st_lat = _pos_dft(T)
    ct_ctx, st_ctx = _pos_dft(lc)
    nchunk = S // ML_L
    n_pairs = R * 2
    n_blocks = -(-n_pairs // MOE_T) + N_EXP

    mods = [modulation(cvec, w_mod, b_mod, l).reshape(8, 1, 6 * D) for l in range(depth)]
    h = norm_mod(xa, norm1_g[0], mods[0], 0, 1, nct)
    for l in range(depth):
        mod3 = mods[l]
        w_big, w_gate = _proj_weights(w_in[l])
        w_qv_t = qv_weights_t(w_in[l])
        h2d = h.reshape(R, D)
        proj = mm(h2d, w_big, BF16, 2048).reshape(B, S, C_TOT)
        graw = mm(h2d, w_gate, F32, LANE).reshape(B, S, LANE)

        qkn, vdup = qk_prep(proj, _qk_gain(q_norm_g[l], k_norm_g[l]), cos_t, sin_t, bd)
        a = attention(qkn, vdup, attn_sink[l].astype(F32) * LOG2E, lc)

        f_ctx = dft_positions(ct_ctx, st_ctx, dft_channels(proj, wc, 0, lc))
        f_lat = dft_positions(ct_lat, st_lat, dft_channels(proj, wc, lc, T))
        f = jnp.concatenate([f_ctx, f_lat], axis=1)

        gbias = jnp.pad(ml_gate_b[l].reshape(1, 4 * ML_H).astype(F32), ((0, 0), (0, LANE - 4 * ML_H)))
        gproc = gate_prep(graw, gbias)[..., :4 * ML_H]
        g5 = gproc.reshape(B, S, 2, 2, ML_H // 2, 2).transpose(0, 4, 1, 2, 3, 5).reshape(B, ML_H // 2, S, 8)
        gates = jnp.pad(g5, ((0, 0), (0, 0), (0, 0), (0, LANE - 8)))
        gates_t = g5.reshape(B, ML_H // 2, nchunk, ML_L, 8).transpose(0, 1, 2, 4, 3)
        qv_t = mm_chunk_transposed(h, w_qv_t, ML_L)
        m = mlstm(proj, qv_t, gates, gates_t, ml_norm_g[l].reshape(1, -1).astype(F32), lc)

        y = merge(a.reshape(R, -1), f.reshape(R, -1), m.reshape(R, -1), proj.reshape(R, C_TOT),
                  w_br_attn[l].astype(BF16), w_br_four[l].astype(BF16), w_br_mlstm[l].astype(BF16),
                  b_gate[l].reshape(3, 1, D).astype(F32))
        wr = jnp.concatenate([w_grp[l], w_exp_router[l].reshape(D, N_EXP)], axis=1)
        wr = jnp.pad(wr, ((0, 0), (0, LANE - N_GRP - N_EXP))).astype(F32)
        wr_hi = wr.astype(BF16)
        wr_lo = (wr - wr_hi.astype(F32)).astype(BF16)
        br = jnp.pad(jnp.concatenate([b_grp[l], b_exp_router[l].reshape(N_EXP)]),
                     (0, LANE - N_GRP - N_EXP)).reshape(1, LANE).astype(F32)
        xa, hp, eid, wts = out_proj_router(y.reshape(B, S, D), w_out[l].astype(BF16), xa, norm2_g[l], mod3, 2, 3, 4,
                                           nct, wr_hi, wr_lo, br)
        plan = moe_plan(jnp.concatenate([eid[..., 0].reshape(R), eid[..., 1].reshape(R)]), n_blocks, R)
        y2 = expert_ffn(hp.reshape(R, D // 2), plan, w1, w3, w2, l, n_blocks, n_pairs).reshape(2, B, S, D // 2)
        if l == depth - 1:
            xa = combine(y2, xa, wts, mod3, 5, nct, True)
        else:
            xa, h = combine_norm(y2, xa, wts, mod3, 5, nct, norm1_g[l + 1], mods[l + 1], 0, 1)

    return xa
```

```python
import functools

import numpy as np
import jax
import jax.numpy as jnp
from jax import lax
from jax.experimental import pallas as pl
from jax.experimental.pallas import tpu as pltpu

F32 = jnp.float32
BF16 = jnp.bfloat16

D = 2048
N_HEADS, N_KV, HD = 16, 4, 64
GRID_W = 64
ROPE_BASE = 10000.0
EPS = 1e-6
ML_H, ML_DK, ML_DV = 8, 64, 128
ML_L = 256
FG, FGD = 4, 256
N_GRP, EPG, N_EXP, FF = 4, 8, 32, 1024
NEG = -1e30
LOG2E = 1.4426950408889634
LANE = 128
ROW_T = 256
MOE_T = 256

Q_W = N_HEADS * HD
KV_W = N_KV * HD
KD_W = 2 * KV_W
C_Q, C_F, C_GP = 0, 1024, 2048
C_MK, C_MO, C_K, C_V, C_TOT = 8192, 8704, 9728, 9984, 10240
QV_ROWS = ML_H * (ML_DK + ML_DV)


def _cp(n_axes, vmem_mb):
    return pltpu.CompilerParams(dimension_semantics=("arbitrary",) * n_axes,
                                vmem_limit_bytes=int(vmem_mb) * 2 ** 20)


def _pick(n, cands):
    for c in cands:
        if n % c == 0:
            return c
    raise ValueError(f"no tile for {n} in {cands}")


def _dot(a, b):
    return jnp.dot(a, b, preferred_element_type=F32)


def _dot_nt(a, b):
    return lax.dot_general(a, b, (((1,), (1,)), ((), ())), preferred_element_type=F32)


def _split(x):
    hi = x.astype(BF16)
    return hi, (x - hi.astype(F32)).astype(BF16)


def _dot3(a, w):
    ah, al = _split(a)
    wh, wl = _split(w)
    return _dot(ah, wh) + _dot(al, wh) + _dot(ah, wl)


def _mod_kernel(a_ref, w_ref, b_ref, o_ref):
    a = a_ref[...]
    a = a * jax.nn.sigmoid(a)
    o_ref[...] = _dot3(a, w_ref[...]) + b_ref[...]


def modulation(cvec, w_mod, b_mod, layer):
    depth, _, n = w_mod.shape
    tn = 512
    return pl.pallas_call(
        _mod_kernel,
        out_shape=jax.ShapeDtypeStruct((8, n), F32),
        grid=(n // tn,),
        in_specs=[pl.BlockSpec((8, D), lambda j: (0, 0)),
                  pl.BlockSpec((None, D, tn), lambda j: (layer, 0, j)),
                  pl.BlockSpec((None, 1, tn), lambda j: (layer, 0, j))],
        out_specs=pl.BlockSpec((8, tn), lambda j: (0, j)),
        compiler_params=_cp(1, 24), name="modulation",
    )(cvec, w_mod, b_mod.reshape(depth, 1, n))


def _mod_spec(k, nct, nb, off=0):
    return pl.BlockSpec((None, 1, D), lambda b, j: (jnp.where(j + off < nct, nb, b), 0, k))


def _norm_kernel(x_ref, g_ref, sc_ref, sh_ref, o_ref):
    x = x_ref[...]
    y = x * lax.rsqrt(jnp.mean(x * x, axis=-1, keepdims=True) + EPS) * g_ref[...]
    o_ref[...] = (y * (1.0 + sc_ref[...]) + sh_ref[...]).astype(o_ref.dtype)


def norm_mod(xa, g, mod3, k_sh, k_sc, nct):
    B, S, _ = xa.shape
    return pl.pallas_call(
        _norm_kernel,
        out_shape=jax.ShapeDtypeStruct((B, S, D), BF16),
        grid=(B, S // ROW_T),
        in_specs=[pl.BlockSpec((None, ROW_T, D), lambda b, j: (b, j, 0)),
                  pl.BlockSpec((1, D), lambda b, j: (0, 0)),
                  _mod_spec(k_sc, nct, B), _mod_spec(k_sh, nct, B)],
        out_specs=pl.BlockSpec((None, ROW_T, D), lambda b, j: (b, j, 0)),
        compiler_params=_cp(2, 24), name="norm_mod",
    )(xa, g.reshape(1, D), mod3, mod3)


def _mm_kernel(a_ref, b_ref, o_ref):
    o_ref[...] = _dot(a_ref[...], b_ref[...]).astype(o_ref.dtype)


def mm(a, b, out_dtype, tn):
    M, K = a.shape
    N = b.shape[1]
    tm = _pick(M, (1024, 512, 256))
    return pl.pallas_call(
        _mm_kernel,
        out_shape=jax.ShapeDtypeStruct((M, N), out_dtype),
        grid=(M // tm, N // tn),
        in_specs=[pl.BlockSpec((tm, K), lambda i, j: (i, 0)),
                  pl.BlockSpec((K, tn), lambda i, j: (0, j))],
        out_specs=pl.BlockSpec((tm, tn), lambda i, j: (i, j)),
        compiler_params=_cp(2, 40), name="mm",
    )(a, b)


def _mm_t_kernel(w_ref, h_ref, o_ref):
    o_ref[...] = _dot_nt(w_ref[...], h_ref[...]).astype(o_ref.dtype)


def mm_chunk_transposed(h, w_t, chunk):
    B, S, K = h.shape
    n = w_t.shape[0]
    return pl.pallas_call(
        _mm_t_kernel,
        out_shape=jax.ShapeDtypeStruct((B, S // chunk, n, chunk), BF16),
        grid=(B, S // chunk),
        in_specs=[pl.BlockSpec((n, K), lambda b, c: (0, 0)),
                  pl.BlockSpec((None, chunk, K), lambda b, c: (b, c, 0))],
        out_specs=pl.BlockSpec((None, None, n, chunk), lambda b, c: (b, c, 0, 0)),
        compiler_params=_cp(2, 40), name="mm_t",
    )(w_t, h)


def _prep_kernel(q_ref, kv_ref, g_ref, cos_ref, sin_ref, bd_ref, o_ref, v_ref):
    cosv = cos_ref[...]
    sinv = sin_ref[...]
    bd = bd_ref[...]
    lane = lax.broadcasted_iota(jnp.int32, cosv.shape, 1)
    even = (lane & 1) == 0
    lo_half = lane < HD

    def norm_rope(x, g):
        hi, lo = _split(x * x)
        ss = _dot(hi, bd) + _dot(lo, bd)
        y = x * lax.rsqrt(ss * (1.0 / HD) + EPS) * g
        sw = jnp.where(even, pltpu.roll(y, LANE - 1, 1), pltpu.roll(y, 1, 1))
        return y * cosv + sw * sinv

    def dup(y):
        r = pltpu.roll(y, HD, 1)
        return jnp.where(lo_half, y, r), jnp.where(lo_half, r, y)

    for s in range(Q_W // LANE):
        sl = slice(s * LANE, (s + 1) * LANE)
        o_ref[:, sl] = norm_rope(q_ref[:, sl].astype(F32), g_ref[:, sl]).astype(o_ref.dtype)
    for s in range(KV_W // LANE):
        sl = slice(s * LANE, (s + 1) * LANE)
        k2 = dup(norm_rope(kv_ref[:, sl].astype(F32), g_ref[:, Q_W + s * LANE:Q_W + (s + 1) * LANE]))
        v2 = dup(kv_ref[:, KV_W + s * LANE:KV_W + (s + 1) * LANE].astype(F32))
        for t in range(2):
            dst = slice((2 * s + t) * LANE, (2 * s + t + 1) * LANE)
            o_ref[:, Q_W + dst.start:Q_W + dst.stop] = k2[t].astype(o_ref.dtype)
            v_ref[:, dst] = v2[t].astype(v_ref.dtype)


def qk_prep(proj, gain, cos_t, sin_t, bd):
    B, S, _ = proj.shape
    return pl.pallas_call(
        _prep_kernel,
        out_shape=(jax.ShapeDtypeStruct((B, S, Q_W + KD_W), BF16), jax.ShapeDtypeStruct((B, S, KD_W), BF16)),
        grid=(B, S // ROW_T),
        in_specs=[pl.BlockSpec((None, ROW_T, Q_W), lambda b, j: (b, j, C_Q // Q_W)),
                  pl.BlockSpec((None, ROW_T, 2 * KV_W), lambda b, j: (b, j, C_K // (2 * KV_W))),
                  pl.BlockSpec((1, Q_W + KV_W), lambda b, j: (0, 0)),
                  pl.BlockSpec((ROW_T, LANE), lambda b, j: (j, 0)),
                  pl.BlockSpec((ROW_T, LANE), lambda b, j: (j, 0)),
                  pl.BlockSpec((LANE, LANE), lambda b, j: (0, 0))],
        out_specs=(pl.BlockSpec((None, ROW_T, Q_W + KD_W), lambda b, j: (b, j, 0)),
                   pl.BlockSpec((None, ROW_T, KD_W), lambda b, j: (b, j, 0))),
        compiler_params=_cp(2, 24), name="qk_prep",
    )(proj, proj, gain, cos_t, sin_t, bd)


AB = 128


def _attn_kernel(sink_ref, q_ref, kp_ref, kc_ref, kn_ref, kx_ref, vp_ref, vc_ref, vn_ref, vx_ref, o_ref,
                 *, nctb, nblk):
    n = pl.program_id(1)
    is_lat = n >= nctb
    iq = lax.broadcasted_iota(jnp.int32, (AB, AB), 0)
    ik = lax.broadcasted_iota(jnp.int32, (AB, AB), 1)
    ok_p = jnp.logical_and(ik >= iq, jnp.logical_and(is_lat, n - 1 >= nctb))
    ok_c = jnp.logical_and(ik >= 0, is_lat)
    ok_n = jnp.logical_and(ik <= iq, jnp.logical_and(is_lat, n + 1 <= nblk - 1))
    nloc = 3 * AB
    bias = jnp.concatenate([jnp.where(ok_p, 0.0, NEG), jnp.where(ok_c, 0.0, NEG), jnp.where(ok_n, 0.0, NEG)], axis=1)
    kall = jnp.concatenate([kp_ref[...], kc_ref[...], kn_ref[...], kx_ref[...]], axis=0)
    vall = jnp.concatenate([vp_ref[...], vc_ref[...], vn_ref[...], vx_ref[...]], axis=0)
    ones = jnp.ones((kall.shape[0], LANE), BF16)
    lane = lax.broadcasted_iota(jnp.int32, (AB, LANE), 1)
    lo_half = lane < HD
    hi_half = jnp.logical_not(lo_half)
    gq = N_HEADS // N_KV
    for g in range(N_KV):
        kg = kall[:, g * LANE:(g + 1) * LANE]
        vext = jnp.concatenate([vall[:, g * LANE:(g + 1) * LANE], ones], axis=1)
        qparts = []
        for jj in range(2):
            qs = q_ref[:, (2 * g + jj) * LANE:(2 * g + jj + 1) * LANE]
            qparts += [jnp.where(lo_half, qs, jnp.zeros_like(qs)), jnp.where(hi_half, qs, jnp.zeros_like(qs))]
        s = _dot_nt(jnp.concatenate(qparts, axis=0), kg)
        ps, sink_w = [], []
        for seg in range(gq):
            sl = s[seg * AB:(seg + 1) * AB]
            s_loc = sl[:, :nloc] + bias
            s_ctx = sl[:, nloc:]
            sk = sink_ref[gq * g + seg]
            m = jnp.maximum(jnp.maximum(jnp.max(s_loc, axis=-1, keepdims=True),
                                        jnp.max(s_ctx, axis=-1, keepdims=True)), sk)
            ps.append(jnp.concatenate([jnp.exp2(s_loc - m), jnp.exp2(s_ctx - m)], axis=1).astype(BF16))
            sink_w.append(jnp.exp2(sk - m))
        o = _dot(jnp.concatenate(ps, axis=0), vext)
        for jj in range(2):
            outs = []
            for half in range(2):
                seg = 2 * jj + half
                rows = slice(seg * AB, (seg + 1) * AB)
                outs.append(o[rows, :LANE] / (o[rows, LANE:] + sink_w[seg]))
            sl_out = slice((2 * g + jj) * LANE, (2 * g + jj + 1) * LANE)
            o_ref[:, sl_out] = jnp.where(lo_half, outs[0], outs[1]).astype(o_ref.dtype)


def attention(qkn, vdup, sink, lc):
    B, S, _ = qkn.shape
    nblk = S // AB
    nctb = lc // AB
    kcol, vcol = Q_W // KD_W, 0

    def rows(fn, col):
        return pl.BlockSpec((None, AB, KD_W), lambda b, n: (b, fn(n), col))

    prev = lambda n: jnp.maximum(n - 1, 0)
    cur = lambda n: n
    nxt = lambda n: jnp.minimum(n + 1, nblk - 1)
    ctx = lambda col: pl.BlockSpec((None, lc, KD_W), lambda b, n: (b, 0, col))
    return pl.pallas_call(
        functools.partial(_attn_kernel, nctb=nctb, nblk=nblk),
        out_shape=jax.ShapeDtypeStruct((B, S, Q_W), BF16),
        grid=(B, nblk),
        in_specs=[pl.BlockSpec(memory_space=pltpu.SMEM),
                  pl.BlockSpec((None, AB, Q_W), lambda b, n: (b, n, 0)),
                  rows(prev, kcol), rows(cur, kcol), rows(nxt, kcol), ctx(kcol),
                  rows(prev, vcol), rows(cur, vcol), rows(nxt, vcol), ctx(vcol)],
        out_specs=pl.BlockSpec((None, AB, Q_W), lambda b, n: (b, n, 0)),
        compiler_params=_cp(2, 32), name="attention",
    )(sink, qkn, qkn, qkn, qkn, qkn, vdup, vdup, vdup, vdup)


def _dftc_kernel(u_ref, w_ref, o_ref):
    w = w_ref[...]
    for g in range(FG):
        r = _dot(u_ref[:, g * FGD:(g + 1) * FGD], w)
        o_ref[:, g * FGD:(g + 1) * FGD] = r[:, :FGD].astype(o_ref.dtype)
        o_ref[:, (FG + g) * FGD:(FG + g + 1) * FGD] = r[:, FGD:].astype(o_ref.dtype)


def dft_channels(proj, wc, row0, nrows):
    B = proj.shape[0]
    off = row0 // ROW_T
    return pl.pallas_call(
        _dftc_kernel,
        out_shape=jax.ShapeDtypeStruct((B, nrows, 2 * FG * FGD), BF16),
        grid=(B, nrows // ROW_T),
        in_specs=[pl.BlockSpec((None, ROW_T, FG * FGD), lambda b, j: (b, j + off, C_F // (FG * FGD))),
                  pl.BlockSpec((FGD, 2 * FGD), lambda b, j: (0, 0))],
        out_specs=pl.BlockSpec((None, ROW_T, 2 * FG * FGD), lambda b, j: (b, j, 0)),
        compiler_params=_cp(2, 32), name="dft_channels",
    )(proj, wc)


def _dftp_kernel(c_ref, s_ref, y1_ref, y2_ref, o_ref, acc_ref):
    k = pl.program_id(2)

    @pl.when(k == 0)
    def _():
        acc_ref[...] = jnp.zeros_like(acc_ref)

    acc_ref[...] += _dot(c_ref[...], y1_ref[...]) + _dot(s_ref[...], y2_ref[...])

    @pl.when(k == pl.num_programs(2) - 1)
    def _():
        o_ref[...] = acc_ref[...].astype(o_ref.dtype)


def dft_positions(ct, st, y):
    B, n, w2 = y.shape
    w = w2 // 2
    t = _pick(n, (1024, 512, 256))
    return pl.pallas_call(
        _dftp_kernel,
        out_shape=jax.ShapeDtypeStruct((B, n, w), BF16),
        grid=(B, n // t, n // t),
        in_specs=[pl.BlockSpec((t, t), lambda b, i, k: (i, k)),
                  pl.BlockSpec((t, t), lambda b, i, k: (i, k)),
                  pl.BlockSpec((None, t, w), lambda b, i, k: (b, k, 0)),
                  pl.BlockSpec((None, t, w), lambda b, i, k: (b, k, 1))],
        out_specs=pl.BlockSpec((None, t, w), lambda b, i, k: (b, i, 0)),
        scratch_shapes=[pltpu.VMEM((t, w), F32)],
        compiler_params=_cp(3, 40), name="dft_positions",
    )(ct, st, y, y)


def _log_sigmoid(x):
    return jnp.minimum(x, 0.0) - jnp.log(1.0 + jnp.exp(-jnp.abs(x)))


def _gate_kernel(g_ref, b_ref, o_ref):
    L = ML_L
    g = g_ref[...] + b_ref[...]
    lane = lax.broadcasted_iota(jnp.int32, (L, LANE), 1)
    is_f = (lane & ML_H) == ML_H
    is_bwd = (lane & (2 * ML_H)) == 2 * ML_H
    lf = jnp.where(is_f, _log_sigmoid(g), 0.0)
    it_r = lax.broadcasted_iota(jnp.int32, (L, L), 0)
    it_c = lax.broadcasted_iota(jnp.int32, (L, L), 1)
    hp = lax.Precision.HIGHEST
    cf = jnp.dot(jnp.where(it_c <= it_r, 1.0, 0.0).astype(F32), lf, precision=hp, preferred_element_type=F32)
    cb = jnp.dot(jnp.where(it_c >= it_r, 1.0, 0.0).astype(F32), lf, precision=hp, preferred_element_type=F32)
    o_ref[...] = jnp.where(is_f, jnp.where(is_bwd, cb, cf), g)


def gate_prep(graw, bias):
    B, S, _ = graw.shape
    return pl.pallas_call(
        _gate_kernel,
        out_shape=jax.ShapeDtypeStruct((B, S, LANE), F32),
        grid=(B, S // ML_L),
        in_specs=[pl.BlockSpec((None, ML_L, LANE), lambda b, c: (b, c, 0)),
                  pl.BlockSpec((1, LANE), lambda b, c: (0, 0))],
        out_specs=pl.BlockSpec((None, ML_L, LANE), lambda b, c: (b, c, 0)),
        compiler_params=_cp(2, 16), name="gate_prep",
    )(graw, bias)


def _mlstm_kernel(k_ref, qt_ref, vt_ref, mo_ref, g_ref, gt_ref, ng_ref, o_ref,
                  hf_ref, hb_ref, c_ref, m_ref, *, nctc, nchunk):
    L = ML_L
    c_ref[...] = jnp.zeros_like(c_ref)
    m_ref[...] = jnp.zeros_like(m_ref)
    key_i = lax.broadcasted_iota(jnp.int32, (L, L), 0)
    qry_i = lax.broadcasted_iota(jnp.int32, (L, L), 1)
    tri = (key_i <= qry_i, key_i >= qry_i)
    lane = lax.broadcasted_iota(jnp.int32, (L, LANE), 1)
    head_lanes = (lane < ML_DK, lane >= ML_DK)
    ones_t = jnp.ones((ML_DV, L), BF16)
    h_refs = (hf_ref, hb_ref)

    def step(it, carry):
        stores = []
        for d in range(2):
            if d == 0:
                c = it
            else:
                c = jnp.where(it < nctc, nctc - 1 - it, nchunk - 1 - (it - nctc))
            r0 = pl.multiple_of(c * L, L)
            gc = g_ref[pl.ds(r0, L), :]
            gr = gt_ref[c]
            kp = k_ref[pl.ds(r0, L), :]
            qt = qt_ref[c]
            last = L - 1 if d == 0 else 0
            for hh in range(2):
                icol = d * 4 + hh
                fcol = d * 4 + 2 + hh
                sidx = d * 2 + hh
                key_term = gc[:, icol:icol + 1] - gc[:, fcol:fcol + 1]
                b_row = gr[fcol:fcol + 1, :]
                i_row = gr[icol:icol + 1, :]
                m_prev = m_ref[sidx:sidx + 1, 0:1]
                logd = jnp.where(tri[d], key_term + b_row, NEG)
                inter = b_row + m_prev
                m_t = jnp.maximum(inter, jnp.max(logd, axis=0, keepdims=True))
                km = jnp.where(head_lanes[hh], kp, jnp.zeros_like(kp))
                p_t = (_dot(km, qt) * jnp.exp(logd - m_t)).astype(BF16)
                q_in = (qt.astype(F32) * jnp.exp(inter - m_t)).astype(BF16)
                vext_t = jnp.concatenate([vt_ref[c, hh * ML_DV:(hh + 1) * ML_DV, :], ones_t], axis=0)
                c_t = c_ref[sidx]
                tot = _dot(jnp.concatenate([vext_t, c_t.astype(BF16)], axis=1),
                           jnp.concatenate([p_t, q_in], axis=0))
                h_t = tot[:ML_DV] / jnp.maximum(jnp.abs(tot[ML_DV:]), jnp.exp(-m_t))
                b_last = b_row[:, last:last + 1]
                log_w = b_last - b_row + i_row
                m_new = jnp.maximum(b_last + m_prev, jnp.max(log_w, axis=-1, keepdims=True))
                w_row = jnp.exp(log_w - m_new).astype(BF16)
                c_new = jnp.exp(b_last + m_prev - m_new) * c_t + _dot(vext_t * w_row, km)
                stores.append((d, hh, sidx, c, h_t, c_new, m_new))
        for d, hh, sidx, c, h_t, c_new, m_new in stores:
            h_refs[d][c, hh] = h_t
            c_ref[sidx] = c_new
            m_ref[sidx:sidx + 1, :] = jnp.broadcast_to(m_new, (1, LANE))
        return carry

    lax.fori_loop(0, nchunk, step, 0)

    def out_step(i, carry):
        r0 = pl.multiple_of(i * L, L)
        for hh in range(2):
            hs = slice(hh * ML_DV, (hh + 1) * ML_DV)
            h = jnp.transpose(hf_ref[i, hh] + hb_ref[i, hh])
            hn = h * lax.rsqrt(jnp.mean(h * h, axis=-1, keepdims=True) + EPS) * ng_ref[:, hs]
            og = jax.nn.sigmoid(mo_ref[pl.ds(r0, L), hs].astype(F32))
            o_ref[pl.ds(r0, L), hs] = (og * hn).astype(o_ref.dtype)
        return carry

    lax.fori_loop(0, nchunk, out_step, 0)


def mlstm(proj, qv_t, gates, gates_t, norm_g, lc):
    B, S, _ = proj.shape
    nchunk = S // ML_L
    pw = 2 * ML_DK
    vw = 2 * ML_DV
    return pl.pallas_call(
        functools.partial(_mlstm_kernel, nctc=lc // ML_L, nchunk=nchunk),
        out_shape=jax.ShapeDtypeStruct((B, S, ML_H * ML_DV), BF16),
        grid=(B, ML_H // 2),
        in_specs=[pl.BlockSpec((None, S, pw), lambda b, p: (b, 0, C_MK // pw + p)),
                  pl.BlockSpec((None, nchunk, pw, ML_L), lambda b, p: (b, 0, p, 0)),
                  pl.BlockSpec((None, nchunk, vw, ML_L), lambda b, p: (b, 0, ML_H * ML_DK // vw + p, 0)),
                  pl.BlockSpec((None, S, vw), lambda b, p: (b, 0, C_MO // vw + p)),
                  pl.BlockSpec((None, None, S, LANE), lambda b, p: (b, p, 0, 0)),
                  pl.BlockSpec((None, None, nchunk, 8, ML_L), lambda b, p: (b, p, 0, 0, 0)),
                  pl.BlockSpec((1, vw), lambda b, p: (0, p))],
        out_specs=pl.BlockSpec((None, S, vw), lambda b, p: (b, 0, p)),
        scratch_shapes=[pltpu.VMEM((nchunk, 2, ML_DV, ML_L), F32), pltpu.VMEM((nchunk, 2, ML_DV, ML_L), F32),
                        pltpu.VMEM((4, 2 * ML_DV, 2 * ML_DK), F32), pltpu.VMEM((8, LANE), F32)],
        compiler_params=_cp(2, 48), name="mlstm",
    )(proj, qv_t, qv_t, proj, gates, gates_t, norm_g)


def _merge_kernel(a_ref, f_ref, m_ref, wa_ref, wf_ref, wm_ref, ga_ref, gf_ref, gm_ref,
                  ba_ref, bf_ref, bm_ref, o_ref):
    def br(x_ref, w_ref, g_ref, b_ref):
        return jax.nn.sigmoid(g_ref[...].astype(F32) + b_ref[...]) * _dot(x_ref[...], w_ref[...])

    y = br(a_ref, wa_ref, ga_ref, ba_ref) + br(f_ref, wf_ref, gf_ref, bf_ref) + br(m_ref, wm_ref, gm_ref, bm_ref)
    o_ref[...] = y.astype(o_ref.dtype)


def merge(a, f, m, proj, wa, wf, wm, b_gate):
    R, kin = a.shape
    tm = _pick(R, (512, 256))
    tn = 1024
    x_spec = pl.BlockSpec((tm, kin), lambda j, i: (i, 0))
    w_spec = pl.BlockSpec((kin, tn), lambda j, i: (0, j))
    gp = lambda br: pl.BlockSpec((tm, tn), lambda j, i: (i, (C_GP + br * D) // tn + j))
    bg = lambda br: pl.BlockSpec((None, 1, tn), lambda j, i: (br, 0, j))
    return pl.pallas_call(
        _merge_kernel,
        out_shape=jax.ShapeDtypeStruct((R, D), BF16),
        grid=(D // tn, R // tm),
        in_specs=[x_spec, x_spec, x_spec, w_spec, w_spec, w_spec, gp(0), gp(1), gp(2), bg(0), bg(1), bg(2)],
        out_specs=pl.BlockSpec((tm, tn), lambda j, i: (i, j)),
        compiler_params=_cp(2, 40), name="merge",
    )(a, f, m, wa, wf, wm, proj, proj, proj, b_gate, b_gate, b_gate)


def _pack2(lo, hi):
    lo_b = lax.bitcast_convert_type(lo.astype(BF16).astype(F32), jnp.uint32)
    hi_b = lax.bitcast_convert_type(hi.astype(BF16).astype(F32), jnp.uint32)
    return (lo_b >> 16) | (hi_b & jnp.uint32(0xFFFF0000))


def _unpack2(w):
    lo = lax.bitcast_convert_type(w << 16, F32)
    hi = lax.bitcast_convert_type(w & jnp.uint32(0xFFFF0000), F32)
    return lo, hi


def _outproj_router_kernel(y_ref, w_ref, x_ref, g1_ref, g_ref, sc_ref, sh_ref, wh_ref, wl_ref, b_ref,
                           xo_ref, hp_ref, eid_ref, wt_ref):
    x = x_ref[...] + g1_ref[...] * _dot(y_ref[...], w_ref[...])
    xo_ref[...] = x
    y = x * lax.rsqrt(jnp.mean(x * x, axis=-1, keepdims=True) + EPS) * g_ref[...]
    y = y * (1.0 + sc_ref[...]) + sh_ref[...]
    hp_ref[...] = _pack2(y[:, :D // 2], y[:, D // 2:])
    yh, yl = _split(y)
    logits = _dot(yh, wh_ref[...]) + _dot(yl, wh_ref[...]) + _dot(yh, wl_ref[...]) + b_ref[...]
    lane = lax.broadcasted_iota(jnp.int32, logits.shape, 1)
    lanef = lane.astype(F32)
    big = float(LANE)
    glog = jnp.where(lane < N_GRP, logits, NEG)
    gmax = jnp.max(glog, axis=-1, keepdims=True)
    gsel = jnp.min(jnp.where(glog == gmax, lanef, big), axis=-1, keepdims=True)
    pg = 1.0 / jnp.sum(jnp.exp(glog - gmax), axis=-1, keepdims=True)
    lo = N_GRP + EPG * gsel
    el = jnp.where(jnp.logical_and(lanef >= lo, lanef < lo + EPG), logits, NEG)
    v1 = jnp.max(el, axis=-1, keepdims=True)
    i1 = jnp.min(jnp.where(el == v1, lanef, big), axis=-1, keepdims=True)
    el2 = jnp.where(lanef == i1, NEG, el)
    v2 = jnp.max(el2, axis=-1, keepdims=True)
    i2 = jnp.min(jnp.where(el2 == v2, lanef, big), axis=-1, keepdims=True)
    e = jnp.exp(v2 - v1)
    w1 = pg / (1.0 + e)
    w2 = pg * e / (1.0 + e)
    eid_ref[...] = jnp.where(lane == 0, i1 - N_GRP, jnp.where(lane == 1, i2 - N_GRP, 0.0)).astype(jnp.int32)
    wt_ref[...] = jnp.where(lane == 0, w1, jnp.where(lane == 1, w2, 0.0))


def out_proj_router(y, w_out, xa, g, mod3, k_gate, k_sh, k_sc, nct, wr_hi, wr_lo, br):
    B, S, _ = xa.shape
    tile = lambda w: pl.BlockSpec((None, ROW_T, w), lambda b, j: (b, j, 0))
    const = lambda shape: pl.BlockSpec(shape, lambda b, j: (0, 0))
    return pl.pallas_call(
        _outproj_router_kernel,
        out_shape=(jax.ShapeDtypeStruct((B, S, D), F32),
                   jax.ShapeDtypeStruct((B, S, D // 2), jnp.uint32),
                   jax.ShapeDtypeStruct((B, S, LANE), jnp.int32),
                   jax.ShapeDtypeStruct((B, S, LANE), F32)),
        grid=(B, S // ROW_T),
        in_specs=[tile(D), const((D, D)), tile(D), _mod_spec(k_gate, nct, B),
                  const((1, D)), _mod_spec(k_sc, nct, B), _mod_spec(k_sh, nct, B),
                  const((D, LANE)), const((D, LANE)), const((1, LANE))],
        out_specs=(tile(D), tile(D // 2), tile(LANE), tile(LANE)),
        compiler_params=_cp(2, 48), name="out_proj_router",
    )(y, w_out, xa, mod3, g.reshape(1, D), mod3, mod3, wr_hi, wr_lo, br)


DMA_UNROLL = 8


def _expert_kernel(tok_ref, order_ref, rstart_ref, nvalid_ref, be_ref, nexte_ref, nused_ref,
                   x_hbm, w1_hbm, w3_hbm, w2_hbm, y_hbm,
                   xbuf, ybuf, st1, st3, st2, wb1, wb3, wb2, gsem, ssem, wsem, *, layer):
    i = pl.program_id(0)
    nused = nused_ref[0]
    last = pl.num_programs(0) - 1
    used = i < nused
    weights = ((w1_hbm, st1, wb1), (w3_hbm, st3, wb3), (w2_hbm, st2, wb2))

    def gather_rows(blk):
        slot = blk & 1
        rs = rstart_ref[blk]

        def group(k, carry):
            for u in range(DMA_UNROLL):
                src = x_hbm.at[pl.ds(tok_ref[rs + k * DMA_UNROLL + u], 1), :]
                pltpu.make_async_copy(src, xbuf.at[slot, k, pl.ds(u, 1), :], gsem.at[slot]).start(priority=u % 2)
            return carry

        lax.fori_loop(0, MOE_T // DMA_UNROLL, group, 0)

    def scatter_rows(blk, wait):
        slot = blk & 1
        rs = rstart_ref[blk]
        nv = nvalid_ref[blk]

        def row(k, u, r, prio):
            dst = y_hbm.at[pl.ds(0 if wait else order_ref[rs + r], 1), :]
            cp = pltpu.make_async_copy(ybuf.at[slot, k, pl.ds(u, 1), :], dst, ssem.at[slot])
            if wait:
                cp.wait()
            else:
                cp.start(priority=prio)

        def group(k, carry):
            if wait:
                pltpu.make_async_copy(ybuf.at[slot, k], y_hbm.at[pl.ds(0, DMA_UNROLL), :], ssem.at[slot]).wait()
            else:
                for u in range(DMA_UNROLL):
                    row(k, u, k * DMA_UNROLL + u, u % 2)
            return carry

        def tail(r, carry):
            row(lax.shift_right_logical(r, 3), r & (DMA_UNROLL - 1), r, 0)
            return carry

        full = nv // DMA_UNROLL
        lax.fori_loop(0, full, group, 0)
        lax.fori_loop(full * DMA_UNROLL, nv, tail, 0)

    def fetch_weights(e):
        for k, (hbm, st, _) in enumerate(weights):
            pltpu.make_async_copy(hbm.at[layer, e], st, wsem.at[k]).start()

    @pl.when(i == 0)
    def _():
        fetch_weights(be_ref[0])
        gather_rows(0)

    @pl.when(i + 1 < nused)
    def _():
        gather_rows(i + 1)

    @pl.when(jnp.logical_and(i >= 2, i - 2 < nused))
    def _():
        scatter_rows(i - 2, True)

    @pl.when(used)
    def _():
        e = be_ref[i]

        @pl.when(jnp.logical_or(i == 0, be_ref[jnp.maximum(i - 1, 0)] != e))
        def _():
            for k, (hbm, st, wb) in enumerate(weights):
                pltpu.make_async_copy(hbm.at[layer, 0], st, wsem.at[k]).wait()
                wb[...] = st[...].astype(BF16)

            @pl.when(nexte_ref[i] >= 0)
            def _():
                fetch_weights(nexte_ref[i])

        slot = i & 1
        pltpu.make_async_copy(xbuf.at[slot], xbuf.at[slot], gsem.at[slot]).wait()
        xl, xh = _unpack2(xbuf[slot].reshape(MOE_T, D // 2))
        xl = xl.astype(BF16)
        xh = xh.astype(BF16)
        half = D // 2
        a = _dot(xl, wb1[:half, :]) + _dot(xh, wb1[half:, :])
        b = _dot(xl, wb3[:half, :]) + _dot(xh, wb3[half:, :])
        hmid = (a * jax.nn.sigmoid(a) * b).astype(BF16)
        y = _dot(hmid, wb2[...])
        ybuf[slot] = _pack2(y[:, :half], y[:, half:]).reshape(MOE_T // DMA_UNROLL, DMA_UNROLL, half)
        scatter_rows(i, False)

    @pl.when(i == last)
    def _():
        @pl.when(jnp.logical_and(i >= 1, i - 1 < nused))
        def _():
            scatter_rows(i - 1, True)

        @pl.when(used)
        def _():
            scatter_rows(i, True)


def expert_ffn(hp, plan, w1, w3, w2, layer, n_blocks, n_pairs):
    w = hp.shape[1]
    hbm = pl.BlockSpec(memory_space=pl.ANY)
    grid_spec = pltpu.PrefetchScalarGridSpec(
        num_scalar_prefetch=7, grid=(n_blocks,),
        in_specs=[hbm, hbm, hbm, hbm],
        out_specs=hbm,
        scratch_shapes=[pltpu.VMEM((2, MOE_T // DMA_UNROLL, DMA_UNROLL, w), jnp.uint32),
                        pltpu.VMEM((2, MOE_T // DMA_UNROLL, DMA_UNROLL, w), jnp.uint32),
                        pltpu.VMEM((D, FF), F32), pltpu.VMEM((D, FF), F32), pltpu.VMEM((FF, D), F32),
                        pltpu.VMEM((D, FF), BF16), pltpu.VMEM((D, FF), BF16), pltpu.VMEM((FF, D), BF16),
                        pltpu.SemaphoreType.DMA((2,)), pltpu.SemaphoreType.DMA((2,)),
                        pltpu.SemaphoreType.DMA((3,))])
    return pl.pallas_call(
        functools.partial(_expert_kernel, layer=layer),
        out_shape=jax.ShapeDtypeStruct((n_pairs, w), jnp.uint32),
        grid_spec=grid_spec, compiler_params=_cp(1, 56), name="moe_experts",
    )(*plan, hp, w1, w3, w2)


def _combine(ya_ref, yb_ref, x_ref, wt_ref, g_ref):
    w0 = wt_ref[:, 0:1]
    w1 = wt_ref[:, 1:2]
    al, ah = _unpack2(ya_ref[...])
    bl, bh = _unpack2(yb_ref[...])
    half = D // 2
    g = g_ref[...]
    return (x_ref[:, :half] + g[:, :half] * (w0 * al + w1 * bl),
            x_ref[:, half:] + g[:, half:] * (w0 * ah + w1 * bh))


def _combine_kernel(ya_ref, yb_ref, x_ref, wt_ref, g_ref, o_ref):
    lo, hi = _combine(ya_ref, yb_ref, x_ref, wt_ref, g_ref)
    o_ref[:, :D // 2] = lo
    o_ref[:, D // 2:] = hi


def _combine_norm_kernel(ya_ref, yb_ref, x_ref, wt_ref, g_ref, ng_ref, sc_ref, sh_ref, o_ref, h_ref):
    lo, hi = _combine(ya_ref, yb_ref, x_ref, wt_ref, g_ref)
    o_ref[:, :D // 2] = lo
    o_ref[:, D // 2:] = hi
    x = o_ref[...]
    y = x * lax.rsqrt(jnp.mean(x * x, axis=-1, keepdims=True) + EPS) * ng_ref[...]
    h_ref[...] = (y * (1.0 + sc_ref[...]) + sh_ref[...]).astype(h_ref.dtype)


def combine(y2, xa, wts, mod3, k_gate, nct, latent_only):
    B, S, _ = xa.shape
    w = y2.shape[-1]
    off = nct if latent_only else 0
    tile = lambda wd: pl.BlockSpec((None, ROW_T, wd), lambda b, j: (b, j + off, 0))
    pick = lambda k: pl.BlockSpec((None, None, ROW_T, w), lambda b, j: (k, b, j + off, 0))
    return pl.pallas_call(
        _combine_kernel,
        out_shape=jax.ShapeDtypeStruct((B, S - off * ROW_T, D), F32),
        grid=(B, S // ROW_T - off),
        in_specs=[pick(0), pick(1), tile(D), tile(LANE), _mod_spec(k_gate, nct, B, off)],
        out_specs=pl.BlockSpec((None, ROW_T, D), lambda b, j: (b, j, 0)),
        compiler_params=_cp(2, 32), name="moe_combine",
    )(y2, y2, xa, wts, mod3)


def combine_norm(y2, xa, wts, mod3, k_gate, nct, g_next, mod3_next, k_sh, k_sc):
    B, S, _ = xa.shape
    w = y2.shape[-1]
    tile = lambda wd: pl.BlockSpec((None, ROW_T, wd), lambda b, j: (b, j, 0))
    pick = lambda k: pl.BlockSpec((None, None, ROW_T, w), lambda b, j: (k, b, j, 0))
    return pl.pallas_call(
        _combine_norm_kernel,
        out_shape=(jax.ShapeDtypeStruct((B, S, D), F32), jax.ShapeDtypeStruct((B, S, D), BF16)),
        grid=(B, S // ROW_T),
        in_specs=[pick(0), pick(1), tile(D), tile(LANE), _mod_spec(k_gate, nct, B),
                  pl.BlockSpec((1, D), lambda b, j: (0, 0)), _mod_spec(k_sc, nct, B), _mod_spec(k_sh, nct, B)],
        out_specs=(tile(D), tile(D)),
        compiler_params=_cp(2, 32), name="moe_combine_norm",
    )(y2, y2, xa, wts, mod3, g_next.reshape(1, D), mod3_next, mod3_next)


def moe_plan(eid, n_blocks, n_tok):
    order = jnp.argsort(eid).astype(jnp.int32)
    tok = jnp.pad(jnp.where(order >= n_tok, order - n_tok, order), (0, MOE_T))
    experts = jnp.arange(N_EXP, dtype=jnp.int32)
    counts = jnp.sum((eid[:, None] == experts[None, :]).astype(jnp.int32), axis=0)
    nblk_e = (counts + MOE_T - 1) // MOE_T
    bend = jnp.cumsum(nblk_e)
    bstart = bend - nblk_e
    start = jnp.cumsum(counts) - counts
    blk = jnp.arange(n_blocks, dtype=jnp.int32)
    owner = jnp.logical_and(blk[:, None] >= bstart[None, :], blk[:, None] < bend[None, :]).astype(jnp.int32)
    take = lambda v: jnp.sum(owner * v[None, :], axis=1)
    within = blk - take(bstart)
    nused = bend[-1]
    last_e = jnp.max(jnp.where(counts > 0, experts, 0))
    block_e = jnp.where(blk < nused, take(experts), last_e).astype(jnp.int32)
    rstart = jnp.where(blk < nused, take(start) + within * MOE_T, 0).astype(jnp.int32)
    nvalid = jnp.where(blk < nused, jnp.clip(take(counts) - within * MOE_T, 0, MOE_T), 0).astype(jnp.int32)
    nxt = take(bend)
    e_at_nxt = jnp.sum((blk[None, :] == nxt[:, None]).astype(jnp.int32) * block_e[None, :], axis=1)
    next_e = jnp.where(jnp.logical_and(blk < nused, nxt < nused), e_at_nxt, -1).astype(jnp.int32)
    return tok, order, rstart, nvalid, block_e, next_e, nused.astype(jnp.int32).reshape(1)


def _rope_tables(s, lc):
    t = np.arange(s - lc)
    row = (t // GRID_W).astype(np.float64)
    col = (t % GRID_W).astype(np.float64)
    nf = HD // 4
    inv = np.float32(ROPE_BASE) ** (-np.arange(nf, dtype=np.float32) / np.float32(nf))
    ang = np.concatenate([row[:, None] * inv, col[:, None] * inv], axis=-1)
    ang = np.concatenate([np.zeros((lc, HD // 2)), ang], axis=0)
    cos = np.repeat(np.cos(ang), 2, axis=-1)
    sin = np.repeat(np.sin(ang), 2, axis=-1) * np.tile([-1.0, 1.0], HD // 2)
    return jnp.asarray(np.tile(cos, (1, LANE // HD)), F32), jnp.asarray(np.tile(sin, (1, LANE // HD)), F32)


def _chan_dft():
    j = np.arange(FGD)
    ang = 2.0 * np.pi * ((j[:, None] * j[None, :]) % FGD) / FGD
    return jnp.asarray(np.concatenate([np.cos(ang), -np.sin(ang)], axis=1) / np.sqrt(FGD), BF16)


def _pos_dft(n):
    k = np.arange(n)
    ang = 2.0 * np.pi * ((k[:, None] * k[None, :]) % n) / n
    return (jnp.asarray(np.cos(ang) / np.sqrt(n), BF16), jnp.asarray(np.sin(ang) / np.sqrt(n), BF16))


def _proj_weights(w_in):
    w_big = jnp.concatenate([w_in[:, 0:1024], w_in[:, 4640:], w_in[:, 2048:2560], w_in[:, 3584:4608],
                             w_in[:, 1024:1536]], axis=1).astype(BF16)
    w_gate = jnp.pad(w_in[:, 4608:4640], ((0, 0), (0, LANE - 32))).astype(BF16)
    return w_big, w_gate


WT_T = 512


def _wt_kernel(w_ref, o_ref):
    scale = jnp.where(pl.program_id(0) == 0, ML_DK ** -0.5, 1.0)
    o_ref[...] = (jnp.transpose(w_ref[...].astype(F32)) * scale).astype(o_ref.dtype)


def qv_weights_t(w_in):
    w_qv = jnp.concatenate([w_in[:, 1536:2048], w_in[:, 2560:3584]], axis=1).astype(BF16)
    return pl.pallas_call(
        _wt_kernel,
        out_shape=jax.ShapeDtypeStruct((QV_ROWS, D), BF16),
        grid=(QV_ROWS // WT_T, D // WT_T),
        in_specs=[pl.BlockSpec((WT_T, WT_T), lambda n, k: (k, n))],
        out_specs=pl.BlockSpec((WT_T, WT_T), lambda n, k: (n, k)),
        compiler_params=_cp(2, 16), name="qv_weights_t",
    )(w_qv)


def _qk_gain(qg, kg):
    qs = jnp.tile(qg * (HD ** -0.5 * LOG2E), N_HEADS)
    return jnp.concatenate([qs, jnp.tile(kg, N_KV)]).reshape(1, -1).astype(F32)


def kernel(x, c, ctx, c_ctx, w_mod, b_mod, norm1_g, norm2_g, w_in, q_norm_g, k_norm_g, attn_sink, ml_gate_b,
           ml_norm_g, w_br_attn, w_br_four, w_br_mlstm, b_gate, w_out, w_grp, b_grp, w_exp_router,
           b_exp_router, w1, w3, w2):
    B, T, _ = x.shape
    lc = ctx.shape[1]
    S = lc + T
    R = B * S
    depth = w_mod.shape[0]
    nct = lc // ROW_T
    assert lc % ROW_T == 0 and T % ROW_T == 0 and B < 8

    xa = jnp.concatenate([ctx, x], axis=1)
    cvec = jnp.concatenate([c, c_ctx[None], jnp.zeros((7 - B, D), F32)], axis=0)
    cos_t, sin_t = _rope_tables(S, lc)
    bd = jnp.asarray(np.kron(np.eye(LANE // HD), np.ones((HD, HD))), BF16)
    wc = _chan_dft()
    ct_lat, st_lat = _pos_dft(T)
    ct_ctx, st_ctx = _pos_dft(lc)
    nchunk = S // ML_L
    n_pairs = R * 2
    n_blocks = -(-n_pairs // MOE_T) + N_EXP

    mods = [modulation(cvec, w_mod, b_mod, l).reshape(8, 1, 6 * D) for l in range(depth)]
    h = norm_mod(xa, norm1_g[0], mods[0], 0, 1, nct)
    for l in range(depth):
        mod3 = mods[l]
        w_big, w_gate = _proj_weights(w_in[l])
        w_qv_t = qv_weights_t(w_in[l])
        h2d = h.reshape(R, D)
        proj = mm(h2d, w_big, BF16, 2048).reshape(B, S, C_TOT)
        graw = mm(h2d, w_gate, F32, LANE).reshape(B, S, LANE)

        qkn, vdup = qk_prep(proj, _qk_gain(q_norm_g[l], k_norm_g[l]), cos_t, sin_t, bd)
        a = attention(qkn, vdup, attn_sink[l].astype(F32) * LOG2E, lc)

        f_ctx = dft_positions(ct_ctx, st_ctx, dft_channels(proj, wc, 0, lc))
        f_lat = dft_positions(ct_lat, st_lat, dft_channels(proj, wc, lc, T))
        f = jnp.concatenate([f_ctx, f_lat], axis=1)

        gbias = jnp.pad(ml_gate_b[l].reshape(1, 4 * ML_H).astype(F32), ((0, 0), (0, LANE - 4 * ML_H)))
        gproc = gate_prep(graw, gbias)[..., :4 * ML_H]
        g5 = gproc.reshape(B, S, 2, 2, ML_H // 2, 2).transpose(0, 4, 1, 2, 3, 5).reshape(B, ML_H // 2, S, 8)
        gates = jnp.pad(g5, ((0, 0), (0, 0), (0, 0), (0, LANE - 8)))
        gates_t = g5.reshape(B, ML_H // 2, nchunk, ML_L, 8).transpose(0, 1, 2, 4, 3)
        qv_t = mm_chunk_transposed(h, w_qv_t, ML_L)
        m = mlstm(proj, qv_t, gates, gates_t, ml_norm_g[l].reshape(1, -1).astype(F32), lc)

        y = merge(a.reshape(R, -1), f.reshape(R, -1), m.reshape(R, -1), proj.reshape(R, C_TOT),
                  w_br_attn[l].astype(BF16), w_br_four[l].astype(BF16), w_br_mlstm[l].astype(BF16),
                  b_gate[l].reshape(3, 1, D).astype(F32))
        wr = jnp.concatenate([w_grp[l], w_exp_router[l].reshape(D, N_EXP)], axis=1)
        wr = jnp.pad(wr, ((0, 0), (0, LANE - N_GRP - N_EXP))).astype(F32)
        wr_hi = wr.astype(BF16)
        wr_lo = (wr - wr_hi.astype(F32)).astype(BF16)
        br = jnp.pad(jnp.concatenate([b_grp[l], b_exp_router[l].reshape(N_EXP)]),
                     (0, LANE - N_GRP - N_EXP)).reshape(1, LANE).astype(F32)
        xa, hp, eid, wts = out_proj_router(y.reshape(B, S, D), w_out[l].astype(BF16), xa, norm2_g[l], mod3, 2, 3, 4,
                                           nct, wr_hi, wr_lo, br)
        plan = moe_plan(jnp.concatenate([eid[..., 0].reshape(R), eid[..., 1].reshape(R)]), n_blocks, R)
        y2 = expert_ffn(hp.reshape(R, D // 2), plan, w1, w3, w2, l, n_blocks, n_pairs).reshape(2, B, S, D // 2)
        if l == depth - 1:
            xa = combine(y2, xa, wts, mod3, 5, nct, True)
        else:
            xa, h = combine_norm(y2, xa, wts, mod3, 5, nct, norm1_g[l + 1], mods[l + 1], 0, 1)

    return xa
```
